```python
import math
import jax
import jax.numpy as jnp
from jax import lax
import numpy as np

D_MODEL = 4096
BATCH = 16
SEQ = 256
DEPTH = 2
DEC_BATCH = 2
DEC_SEQ = 2048
PAST_LEN = 512

GRID_W = 64
ROPE_THETA = 10000.0
QBLOCK = 128
EPS = 1e-6
NEG_INF = -1e30

N_EVEN = (DEPTH + 1) // 2
N_ODD = DEPTH // 2
HEAD_DIM = 128

MLA_HEADS = 16
MLA_Q_RANK = 768
MLA_KV_RANK = 512
MLA_NOPE = 128
MLA_ROPE = 64
MLA_V = 128
MLA_QK = MLA_NOPE + MLA_ROPE
WIN_HEADS = 16
WIN_KV = 4
WIN_GROUP = WIN_HEADS // WIN_KV
WINDOW = 128
DIFF_HEADS = 16
DIFF_KV = 4
DIFF_GROUP = DIFF_HEADS // DIFF_KV
DIFF_D = HEAD_DIM // 2
AX_HEADS = 16
AX_KV = 4
AX_GROUP = AX_HEADS // AX_KV

EVEN_IN = MLA_Q_RANK + MLA_KV_RANK + MLA_ROPE + (WIN_HEADS + 2 * WIN_KV) * HEAD_DIM
EVEN_OUT = MLA_HEADS * MLA_V + WIN_HEADS * HEAD_DIM
ODD_IN = (DIFF_HEADS + 2 * DIFF_KV) * HEAD_DIM + (AX_HEADS + 2 * AX_KV) * HEAD_DIM
ODD_OUT = DIFF_HEADS * HEAD_DIM + AX_HEADS * HEAD_DIM

N_EXPERTS = 64
N_EXPERT_GROUPS = 8
TOPK_GROUPS = 4
TOP_K = 8
EXPERT_FF = 256
SHARED_FF = 512
ROUTED_SCALE = 2.5

kernel_name = 'hybrid_diffusion_prefix_step'


def rms_norm(x, g):
    xf = x.astype(jnp.float32)
    y = xf * lax.rsqrt(jnp.mean(xf * xf, axis=-1, keepdims=True) + EPS)
    return (y * g.astype(jnp.float32)).astype(x.dtype)


def rope_2d(n_tok, rot_dim):
    rows = n_tok // GRID_W
    row = jnp.broadcast_to(jnp.arange(rows, dtype=jnp.float32)[:, None], (rows, GRID_W)).reshape(-1)
    col = jnp.broadcast_to(jnp.arange(GRID_W, dtype=jnp.float32)[None, :], (rows, GRID_W)).reshape(-1)
    n_freq = rot_dim // 4
    inv = ROPE_THETA ** (-jnp.arange(n_freq, dtype=jnp.float32) / n_freq)
    ang = jnp.concatenate([row[:, None] * inv, col[:, None] * inv], axis=-1)
    return (jnp.cos(ang), jnp.sin(ang))


def apply_rope(x, cos, sin):
    shp = (cos.shape[0],) + (1,) * (x.ndim - 3) + (cos.shape[-1],)
    c = cos.reshape(shp).astype(x.dtype)
    s = sin.reshape(shp).astype(x.dtype)
    half = x.shape[-1] // 2
    x1, x2 = x[..., :half], x[..., half:]
    return jnp.concatenate([x1 * c - x2 * s, x2 * c + x1 * s], axis=-1)


def rope_tail(x, cos, sin):
    r = 2 * cos.shape[-1]
    return jnp.concatenate([x[..., :-r], apply_rope(x[..., -r:], cos, sin)], axis=-1)


def attend(q, k, v, sink=None):
    b, s, nkv, g, dq = q.shape
    nb = s // QBLOCK
    scale = dq ** -0.5
    qb = jnp.moveaxis(q.reshape(b, nb, QBLOCK, nkv, g, dq), 1, 0)

    def one_block(qblk):
        logits = jnp.einsum('bqkgd,btkd->bkgqt', qblk, k, preferred_element_type=jnp.float32) * scale
        if sink is not None:
            sk = jnp.broadcast_to(sink.astype(jnp.float32)[None, :, :, None, None], logits.shape[:-1] + (1,))
            logits = jnp.concatenate([logits, sk], axis=-1)
        p = jax.nn.softmax(logits, axis=-1)
        if sink is not None:
            p = p[..., :-1]
        return jnp.einsum('bkgqt,btkd->bqkgd', p.astype(v.dtype), v)

    out = lax.map(one_block, qb)
    return jnp.moveaxis(out, 0, 1).reshape(b, s, nkv, g, v.shape[-1])


def window_attend(q, k, v, k_ctx, v_ctx, sink):
    b, s, nkv, g, d = q.shape
    nb = s // WINDOW
    scale = d ** -0.5
    pad = ((0, 0), (WINDOW, WINDOW), (0, 0), (0, 0))
    kp = jnp.pad(k, pad).reshape(b, nb + 2, WINDOW, nkv, d)
    vp = jnp.pad(v, pad).reshape(b, nb + 2, WINDOW, nkv, d)
    kw = jnp.concatenate([kp[:, :-2], kp[:, 1:-1], kp[:, 2:]], axis=2)
    vw = jnp.concatenate([vp[:, :-2], vp[:, 1:-1], vp[:, 2:]], axis=2)
    qb = q.reshape(b, nb, WINDOW, nkv, g, d)
    s_loc = jnp.einsum('bnqkgd,bnmkd->bnkgqm', qb, kw, preferred_element_type=jnp.float32) * scale
    r = jnp.arange(WINDOW)[:, None]
    m = jnp.arange(3 * WINDOW)[None, :]
    rel = m - r
    j = jnp.arange(nb)[:, None, None] * WINDOW - WINDOW + m[None]
    valid = (rel >= 0)[None] & (rel <= 2 * WINDOW)[None] & (j >= 0) & (j < s)
    s_loc = jnp.where(valid[None, :, None, None], s_loc, NEG_INF)
    s_ctx = jnp.einsum('bnqkgd,btkd->bnkgqt', qb, k_ctx, preferred_element_type=jnp.float32) * scale
    sk = jnp.broadcast_to(sink.astype(jnp.float32)[None, None, :, :, None, None], s_loc.shape[:-1] + (1,))
    p = jax.nn.softmax(jnp.concatenate([s_loc, s_ctx, sk], axis=-1), axis=-1)
    w3 = 3 * WINDOW
    lc = k_ctx.shape[1]
    p_loc = p[..., :w3].astype(v.dtype)
    p_ctx = p[..., w3:w3 + lc].astype(v.dtype)
    out = (jnp.einsum('bnkgqm,bnmkd->bnqkgd', p_loc, vw)
           + jnp.einsum('bnkgqt,btkd->bnqkgd', p_ctx, v_ctx))
    return out.reshape(b, s, nkv, g, d)


def mla_keys_values(c_kv, k_rope, w_kvb, g_kn):
    b, t, _ = c_kv.shape
    kv = jnp.dot(c_kv, w_kvb).reshape(b, t, MLA_HEADS, MLA_NOPE + MLA_V)
    k_pe = jnp.broadcast_to(k_rope[:, :, None, :], (b, t, MLA_HEADS, MLA_ROPE))
    k = rms_norm(jnp.concatenate([kv[..., :MLA_NOPE], k_pe], axis=-1), g_kn)
    return k, kv[..., MLA_NOPE:]


def even_mixer(h, ctx, rope, p):
    (w_in, g_q, w_qb, g_kv, w_kvb, g_qn, g_kn, g_wq, g_wk, sink, w_out) = p
    b, s, _ = h.shape
    o1 = MLA_Q_RANK
    o2 = o1 + MLA_KV_RANK
    o3 = o2 + MLA_ROPE
    o4 = o3 + WIN_HEADS * HEAD_DIM
    o5 = o4 + WIN_KV * HEAD_DIM
    proj = jnp.dot(h, w_in)
    q_mla = rms_norm(jnp.dot(rms_norm(proj[..., :o1], g_q), w_qb).reshape(b, s, MLA_HEADS, MLA_QK), g_qn)
    c_kv = rms_norm(proj[..., o1:o2], g_kv)
    k_rope = proj[..., o2:o3]
    wq = rms_norm(proj[..., o3:o4].reshape(b, s, WIN_KV, WIN_GROUP, HEAD_DIM), g_wq)
    wk = rms_norm(proj[..., o4:o5].reshape(b, s, WIN_KV, HEAD_DIM), g_wk)
    wv = proj[..., o5:].reshape(b, s, WIN_KV, HEAD_DIM)
    mk, mv = mla_keys_values(c_kv, k_rope, w_kvb, g_kn)
    sink_kg = sink.reshape(WIN_KV, WIN_GROUP)
    if ctx is None:
        a = attend(q_mla[:, :, :, None], mk, mv)
        bw = attend(wq, wk, wv, sink_kg)
        state = (c_kv, k_rope, wk, wv)
    else:
        cos_r, sin_r, cos_h, sin_h = rope
        ckv_c, krope_c, wk_c, wv_c = ctx
        ck, cv = mla_keys_values(ckv_c, krope_c, w_kvb, g_kn)
        q_mla = rope_tail(q_mla, cos_r, sin_r)
        mk = rope_tail(mk, cos_r, sin_r)
        a = attend(q_mla[:, :, :, None], jnp.concatenate([mk, ck], axis=1), jnp.concatenate([mv, cv], axis=1))
        bw = window_attend(apply_rope(wq, cos_h, sin_h), apply_rope(wk, cos_h, sin_h), wv, wk_c, wv_c, sink_kg)
        state = None
    out = jnp.concatenate([a.reshape(b, s, -1), bw.reshape(b, s, -1)], axis=-1)
    return jnp.dot(out, w_out), state


def odd_mixer(h, ctx, rope, p, lam_init):
    (w_in, g_dq, g_dk, lq1, lk1, lq2, lk2, g_sub, g_aq, g_ak, w_out) = p
    b, s, _ = h.shape
    o1 = DIFF_HEADS * HEAD_DIM
    o2 = o1 + DIFF_KV * HEAD_DIM
    o3 = o2 + DIFF_KV * HEAD_DIM
    o4 = o3 + AX_HEADS * HEAD_DIM
    o5 = o4 + AX_KV * HEAD_DIM
    proj = jnp.dot(h, w_in)
    dq = rms_norm(proj[..., :o1].reshape(b, s, DIFF_KV, DIFF_GROUP, 2, DIFF_D), g_dq)
    dk = rms_norm(proj[..., o1:o2].reshape(b, s, DIFF_KV, 2, DIFF_D), g_dk)
    dv = proj[..., o2:o3].reshape(b, s, DIFF_KV, HEAD_DIM)
    aq = rms_norm(proj[..., o3:o4].reshape(b, s, AX_KV, AX_GROUP, HEAD_DIM), g_aq)
    ak = rms_norm(proj[..., o4:o5].reshape(b, s, AX_KV, HEAD_DIM), g_ak)
    av = proj[..., o5:].reshape(b, s, AX_KV, HEAD_DIM)
    if ctx is None:
        state = (dk, dv, ak, av)
        kd, vd, ka, va = dk, dv, ak, av
    else:
        cos_r, sin_r, cos_h, sin_h = rope
        dq = apply_rope(dq, cos_r, sin_r)
        dk = apply_rope(dk, cos_r, sin_r)
        aq = apply_rope(aq, cos_h, sin_h)
        ak = apply_rope(ak, cos_h, sin_h)
        dk_c, dv_c, ak_c, av_c = ctx
        kd = jnp.concatenate([dk, dk_c], axis=1)
        vd = jnp.concatenate([dv, dv_c], axis=1)
        ka = jnp.concatenate([ak, ak_c], axis=1)
        va = jnp.concatenate([av, av_c], axis=1)
        state = None
    o_1 = attend(dq[..., 0, :], kd[..., 0, :], vd)
    o_2 = attend(dq[..., 1, :], kd[..., 1, :], vd)
    f32 = jnp.float32
    lam = (jnp.exp(jnp.sum(lq1.astype(f32) * lk1.astype(f32)))
           - jnp.exp(jnp.sum(lq2.astype(f32) * lk2.astype(f32))) + lam_init)
    od = rms_norm(o_1 - lam.astype(o_1.dtype) * o_2, g_sub) * (1.0 - lam_init)
    oa = attend(aq, ka, va)
    out = jnp.concatenate([od.reshape(b, s, -1), oa.reshape(b, s, -1)], axis=-1)
    return jnp.dot(out, w_out), state


def moe(h, p):
    w_router, b_router, w_gate, w_up, w_down, ws_gate, ws_up, ws_down = p
    shape = h.shape
    t = h.reshape(-1, shape[-1])
    n = t.shape[0]
    scores = jax.nn.sigmoid(jnp.dot(t, w_router, preferred_element_type=jnp.float32))
    biased = scores + b_router.astype(jnp.float32)
    per_group = N_EXPERTS // N_EXPERT_GROUPS
    grp_score = lax.top_k(biased.reshape(n, N_EXPERT_GROUPS, per_group), 2)[0].sum(-1)
    _, g_idx = lax.top_k(grp_score, TOPK_GROUPS)
    g_mask = jax.nn.one_hot(g_idx, N_EXPERT_GROUPS, dtype=jnp.float32).sum(1)
    e_mask = jnp.repeat(g_mask, per_group, axis=1) > 0
    _, e_idx = lax.top_k(jnp.where(e_mask, biased, -jnp.inf), TOP_K)
    w = jnp.take_along_axis(scores, e_idx, axis=1)
    w = w / jnp.sum(w, axis=-1, keepdims=True) * ROUTED_SCALE
    gates = jnp.einsum('nk,nke->ne', w, jax.nn.one_hot(e_idx, N_EXPERTS, dtype=jnp.float32)).astype(t.dtype)
    a = jnp.einsum('nd,edf->nef', t, w_gate)
    u = jnp.einsum('nd,edf->nef', t, w_up)
    routed = jnp.einsum('nef,efd->nd', jax.nn.silu(a) * u * gates[:, :, None], w_down)
    shared = jnp.dot(jax.nn.silu(jnp.dot(t, ws_gate)) * jnp.dot(t, ws_up), ws_down)
    return (routed + shared).reshape(shape)


def modulation(cond, w_mod, b_mod):
    m = jnp.dot(jax.nn.silu(cond), w_mod) + b_mod
    return jnp.split(m, 6, axis=-1)


def ada_norm(x, g, shift, scale):
    return rms_norm(x, g) * (1.0 + scale[:, None]) + shift[:, None]


def setup_inputs(seed: int = 0) -> dict:
    key = jax.random.key(seed)
    keys = iter(jax.random.split(key, 64))

    def nrm(shape, std=1.0):
        return std * jax.random.normal(next(keys), shape, jnp.float32)

    def gain(shape):
        return 1.0 + 0.05 * nrm(shape)

    d = D_MODEL
    ne, no = N_EVEN, N_ODD
    return {
        'x_prompt': nrm((BATCH, SEQ, d)),
        'x_sample': nrm((DEC_BATCH, DEC_SEQ, d)),
        'cache_mla_ckv': nrm((DEC_BATCH, ne, PAST_LEN, MLA_KV_RANK)),
        'cache_mla_krope': nrm((DEC_BATCH, ne, PAST_LEN, MLA_ROPE)),
        'cache_win_k': nrm((DEC_BATCH, ne, PAST_LEN, WIN_KV, HEAD_DIM)),
        'cache_win_v': nrm((DEC_BATCH, ne, PAST_LEN, WIN_KV, HEAD_DIM)),
        'cache_diff_k': nrm((DEC_BATCH, no, PAST_LEN, DIFF_KV, 2, DIFF_D)),
        'cache_diff_v': nrm((DEC_BATCH, no, PAST_LEN, DIFF_KV, HEAD_DIM)),
        'cache_ax_k': nrm((DEC_BATCH, no, PAST_LEN, AX_KV, HEAD_DIM)),
        'cache_ax_v': nrm((DEC_BATCH, no, PAST_LEN, AX_KV, HEAD_DIM)),
        'c': nrm((DEC_BATCH, d)),
        'c_ctx': nrm((d,)),
        'w_mod': nrm((DEPTH, d, 6 * d), 0.5 * d ** -0.5),
        'b_mod': nrm((DEPTH, 6 * d), 0.01),
        'g_norm_mix': gain((DEPTH, d)),
        'g_norm_ffn': gain((DEPTH, d)),
        'w_in_even': nrm((ne, d, EVEN_IN), d ** -0.5),
        'g_mla_q': gain((ne, MLA_Q_RANK)),
        'w_mla_qb': nrm((ne, MLA_Q_RANK, MLA_HEADS * MLA_QK), MLA_Q_RANK ** -0.5),
        'g_mla_kv': gain((ne, MLA_KV_RANK)),
        'w_mla_kvb': nrm((ne, MLA_KV_RANK, MLA_HEADS * (MLA_NOPE + MLA_V)), MLA_KV_RANK ** -0.5),
        'g_mla_qn': gain((ne, MLA_QK)),
        'g_mla_kn': gain((ne, MLA_QK)),
        'g_win_qn': gain((ne, HEAD_DIM)),
        'g_win_kn': gain((ne, HEAD_DIM)),
        'win_sink': nrm((ne, WIN_HEADS)),
        'w_out_even': nrm((ne, EVEN_OUT, d), EVEN_OUT ** -0.5),
        'w_in_odd': nrm((no, d, ODD_IN), d ** -0.5),
        'g_diff_qn': gain((no, DIFF_D)),
        'g_diff_kn': gain((no, DIFF_D)),
        'diff_lq1': nrm((no, DIFF_D), 0.1),
        'diff_lk1': nrm((no, DIFF_D), 0.1),
        'diff_lq2': nrm((no, DIFF_D), 0.1),
        'diff_lk2': nrm((no, DIFF_D), 0.1),
        'g_diff_sub': gain((no, HEAD_DIM)),
        'g_ax_qn': gain((no, HEAD_DIM)),
        'g_ax_kn': gain((no, HEAD_DIM)),
        'w_out_odd': nrm((no, ODD_OUT, d), ODD_OUT ** -0.5),
        'w_router': nrm((DEPTH, d, N_EXPERTS), d ** -0.5),
        'b_router': nrm((DEPTH, N_EXPERTS), 0.01),
        'w_exp_gate': nrm((DEPTH, N_EXPERTS, d, EXPERT_FF), d ** -0.5),
        'w_exp_up': nrm((DEPTH, N_EXPERTS, d, EXPERT_FF), d ** -0.5),
        'w_exp_down': nrm((DEPTH, N_EXPERTS, EXPERT_FF, d), EXPERT_FF ** -0.5),
        'w_sh_gate': nrm((DEPTH, d, SHARED_FF), d ** -0.5),
        'w_sh_up': nrm((DEPTH, d, SHARED_FF), d ** -0.5),
        'w_sh_down': nrm((DEPTH, SHARED_FF, d), SHARED_FF ** -0.5),
    }


def reference(x_prompt, x_sample, cache_mla_ckv, cache_mla_krope, cache_win_k, cache_win_v,
              cache_diff_k, cache_diff_v, cache_ax_k, cache_ax_v, c, c_ctx,
              w_mod, b_mod, g_norm_mix, g_norm_ffn,
              w_in_even, g_mla_q, w_mla_qb, g_mla_kv, w_mla_kvb, g_mla_qn, g_mla_kn,
              g_win_qn, g_win_kn, win_sink, w_out_even,
              w_in_odd, g_diff_qn, g_diff_kn, diff_lq1, diff_lk1, diff_lq2, diff_lk2,
              g_diff_sub, g_ax_qn, g_ax_kn, w_out_odd,
              w_router, b_router, w_exp_gate, w_exp_up, w_exp_down, w_sh_gate, w_sh_up, w_sh_down):
    n_lat = x_sample.shape[1]
    rope_even = rope_2d(n_lat, MLA_ROPE) + rope_2d(n_lat, HEAD_DIM)
    rope_odd = rope_2d(n_lat, DIFF_D) + rope_2d(n_lat, HEAD_DIM)
    yp, ys = x_prompt, x_sample
    s_ckv, s_krope, s_wk, s_wv = [], [], [], []
    s_dk, s_dv, s_ak, s_av = [], [], [], []
    for l in range(DEPTH):
        i = l // 2
        sh1_p, sc1_p, g1_p, sh2_p, sc2_p, g2_p = modulation(c_ctx[None], w_mod[l], b_mod[l])
        sh1_s, sc1_s, g1_s, sh2_s, sc2_s, g2_s = modulation(c, w_mod[l], b_mod[l])
        hp = ada_norm(yp, g_norm_mix[l], sh1_p, sc1_p)
        hs = ada_norm(ys, g_norm_mix[l], sh1_s, sc1_s)
        if l % 2 == 0:
            pe = (w_in_even[i], g_mla_q[i], w_mla_qb[i], g_mla_kv[i], w_mla_kvb[i], g_mla_qn[i],
                  g_mla_kn[i], g_win_qn[i], g_win_kn[i], win_sink[i], w_out_even[i])
            mp, (ckv, krope, wk, wv) = even_mixer(hp, None, None, pe)
            ctx = (cache_mla_ckv[:, i], cache_mla_krope[:, i], cache_win_k[:, i], cache_win_v[:, i])
            ms, _ = even_mixer(hs, ctx, rope_even, pe)
            s_ckv.append(ckv)
            s_krope.append(krope)
            s_wk.append(wk)
            s_wv.append(wv)
        else:
            po = (w_in_odd[i], g_diff_qn[i], g_diff_kn[i], diff_lq1[i], diff_lk1[i], diff_lq2[i],
                  diff_lk2[i], g_diff_sub[i], g_ax_qn[i], g_ax_kn[i], w_out_odd[i])
            lam_init = 0.8 - 0.6 * math.exp(-0.3 * l)
            mp, (dk, dv, ak, av) = odd_mixer(hp, None, None, po, lam_init)
            ctx = (cache_diff_k[:, i], cache_diff_v[:, i], cache_ax_k[:, i], cache_ax_v[:, i])
            ms, _ = odd_mixer(hs, ctx, rope_odd, po, lam_init)
            s_dk.append(dk)
            s_dv.append(dv)
            s_ak.append(ak)
            s_av.append(av)
        yp = yp + g1_p[:, None] * mp
        ys = ys + g1_s[:, None] * ms
        pm = (w_router[l], b_router[l], w_exp_gate[l], w_exp_up[l], w_exp_down[l],
              w_sh_gate[l], w_sh_up[l], w_sh_down[l])
        yp = yp + g2_p[:, None] * moe(ada_norm(yp, g_norm_ffn[l], sh2_p, sc2_p), pm)
        ys = ys + g2_s[:, None] * moe(ada_norm(ys, g_norm_ffn[l], sh2_s, sc2_s), pm)
    return (yp, ys,
            jnp.stack(s_ckv, axis=1), jnp.stack(s_krope, axis=1),
            jnp.stack(s_wk, axis=1), jnp.stack(s_wv, axis=1),
            jnp.stack(s_dk, axis=1), jnp.stack(s_dv, axis=1),
            jnp.stack(s_ak, axis=1), jnp.stack(s_av, axis=1))
```

```python
import functools
import math

import jax
import jax.numpy as jnp
from jax import lax
from jax.experimental import pallas as pl
from jax.experimental.pallas import tpu as pltpu

F32 = jnp.float32
BF16 = jnp.bfloat16

GRID_W = 64
ROPE_THETA = 10000.0
EPS = 1e-6
NEG_INF = -1e30
HEAD_DIM = 128
MLA_HEADS = 16
MLA_Q_RANK = 768
MLA_KV_RANK = 512
MLA_NOPE = 128
MLA_ROPE = 64
MLA_V = 128
MLA_QK = MLA_NOPE + MLA_ROPE
MLA_QK_PAD = 256
WIN_HEADS = 16
WIN_KV = 4
WINDOW = 128
DIFF_HEADS = 16
DIFF_KV = 4
DIFF_D = HEAD_DIM // 2
AX_HEADS = 16
AX_KV = 4
N_EXPERTS = 64
N_EXPERT_GROUPS = 8
TOPK_GROUPS = 4
TOP_K = 8
ROUTED_SCALE = 2.5

V7X_LANES = 128
V7X_VMEM_LIMIT_BYTES = 56 * 1024 * 1024
MM_WEIGHT_TILE_BYTES = 32 * 1024 * 1024
MOE_ROW_TILE = 256


def _cparams(sem):
    return pltpu.CompilerParams(dimension_semantics=sem, vmem_limit_bytes=V7X_VMEM_LIMIT_BYTES)


def _largest_tile(n, cap, step):
    if n <= cap:
        return n
    t = (cap // step) * step
    while t >= step:
        if n % t == 0:
            return t
        t -= step
    raise ValueError(f"no tile for {n} under {cap}")


def _group_index_fn(tm, n_ctx_rows, dec_rows):
    assert n_ctx_rows % tm == 0 and dec_rows % tm == 0
    ctx_tiles = n_ctx_rows // tm
    per_dec = dec_rows // tm

    def fn(i):
        return jnp.where(i < ctx_tiles, 0, 1 + (jnp.maximum(i - ctx_tiles, 0)) // per_dec)

    return fn


def _mm_body(*refs, has_bias, has_resid):
    x_ref, w_ref = refs[0], refs[1]
    idx = 2
    if has_bias:
        bias_ref = refs[idx]
        idx += 1
    if has_resid:
        resid_ref, gate_ref = refs[idx], refs[idx + 1]
        idx += 2
    o_ref, wb_ref = refs[idx], refs[idx + 1]

    @pl.when(pl.program_id(1) == 0)
    def _():
        wb_ref[...] = w_ref[...].astype(BF16)

    acc = jnp.dot(x_ref[...].astype(BF16), wb_ref[...], preferred_element_type=F32)
    if has_bias:
        acc = acc + bias_ref[...]
    if has_resid:
        acc = resid_ref[...] + gate_ref[...] * acc
    o_ref[...] = acc.astype(o_ref.dtype)


def _matmul(x, w, *, out_dtype, bias=None, resid=None, gates=None, group_fn=None, name):
    m, k = x.shape
    n = w.shape[1]
    tm = _largest_tile(m, 512, 8)
    tn_cap = max(V7X_LANES, min(2048, MM_WEIGHT_TILE_BYTES // (k * 10)))
    tn = _largest_tile(n, tn_cap, V7X_LANES)
    grid = (n // tn, m // tm)
    in_specs = [pl.BlockSpec((tm, k), lambda j, i: (i, 0)),
                pl.BlockSpec((k, tn), lambda j, i: (0, j))]
    args = [x, w]
    if bias is not None:
        in_specs.append(pl.BlockSpec((1, tn), lambda j, i: (0, j)))
        args.append(bias)
    if resid is not None:
        gfn = group_fn(tm)
        in_specs.append(pl.BlockSpec((tm, tn), lambda j, i: (i, j)))
        in_specs.append(pl.BlockSpec((None, 1, tn), lambda j, i: (gfn(i), 0, j)))
        args += [resid, gates]
    return pl.pallas_call(
        functools.partial(_mm_body, has_bias=bias is not None, has_resid=resid is not None),
        out_shape=jax.ShapeDtypeStruct((m, n), out_dtype),
        grid=grid,
        in_specs=in_specs,
        out_specs=pl.BlockSpec((tm, tn), lambda j, i: (i, j)),
        scratch_shapes=[pltpu.VMEM((k, tn), BF16)],
        compiler_params=_cparams(("arbitrary", "arbitrary")),
        name=name,
    )(*args)


def _adanorm_body(x_ref, g_ref, shift_ref, scale_ref, *rest, has_router):
    x = x_ref[...]
    y = x * lax.rsqrt(jnp.mean(x * x, axis=-1, keepdims=True) + EPS) * g_ref[...]
    t = y * (1.0 + scale_ref[...]) + shift_ref[...]
    if has_router:
        wr_ref, o_ref, lg_ref = rest
        lg_ref[...] = jnp.dot(t, wr_ref[...], precision=lax.Precision.HIGHEST,
                              preferred_element_type=F32)
    else:
        (o_ref,) = rest
    o_ref[...] = t.astype(o_ref.dtype)


def _ada_norm(x, g, shift, scale, group_fn, *, w_router=None, name):
    n, d = x.shape
    tm = 256
    gfn = group_fn(tm)
    in_specs = [pl.BlockSpec((tm, d), lambda i: (i, 0)),
                pl.BlockSpec((1, d), lambda i: (0, 0)),
                pl.BlockSpec((None, 1, d), lambda i: (gfn(i), 0, 0)),
                pl.BlockSpec((None, 1, d), lambda i: (gfn(i), 0, 0))]
    args = [x, g, shift, scale]
    out_shape = jax.ShapeDtypeStruct((n, d), BF16)
    out_specs = pl.BlockSpec((tm, d), lambda i: (i, 0))
    if w_router is not None:
        e = w_router.shape[1]
        in_specs.append(pl.BlockSpec((d, e), lambda i: (0, 0)))
        args.append(w_router)
        out_shape = (out_shape, jax.ShapeDtypeStruct((n, e), F32))
        out_specs = (out_specs, pl.BlockSpec((tm, e), lambda i: (i, 0)))
    return pl.pallas_call(
        functools.partial(_adanorm_body, has_router=w_router is not None),
        out_shape=out_shape,
        grid=(n // tm,),
        in_specs=in_specs,
        out_specs=out_specs,
        compiler_params=_cparams(("arbitrary",)),
        name=name,
    )(*args)


def _attn_body(*refs, hb, groups, dq, dv, tq, tk, scale, t_new, t_ctx, window, has_sink,
               diff_post_scale):
    it = iter(refs)
    q_ref, kn_ref, vn_ref = next(it), next(it), next(it)
    kc_ref = vc_ref = sink_ref = lam_ref = gsub_ref = None
    if t_ctx:
        kc_ref, vc_ref = next(it), next(it)
    if has_sink:
        sink_ref = next(it)
    diff = diff_post_scale is not None
    if diff:
        lam_ref, gsub_ref = next(it), next(it)
    o_ref = next(it)

    h = pl.program_id(1)
    i = pl.program_id(2)
    nstack = 2 * groups if diff else groups
    rows = nstack * tq

    def stack_rows(parts):
        return parts[0] if len(parts) == 1 else jnp.concatenate(parts, axis=0)

    if window is not None:
        wk = min(t_new, tq + 2 * window)
        if wk == t_new:
            wstart = 0
        else:
            wstart = pl.multiple_of(jnp.clip(i * tq - window, 0, t_new - wk), V7X_LANES)
        qi = i * tq + lax.broadcasted_iota(jnp.int32, (tq, wk), 0)
        kj = wstart + lax.broadcasted_iota(jnp.int32, (tq, wk), 1)
        wmask = stack_rows([jnp.abs(qi - kj) <= window] * nstack)

    for j in range(hb):
        qj = q_ref[:, j * groups * dq:(j + 1) * groups * dq]
        parts = [qj[:, g * dq:(g + 1) * dq] for g in range(groups)]
        if diff:
            lo = lax.broadcasted_iota(jnp.int32, (tq, dq), 1) < dq // 2
            zero = jnp.zeros((tq, dq), qj.dtype)
            parts = [jnp.where(lo, p, zero) for p in parts] + [jnp.where(lo, zero, p) for p in parts]
        qs = stack_rows(parts)

        if has_sink:
            sinks = []
            for g in range(groups):
                hh = (h * hb + j) * groups + g
                sinks.append(jnp.broadcast_to(sink_ref[pl.ds(hh, 1), :][:, :1], (tq, 1)))
            m = stack_rows(sinks)
            l = jnp.ones((rows, 1), F32)
        else:
            m = jnp.full((rows, 1), NEG_INF, F32)
            l = jnp.zeros((rows, 1), F32)
        acc = jnp.zeros((rows, dv), F32)

        def step(carry, kc, vc, mask):
            m, l, acc = carry
            s = lax.dot_general(qs, kc, (((1,), (1,)), ((), ())), preferred_element_type=F32) * scale
            if mask is not None:
                s = jnp.where(mask, s, NEG_INF)
            m_new = jnp.maximum(m, jnp.max(s, axis=-1, keepdims=True))
            alpha = jnp.exp(m - m_new)
            p = jnp.exp(s - m_new)
            l = alpha * l + jnp.sum(p, axis=-1, keepdims=True)
            acc = alpha * acc + jnp.dot(p.astype(BF16), vc, preferred_element_type=F32)
            return m_new, l, acc

        carry = (m, l, acc)
        kcols = slice(j * dq, (j + 1) * dq)
        vcols = slice(j * dv, (j + 1) * dv)
        if window is not None:
            carry = step(carry, kn_ref[pl.ds(wstart, wk), kcols], vn_ref[pl.ds(wstart, wk), vcols], wmask)
        else:
            for c in range(t_new // tk):
                carry = step(carry, kn_ref[c * tk:(c + 1) * tk, kcols], vn_ref[c * tk:(c + 1) * tk, vcols], None)
        if t_ctx:
            tkc = min(tk, t_ctx)
            for c in range(t_ctx // tkc):
                carry = step(carry, kc_ref[c * tkc:(c + 1) * tkc, kcols], vc_ref[c * tkc:(c + 1) * tkc, vcols], None)
        m, l, acc = carry
        o = acc / l
        if diff:
            half = groups * tq
            d = o[:half] - lam_ref[...] * o[half:]
            o = (d * lax.rsqrt(jnp.mean(d * d, axis=-1, keepdims=True) + EPS) * gsub_ref[...]) * diff_post_scale
        outs = [o[g * tq:(g + 1) * tq] for g in range(groups)]
        oj = outs[0] if groups == 1 else jnp.concatenate(outs, axis=1)
        o_ref[:, j * groups * dv:(j + 1) * groups * dv] = oj.astype(o_ref.dtype)


def _attention(q, q_off, k_new, v_new, kn_off, *, batch, seq, nkv, groups, dq, dv, scale, hb, tq,
               k_ctx=None, v_ctx=None, kc_off=0, t_ctx=0, sink=None, window=None,
               lam=None, g_sub=None, diff_post_scale=None, name):
    t_new = seq
    tq = min(tq, seq)
    tk = min(512, t_new)
    assert seq % tq == 0 and t_new % tk == 0 and nkv % hb == 0
    assert q_off % tq == 0 and kn_off % t_new == 0
    qb, nb = q_off // tq, kn_off // t_new
    spt = seq // tq
    in_specs = [pl.BlockSpec((tq, hb * groups * dq), lambda b, h, i: (qb + b * spt + i, h)),
                pl.BlockSpec((t_new, hb * dq), lambda b, h, i: (nb + b, h)),
                pl.BlockSpec((t_new, hb * dv), lambda b, h, i: (nb + b, h))]
    args = [q, k_new, v_new]
    if t_ctx:
        assert kc_off % t_ctx == 0
        cb = kc_off // t_ctx
        in_specs += [pl.BlockSpec((t_ctx, hb * dq), lambda b, h, i: (cb + b, h)),
                     pl.BlockSpec((t_ctx, hb * dv), lambda b, h, i: (cb + b, h))]
        args += [k_ctx, v_ctx]
    if sink is not None:
        in_specs.append(pl.BlockSpec(sink.shape, lambda b, h, i: (0, 0)))
        args.append(sink)
    if diff_post_scale is not None:
        in_specs += [pl.BlockSpec((1, dv), lambda b, h, i: (0, 0))] * 2
        args += [lam, g_sub]
    body = functools.partial(
        _attn_body, hb=hb, groups=groups, dq=dq, dv=dv, tq=tq, tk=tk, scale=scale, t_new=t_new,
        t_ctx=t_ctx, window=window, has_sink=sink is not None, diff_post_scale=diff_post_scale)
    return pl.pallas_call(
        body,
        out_shape=jax.ShapeDtypeStruct((batch * seq, nkv * groups * dv), BF16),
        grid=(batch, nkv // hb, spt),
        in_specs=in_specs,
        out_specs=pl.BlockSpec((tq, hb * groups * dv), lambda b, h, i: (b * spt + i, h)),
        compiler_params=_cparams(("arbitrary", "arbitrary", "arbitrary")),
        name=name,
    )(*args)


def _experts_body(te_ref, tv_ref, x_ref, g_ref, wg_ref, wu_ref, wd_ref, o_ref, wgb, wub, wdb):
    t = pl.program_id(0)
    prev = te_ref[jnp.maximum(t - 1, 0)]
    first = jnp.logical_or(t == 0, te_ref[t] != prev)

    @pl.when(jnp.logical_and(first, tv_ref[t] == 1))
    def _():
        wgb[...] = wg_ref[...].astype(BF16)
        wub[...] = wu_ref[...].astype(BF16)
        wdb[...] = wd_ref[...].astype(BF16)

    @pl.when(tv_ref[t] == 1)
    def _():
        x = x_ref[...]
        a = jnp.dot(x, wgb[...], preferred_element_type=F32)
        u = jnp.dot(x, wub[...], preferred_element_type=F32)
        hcur = a * jax.nn.sigmoid(a) * u
        gate = g_ref[...]
        ff = hcur.shape[1]
        hcur = jnp.concatenate([hcur[:, c:c + V7X_LANES] * gate for c in range(0, ff, V7X_LANES)], axis=1)
        o_ref[...] = jnp.dot(hcur.astype(BF16), wdb[...], preferred_element_type=F32)

    @pl.when(tv_ref[t] == 0)
    def _():
        o_ref[...] = jnp.zeros_like(o_ref)


def _routed_experts(xs, row_gate, tile_expert, tile_valid, w_gate, w_up, w_down):
    r, d = xs.shape
    ff = w_gate.shape[2]
    tm = MOE_ROW_TILE
    grid_spec = pltpu.PrefetchScalarGridSpec(
        num_scalar_prefetch=2,
        grid=(r // tm,),
        in_specs=[pl.BlockSpec((tm, d), lambda t, te, tv: (t, 0)),
                  pl.BlockSpec((tm, V7X_LANES), lambda t, te, tv: (t, 0)),
                  pl.BlockSpec((None, d, ff), lambda t, te, tv: (te[t], 0, 0)),
                  pl.BlockSpec((None, d, ff), lambda t, te, tv: (te[t], 0, 0)),
                  pl.BlockSpec((None, ff, d), lambda t, te, tv: (te[t], 0, 0))],
        out_specs=pl.BlockSpec((tm, d), lambda t, te, tv: (t, 0)),
        scratch_shapes=[pltpu.VMEM((d, ff), BF16), pltpu.VMEM((d, ff), BF16), pltpu.VMEM((ff, d), BF16)],
    )
    return pl.pallas_call(
        _experts_body,
        out_shape=jax.ShapeDtypeStruct((r, d), F32),
        grid_spec=grid_spec,
        compiler_params=_cparams(("arbitrary",)),
        name="routed_experts",
    )(tile_expert, tile_valid, xs, row_gate, w_gate, w_up, w_down)


def _shared_body(t_ref, wg_ref, wu_ref, wd_ref, routed_ref, resid_ref, gate_ref, o_ref, h_ref):
    @pl.when(pl.program_id(1) == 0)
    def _():
        x = t_ref[...]
        a = jnp.dot(x, wg_ref[...], preferred_element_type=F32)
        u = jnp.dot(x, wu_ref[...], preferred_element_type=F32)
        h_ref[...] = (a * jax.nn.sigmoid(a) * u).astype(BF16)

    shared = jnp.dot(h_ref[...], wd_ref[...], preferred_element_type=F32)
    o_ref[...] = resid_ref[...] + gate_ref[...] * (routed_ref[...] + shared)


def _shared_expert_combine(t, ws_gate, ws_up, ws_down, routed, resid, gates, group_fn):
    n, d = t.shape
    ff = ws_gate.shape[1]
    tm = _largest_tile(n, 512, 8)
    tn = _largest_tile(d, 1024, V7X_LANES)
    gfn = group_fn(tm)
    return pl.pallas_call(
        _shared_body,
        out_shape=jax.ShapeDtypeStruct((n, d), F32),
        grid=(n // tm, d // tn),
        in_specs=[pl.BlockSpec((tm, d), lambda i, j: (i, 0)),
                  pl.BlockSpec((d, ff), lambda i, j: (0, 0)),
                  pl.BlockSpec((d, ff), lambda i, j: (0, 0)),
                  pl.BlockSpec((ff, tn), lambda i, j: (0, j)),
                  pl.BlockSpec((tm, tn), lambda i, j: (i, j)),
                  pl.BlockSpec((tm, tn), lambda i, j: (i, j)),
                  pl.BlockSpec((None, 1, tn), lambda i, j: (gfn(i), 0, j))],
        out_specs=pl.BlockSpec((tm, tn), lambda i, j: (i, j)),
        scratch_shapes=[pltpu.VMEM((tm, ff), BF16)],
        compiler_params=_cparams(("arbitrary", "arbitrary")),
        name="shared_expert_combine",
    )(t, ws_gate, ws_up, ws_down, routed, resid, gates)


def _route(logits, b_router):
    n = logits.shape[0]
    scores = jax.nn.sigmoid(logits)
    biased = scores + b_router.astype(F32)
    per_group = N_EXPERTS // N_EXPERT_GROUPS
    grp_score = lax.top_k(biased.reshape(n, N_EXPERT_GROUPS, per_group), 2)[0].sum(-1)
    _, g_idx = lax.top_k(grp_score, TOPK_GROUPS)
    g_mask = jax.nn.one_hot(g_idx, N_EXPERT_GROUPS, dtype=F32).sum(1)
    e_mask = jnp.repeat(g_mask, per_group, axis=1) > 0
    _, e_idx = lax.top_k(jnp.where(e_mask, biased, -jnp.inf), TOP_K)
    w = jnp.take_along_axis(scores, e_idx, axis=1)
    w = w / jnp.sum(w, axis=-1, keepdims=True) * ROUTED_SCALE
    return e_idx, w


def _dispatch_plan(e_idx, w, tm):
    n, k = e_idx.shape
    e = N_EXPERTS
    n_rows = n * k + e * tm
    n_tiles = n_rows // tm
    sel = jnp.zeros((n, e), jnp.int32).at[jnp.arange(n)[:, None], e_idx].set(1)
    rank = jnp.cumsum(sel, axis=0) - sel
    counts = jnp.sum(sel, axis=0)
    tiles_e = (counts + tm - 1) // tm
    tile_end = jnp.cumsum(tiles_e)
    row_start = (tile_end - tiles_e) * tm
    dest = row_start[None, :] + rank
    pos = jnp.take_along_axis(dest, e_idx, axis=1)
    flat = pos.reshape(-1)
    row_token = jnp.zeros((n_rows,), jnp.int32).at[flat].set(jnp.repeat(jnp.arange(n, dtype=jnp.int32), k))
    row_gate = jnp.zeros((n_rows,), F32).at[flat].set(w.reshape(-1))
    tile_ids = jnp.arange(n_tiles, dtype=jnp.int32)
    tile_expert = jnp.minimum(jnp.searchsorted(tile_end, tile_ids, side="right"), e - 1).astype(jnp.int32)
    tile_valid = (tile_ids < tile_end[-1]).astype(jnp.int32)
    return pos, row_token, row_gate, tile_expert, tile_valid


def _moe(y, g_ffn, shift, scale, gates, group_fn, w_router, b_router, w_gate, w_up, w_down,
         ws_gate, ws_up, ws_down):
    n, d = y.shape
    t, logits = _ada_norm(y, g_ffn, shift, scale, group_fn, w_router=w_router, name="ada_norm_ffn")
    e_idx, w = _route(logits, b_router)
    pos, row_token, row_gate, tile_expert, tile_valid = _dispatch_plan(e_idx, w, MOE_ROW_TILE)
    xs = jnp.take(t, row_token, axis=0)
    row_gate_b = jnp.broadcast_to(row_gate[:, None], (row_gate.shape[0], V7X_LANES))
    rows = _routed_experts(xs, row_gate_b, tile_expert, tile_valid, w_gate, w_up, w_down)
    routed = jnp.take(rows, pos.reshape(-1), axis=0).reshape(n, TOP_K, d).sum(axis=1)
    return _shared_expert_combine(t, ws_gate.astype(BF16), ws_up.astype(BF16), ws_down.astype(BF16),
                                  routed, y, gates, group_fn)


def _rms(x, g):
    return x * lax.rsqrt(jnp.mean(x * x, axis=-1, keepdims=True) + EPS) * g.astype(F32)


def _rope_2d(n_tok, rot_dim):
    rows = n_tok // GRID_W
    row = jnp.broadcast_to(jnp.arange(rows, dtype=F32)[:, None], (rows, GRID_W)).reshape(-1)
    col = jnp.broadcast_to(jnp.arange(GRID_W, dtype=F32)[None, :], (rows, GRID_W)).reshape(-1)
    n_freq = rot_dim // 4
    inv = ROPE_THETA ** (-jnp.arange(n_freq, dtype=F32) / n_freq)
    ang = jnp.concatenate([row[:, None] * inv, col[:, None] * inv], axis=-1)
    return jnp.cos(ang), jnp.sin(ang)


def _rope(x, cos, sin):
    shp = (cos.shape[0],) + (1,) * (x.ndim - 3) + (cos.shape[-1],)
    c, s = cos.reshape(shp), sin.reshape(shp)
    half = x.shape[-1] // 2
    x1, x2 = x[..., :half], x[..., half:]
    return jnp.concatenate([x1 * c - x2 * s, x2 * c + x1 * s], axis=-1)


def _rope_rows(x, n_ctx, dec_b, dec_s, cos, sin):
    lat = x[n_ctx:].reshape((dec_b, dec_s) + x.shape[1:])
    lat = _rope(lat, cos, sin).reshape((dec_b * dec_s,) + x.shape[1:])
    return jnp.concatenate([x[:n_ctx], lat], axis=0)


def _even_mixer(h, dims, cache, p):
    (bp, sp, dec_b, dec_s, past) = dims
    n_ctx = bp * sp
    n = h.shape[0]
    (w_in, g_q, w_qb, g_kv, w_kvb, g_qn, g_kn, g_wq, g_wk, sink, w_out) = p
    ckv_c, krope_c, wk_c, wv_c = cache
    o1 = MLA_Q_RANK
    o2 = o1 + MLA_KV_RANK
    o3 = o2 + MLA_ROPE
    o4 = o3 + WIN_HEADS * HEAD_DIM
    o5 = o4 + WIN_KV * HEAD_DIM
    pad = (-(w_in.shape[1])) % V7X_LANES
    w_in_p = jnp.concatenate([w_in[:, :o2], w_in[:, o3:], w_in[:, o2:o3],
                              jnp.zeros((w_in.shape[0], pad), w_in.dtype)], axis=1)
    proj = _matmul(h, w_in_p, out_dtype=F32, name="even_in_proj")
    c1 = o2 + WIN_HEADS * HEAD_DIM
    c2 = c1 + WIN_KV * HEAD_DIM
    c3 = c2 + WIN_KV * HEAD_DIM
    q_lat = _rms(proj[:, :o1], g_q)
    c_kv = _rms(proj[:, o1:o2], g_kv)
    wq = _rms(proj[:, o2:c1].reshape(n, WIN_HEADS, HEAD_DIM), g_wq)
    wk = _rms(proj[:, c1:c2].reshape(n, WIN_KV, HEAD_DIM), g_wk)
    wv = proj[:, c2:c3]
    k_rope = proj[:, c3:c3 + MLA_ROPE]

    cos_r, sin_r = _rope_2d(dec_s, MLA_ROPE)
    cos_h, sin_h = _rope_2d(dec_s, HEAD_DIM)

    w_qb_p = jnp.pad(w_qb.reshape(MLA_Q_RANK, MLA_HEADS, MLA_QK),
                     ((0, 0), (0, 0), (0, MLA_QK_PAD - MLA_QK))).reshape(MLA_Q_RANK, MLA_HEADS * MLA_QK_PAD)
    q_mla = _matmul(q_lat.astype(BF16), w_qb_p, out_dtype=F32, name="mla_q_up")
    q_mla = _rms(q_mla.reshape(n, MLA_HEADS, MLA_QK_PAD)[..., :MLA_QK], g_qn)
    q_tail = _rope_rows(q_mla[..., MLA_NOPE:], n_ctx, dec_b, dec_s, cos_r, sin_r)
    q_mla = jnp.concatenate([q_mla[..., :MLA_NOPE], q_tail,
                             jnp.zeros((n, MLA_HEADS, MLA_QK_PAD - MLA_QK), F32)], axis=-1)
    q_mla = q_mla.reshape(n, MLA_HEADS * MLA_QK_PAD).astype(BF16)

    n_all = n + dec_b * past
    ckv_all = jnp.concatenate([c_kv, ckv_c.reshape(dec_b * past, MLA_KV_RANK)], axis=0)
    krope_all = jnp.concatenate([k_rope, krope_c.reshape(dec_b * past, MLA_ROPE)], axis=0)
    kv = _matmul(ckv_all.astype(BF16), w_kvb, out_dtype=F32, name="mla_kv_up")
    kv = kv.reshape(n_all, MLA_HEADS, MLA_NOPE + MLA_V)
    k_pe = jnp.broadcast_to(krope_all[:, None, :], (n_all, MLA_HEADS, MLA_ROPE))
    mk = _rms(jnp.concatenate([kv[..., :MLA_NOPE], k_pe], axis=-1), g_kn)
    k_tail = mk[..., MLA_NOPE:]
    lat_tail = _rope(k_tail[n_ctx:n].reshape(dec_b, dec_s, MLA_HEADS, MLA_ROPE), cos_r, sin_r)
    k_tail = jnp.concatenate([k_tail[:n_ctx], lat_tail.reshape(dec_b * dec_s, MLA_HEADS, MLA_ROPE),
                              k_tail[n:]], axis=0)
    mk = jnp.concatenate([mk[..., :MLA_NOPE], k_tail,
                          jnp.zeros((n_all, MLA_HEADS, MLA_QK_PAD - MLA_QK), F32)], axis=-1)
    mk = mk.reshape(n_all, MLA_HEADS * MLA_QK_PAD).astype(BF16)
    mv = kv[..., MLA_NOPE:].reshape(n_all, MLA_HEADS * MLA_V).astype(BF16)

    mla_scale = MLA_QK ** -0.5
    a_ctx = _attention(q_mla, 0, mk, mv, 0, batch=bp, seq=sp, nkv=MLA_HEADS, groups=1,
                       dq=MLA_QK_PAD, dv=MLA_V, scale=mla_scale, hb=MLA_HEADS, tq=256, name="mla_attn_ctx")
    a_lat = _attention(q_mla, n_ctx, mk, mv, n_ctx, batch=dec_b, seq=dec_s, nkv=MLA_HEADS, groups=1,
                       dq=MLA_QK_PAD, dv=MLA_V, scale=mla_scale, hb=4, tq=512,
                       k_ctx=mk, v_ctx=mv, kc_off=n, t_ctx=past, name="mla_attn_lat")

    grp = WIN_HEADS // WIN_KV
    sink_b = jnp.broadcast_to(sink.astype(F32)[:, None], (WIN_HEADS, V7X_LANES))
    wq_r = _rope_rows(wq, n_ctx, dec_b, dec_s, cos_h, sin_h).reshape(n, WIN_HEADS * HEAD_DIM).astype(BF16)
    wk_r = _rope_rows(wk, n_ctx, dec_b, dec_s, cos_h, sin_h).reshape(n, WIN_KV * HEAD_DIM).astype(BF16)
    wv_b = wv.astype(BF16)
    win_scale = HEAD_DIM ** -0.5
    b_ctx = _attention(wq_r, 0, wk_r, wv_b, 0, batch=bp, seq=sp, nkv=WIN_KV, groups=grp,
                       dq=HEAD_DIM, dv=HEAD_DIM, scale=win_scale, hb=WIN_KV, tq=256, sink=sink_b,
                       name="win_attn_ctx")
    b_lat = _attention(wq_r, n_ctx, wk_r, wv_b, n_ctx, batch=dec_b, seq=dec_s, nkv=WIN_KV, groups=grp,
                       dq=HEAD_DIM, dv=HEAD_DIM, scale=win_scale, hb=WIN_KV, tq=256,
                       k_ctx=wk_c.reshape(dec_b * past, WIN_KV * HEAD_DIM).astype(BF16),
                       v_ctx=wv_c.reshape(dec_b * past, WIN_KV * HEAD_DIM).astype(BF16),
                       kc_off=0, t_ctx=past, sink=sink_b, window=WINDOW, name="win_attn_lat")

    out = jnp.concatenate([jnp.concatenate([a_ctx, a_lat], axis=0),
                           jnp.concatenate([b_ctx, b_lat], axis=0)], axis=1)
    state = (c_kv[:n_ctx].reshape(bp, 1, sp, MLA_KV_RANK),
             k_rope[:n_ctx].reshape(bp, 1, sp, MLA_ROPE),
             wk[:n_ctx].reshape(bp, 1, sp, WIN_KV, HEAD_DIM),
             wv[:n_ctx].reshape(bp, 1, sp, WIN_KV, HEAD_DIM))
    return out, w_out, state


def _odd_mixer(h, dims, cache, p, lam_init):
    (bp, sp, dec_b, dec_s, past) = dims
    n_ctx = bp * sp
    n = h.shape[0]
    (w_in, g_dq, g_dk, lq1, lk1, lq2, lk2, g_sub, g_aq, g_ak, w_out) = p
    dk_c, dv_c, ak_c, av_c = cache
    o1 = DIFF_HEADS * HEAD_DIM
    o2 = o1 + DIFF_KV * HEAD_DIM
    o3 = o2 + DIFF_KV * HEAD_DIM
    o4 = o3 + AX_HEADS * HEAD_DIM
    o5 = o4 + AX_KV * HEAD_DIM
    proj = _matmul(h, w_in, out_dtype=F32, name="odd_in_proj")
    dq = _rms(proj[:, :o1].reshape(n, DIFF_HEADS, 2, DIFF_D), g_dq)
    dk = _rms(proj[:, o1:o2].reshape(n, DIFF_KV, 2, DIFF_D), g_dk)
    dv = proj[:, o2:o3]
    aq = _rms(proj[:, o3:o4].reshape(n, AX_HEADS, HEAD_DIM), g_aq)
    ak = _rms(proj[:, o4:o5].reshape(n, AX_KV, HEAD_DIM), g_ak)
    av = proj[:, o5:]

    cos_r, sin_r = _rope_2d(dec_s, DIFF_D)
    cos_h, sin_h = _rope_2d(dec_s, HEAD_DIM)
    dq_r = _rope_rows(dq, n_ctx, dec_b, dec_s, cos_r, sin_r).reshape(n, DIFF_HEADS * HEAD_DIM).astype(BF16)
    dk_r = _rope_rows(dk, n_ctx, dec_b, dec_s, cos_r, sin_r).reshape(n, DIFF_KV * HEAD_DIM).astype(BF16)
    aq_r = _rope_rows(aq, n_ctx, dec_b, dec_s, cos_h, sin_h).reshape(n, AX_HEADS * HEAD_DIM).astype(BF16)
    ak_r = _rope_rows(ak, n_ctx, dec_b, dec_s, cos_h, sin_h).reshape(n, AX_KV * HEAD_DIM).astype(BF16)
    dv_b = dv.astype(BF16)
    av_b = av.astype(BF16)

    lam = (jnp.exp(jnp.sum(lq1.astype(F32) * lk1.astype(F32)))
           - jnp.exp(jnp.sum(lq2.astype(F32) * lk2.astype(F32))) + lam_init)
    lam_b = jnp.broadcast_to(lam.astype(F32), (1, HEAD_DIM))
    g_sub_b = g_sub.astype(F32).reshape(1, HEAD_DIM)
    grp = DIFF_HEADS // DIFF_KV
    diff_kw = dict(nkv=DIFF_KV, groups=grp, dq=HEAD_DIM, dv=HEAD_DIM, scale=DIFF_D ** -0.5, hb=DIFF_KV,
                   lam=lam_b, g_sub=g_sub_b, diff_post_scale=1.0 - lam_init)
    d_ctx = _attention(dq_r, 0, dk_r, dv_b, 0, batch=bp, seq=sp, tq=256, name="diff_attn_ctx", **diff_kw)
    d_lat = _attention(dq_r, n_ctx, dk_r, dv_b, n_ctx, batch=dec_b, seq=dec_s, tq=128,
                       k_ctx=dk_c.reshape(dec_b * past, DIFF_KV * HEAD_DIM).astype(BF16),
                       v_ctx=dv_c.reshape(dec_b * past, DIFF_KV * HEAD_DIM).astype(BF16),
                       kc_off=0, t_ctx=past, name="diff_attn_lat", **diff_kw)
    agrp = AX_HEADS // AX_KV
    ax_kw = dict(nkv=AX_KV, groups=agrp, dq=HEAD_DIM, dv=HEAD_DIM, scale=HEAD_DIM ** -0.5, hb=AX_KV)
    x_ctx = _attention(aq_r, 0, ak_r, av_b, 0, batch=bp, seq=sp, tq=256, name="ax_attn_ctx", **ax_kw)
    x_lat = _attention(aq_r, n_ctx, ak_r, av_b, n_ctx, batch=dec_b, seq=dec_s, tq=256,
                       k_ctx=ak_c.reshape(dec_b * past, AX_KV * HEAD_DIM).astype(BF16),
                       v_ctx=av_c.reshape(dec_b * past, AX_KV * HEAD_DIM).astype(BF16),
                       kc_off=0, t_ctx=past, name="ax_attn_lat", **ax_kw)
    out = jnp.concatenate([jnp.concatenate([d_ctx, d_lat], axis=0),
                           jnp.concatenate([x_ctx, x_lat], axis=0)], axis=1)
    state = (dk[:n_ctx].reshape(bp, 1, sp, DIFF_KV, 2, DIFF_D),
             dv[:n_ctx].reshape(bp, 1, sp, DIFF_KV, HEAD_DIM),
             ak[:n_ctx].reshape(bp, 1, sp, AX_KV, HEAD_DIM),
             av[:n_ctx].reshape(bp, 1, sp, AX_KV, HEAD_DIM))
    return out, w_out, state


def kernel(x_prompt, x_sample, cache_mla_ckv, cache_mla_krope, cache_win_k, cache_win_v, cache_diff_k, cache_diff_v, cache_ax_k, cache_ax_v, c, c_ctx, w_mod, b_mod, g_norm_mix, g_norm_ffn, w_in_even, g_mla_q, w_mla_qb, g_mla_kv, w_mla_kvb, g_mla_qn, g_mla_kn, g_win_qn, g_win_kn, win_sink, w_out_even, w_in_odd, g_diff_qn, g_diff_kn, diff_lq1, diff_lk1, diff_lq2, diff_lk2, g_diff_sub, g_ax_qn, g_ax_kn, w_out_odd, w_router, b_router, w_exp_gate, w_exp_up, w_exp_down, w_sh_gate, w_sh_up, w_sh_down):
    bp, sp, d = x_prompt.shape
    dec_b, dec_s, _ = x_sample.shape
    depth = w_mod.shape[0]
    n_ctx = bp * sp
    n = n_ctx + dec_b * dec_s
    dims = (bp, sp, dec_b, dec_s, cache_mla_ckv.shape[2])
    group_fn = lambda tm: _group_index_fn(tm, n_ctx, dec_s)
    n_groups = 1 + dec_b

    y = jnp.concatenate([x_prompt.reshape(n_ctx, d), x_sample.reshape(dec_b * dec_s, d)], axis=0)

    cond = jnp.concatenate([c_ctx[None], c, jnp.zeros((8 - n_groups % 8, d), F32)], axis=0)
    cond = jax.nn.silu(cond)

    states_even, states_odd = [], []
    for l in range(depth):
        i = l // 2
        mod = _matmul(cond, w_mod[l], out_dtype=F32, bias=b_mod[l][None], name="modulation")
        mod = mod[:n_groups].reshape(n_groups, 6, 1, d)
        sh1, sc1, g1, sh2, sc2, g2 = (mod[:, j] for j in range(6))
        h = _ada_norm(y, g_norm_mix[l][None], sh1, sc1, group_fn, name="ada_norm_mix")
        if l % 2 == 0:
            pe = (w_in_even[i], g_mla_q[i], w_mla_qb[i], g_mla_kv[i], w_mla_kvb[i], g_mla_qn[i],
                  g_mla_kn[i], g_win_qn[i], g_win_kn[i], win_sink[i], w_out_even[i])
            cache = (cache_mla_ckv[:, i], cache_mla_krope[:, i], cache_win_k[:, i], cache_win_v[:, i])
            out, w_out, state = _even_mixer(h, dims, cache, pe)
            states_even.append(state)
        else:
            po = (w_in_odd[i], g_diff_qn[i], g_diff_kn[i], diff_lq1[i], diff_lk1[i], diff_lq2[i],
                  diff_lk2[i], g_diff_sub[i], g_ax_qn[i], g_ax_kn[i], w_out_odd[i])
            cache = (cache_diff_k[:, i], cache_diff_v[:, i], cache_ax_k[:, i], cache_ax_v[:, i])
            lam_init = 0.8 - 0.6 * math.exp(-0.3 * l)
            out, w_out, state = _odd_mixer(h, dims, cache, po, lam_init)
            states_odd.append(state)
        y = _matmul(out, w_out, out_dtype=F32, resid=y, gates=g1, group_fn=group_fn, name="mixer_out_proj")
        y = _moe(y, g_norm_ffn[l][None], sh2, sc2, g2, group_fn, w_router[l], b_router[l],
                 w_exp_gate[l], w_exp_up[l], w_exp_down[l], w_sh_gate[l], w_sh_up[l], w_sh_down[l])

    yp = y[:n_ctx].reshape(bp, sp, d)
    ys = y[n_ctx:].reshape(dec_b, dec_s, d)
    even = tuple(jnp.concatenate([s[j] for s in states_even], axis=1) for j in range(4))
    odd = tuple(jnp.concatenate([s[j] for s in states_odd], axis=1) for j in range(4))
    return (yp, ys) + even + odd
```

```python
import functools
import math

import jax
import jax.numpy as jnp
from jax import lax
from jax.experimental import pallas as pl
from jax.experimental.pallas import tpu as pltpu

F32 = jnp.float32
BF16 = jnp.bfloat16

GRID_W = 64
ROPE_THETA = 10000.0
EPS = 1e-6
NEG_INF = -1e30
HEAD_DIM = 128
MLA_HEADS = 16
MLA_Q_RANK = 768
MLA_KV_RANK = 512
MLA_NOPE = 128
MLA_ROPE = 64
MLA_V = 128
MLA_QK = MLA_NOPE + MLA_ROPE
MLA_QK_PAD = 256
WIN_HEADS = 16
WIN_KV = 4
WINDOW = 128
DIFF_HEADS = 16
DIFF_KV = 4
DIFF_D = HEAD_DIM // 2
AX_HEADS = 16
AX_KV = 4
N_EXPERTS = 64
N_EXPERT_GROUPS = 8
TOPK_GROUPS = 4
TOP_K = 8
ROUTED_SCALE = 2.5

V7X_LANES = 128
V7X_VMEM_LIMIT_BYTES = 56 * 1024 * 1024
MM_WEIGHT_TILE_BYTES = 32 * 1024 * 1024
MOE_ROW_TILE = 256


def _cparams(sem):
    return pltpu.CompilerParams(dimension_semantics=sem, vmem_limit_bytes=V7X_VMEM_LIMIT_BYTES)


def _largest_tile(n, cap, step):
    if n <= cap:
        return n
    t = (cap // step) * step
    while t >= step:
        if n % t == 0:
            return t
        t -= step
    raise ValueError(f"no tile for {n} under {cap}")


def _group_index_fn(tm, n_ctx_rows, dec_rows):
    assert n_ctx_rows % tm == 0 and dec_rows % tm == 0
    ctx_tiles = n_ctx_rows // tm
    per_dec = dec_rows // tm

    def fn(i):
        return jnp.where(i < ctx_tiles, 0, 1 + (jnp.maximum(i - ctx_tiles, 0)) // per_dec)

    return fn


def _mm_body(*refs, has_bias, has_resid):
    x_ref, w_ref = refs[0], refs[1]
    idx = 2
    if has_bias:
        bias_ref = refs[idx]
        idx += 1
    if has_resid:
        resid_ref, gate_ref = refs[idx], refs[idx + 1]
        idx += 2
    o_ref, wb_ref = refs[idx], refs[idx + 1]

    @pl.when(pl.program_id(1) == 0)
    def _():
        wb_ref[...] = w_ref[...].astype(BF16)

    acc = jnp.dot(x_ref[...].astype(BF16), wb_ref[...], preferred_element_type=F32)
    if has_bias:
        acc = acc + bias_ref[...]
    if has_resid:
        acc = resid_ref[...] + gate_ref[...] * acc
    o_ref[...] = acc.astype(o_ref.dtype)


def _matmul(x, w, *, out_dtype, layer=None, bias=None, resid=None, gates=None, group_fn=None, name):
    m, k = x.shape
    n = w.shape[-1]
    tm = _largest_tile(m, 512, 8)
    tn_cap = max(V7X_LANES, min(2048, MM_WEIGHT_TILE_BYTES // (k * 10)))
    tn = _largest_tile(n, tn_cap, V7X_LANES)
    grid = (n // tn, m // tm)
    if layer is None:
        w_spec = pl.BlockSpec((k, tn), lambda j, i: (0, j))
    else:
        w_spec = pl.BlockSpec((None, k, tn), lambda j, i: (layer, 0, j))
    in_specs = [pl.BlockSpec((tm, k), lambda j, i: (i, 0)), w_spec]
    args = [x, w]
    if bias is not None:
        in_specs.append(pl.BlockSpec((None, 1, tn), lambda j, i: (layer, 0, j)))
        args.append(bias)
    if resid is not None:
        gfn = group_fn(tm)
        in_specs.append(pl.BlockSpec((tm, tn), lambda j, i: (i, j)))
        in_specs.append(pl.BlockSpec((None, 1, tn), lambda j, i: (gfn(i), 0, j)))
        args += [resid, gates]
    return pl.pallas_call(
        functools.partial(_mm_body, has_bias=bias is not None, has_resid=resid is not None),
        out_shape=jax.ShapeDtypeStruct((m, n), out_dtype),
        grid=grid,
        in_specs=in_specs,
        out_specs=pl.BlockSpec((tm, tn), lambda j, i: (i, j)),
        scratch_shapes=[pltpu.VMEM((k, tn), BF16)],
        compiler_params=_cparams(("arbitrary", "arbitrary")),
        name=name,
    )(*args)


def _first_max(cur, sub, limit):
    m = jnp.max(cur, axis=0, keepdims=True)
    first = jnp.min(jnp.where(cur == m, sub, limit), axis=0, keepdims=True)
    return m, first, sub == first


def _route_columns(logits, bias, tri, carry):
    e, t = logits.shape
    per = e // N_EXPERT_GROUPS
    scores = jax.nn.sigmoid(logits)
    biased = scores + bias
    neg = -jnp.inf
    gsub = lax.broadcasted_iota(jnp.int32, (per, t), 0)
    grp = []
    for g in range(N_EXPERT_GROUPS):
        xg = biased[g * per:(g + 1) * per]
        m1, _, hit = _first_max(xg, gsub, per)
        m2 = jnp.max(jnp.where(hit, neg, xg), axis=0, keepdims=True)
        grp.append(m1 + m2)
    grp = jnp.concatenate(grp, axis=0)
    nsub = lax.broadcasted_iota(jnp.int32, (N_EXPERT_GROUPS, t), 0)
    gsel = jnp.zeros((N_EXPERT_GROUPS, t), F32)
    for _ in range(TOPK_GROUPS):
        _, _, hit = _first_max(grp, nsub, N_EXPERT_GROUPS)
        gsel = jnp.where(hit, 1.0, gsel)
        grp = jnp.where(hit, neg, grp)
    emask = jnp.concatenate([jnp.broadcast_to(gsel[g:g + 1], (per, t)) for g in range(N_EXPERT_GROUPS)], axis=0)
    cur = jnp.where(emask > 0.5, biased, neg)
    esub = lax.broadcasted_iota(jnp.int32, (e, t), 0)
    sel = jnp.zeros((e, t), F32)
    ids, ws, hits = [], [], []
    for _ in range(TOP_K):
        _, first, hit = _first_max(cur, esub, e)
        ids.append(first)
        ws.append(jnp.sum(jnp.where(hit, scores, 0.0), axis=0, keepdims=True))
        hits.append(hit)
        sel = jnp.where(hit, 1.0, sel)
        cur = jnp.where(hit, neg, cur)
    w = jnp.concatenate(ws, axis=0)
    gates = w / jnp.sum(w, axis=0, keepdims=True) * ROUTED_SCALE
    rank_all = jnp.dot(sel.astype(BF16), tri, preferred_element_type=F32) + carry
    ranks = [jnp.sum(jnp.where(hit, rank_all, 0.0), axis=0, keepdims=True) for hit in hits]
    counts = jnp.sum(sel, axis=1, keepdims=True)
    return (jnp.concatenate(ids, axis=0), gates, jnp.concatenate(ranks, axis=0).astype(jnp.int32), counts)


def _adanorm_body(x_ref, g_ref, shift_ref, scale_ref, *rest, has_router):
    x = x_ref[...]
    y = x * lax.rsqrt(jnp.mean(x * x, axis=-1, keepdims=True) + EPS) * g_ref[...]
    t = y * (1.0 + scale_ref[...]) + shift_ref[...]
    if has_router:
        wrt_ref, br_ref, tri_ref, o_ref, ids_ref, gate_ref, rank_ref, cnt_ref, carry_ref = rest

        @pl.when(pl.program_id(0) == 0)
        def _():
            carry_ref[...] = jnp.zeros_like(carry_ref)

        logits = lax.dot_general(wrt_ref[...], t, (((1,), (1,)), ((), ())),
                                 precision=lax.Precision.HIGHEST, preferred_element_type=F32)
        ids, gates, ranks, counts = _route_columns(logits, br_ref[...], tri_ref[...], carry_ref[:, :1])
        ids_ref[...] = ids
        gate_ref[...] = gates
        rank_ref[...] = ranks
        carry_ref[...] = carry_ref[...] + counts
        cnt_ref[...] = carry_ref[...]
    else:
        (o_ref,) = rest
    o_ref[...] = t.astype(o_ref.dtype)


def _ada_norm(x, g, shift, scale, group_fn, *, w_router=None, b_router=None, name):
    n, d = x.shape
    tm = 256
    gfn = group_fn(tm)
    in_specs = [pl.BlockSpec((tm, d), lambda i: (i, 0)),
                pl.BlockSpec((1, d), lambda i: (0, 0)),
                pl.BlockSpec((None, 1, d), lambda i: (gfn(i), 0, 0)),
                pl.BlockSpec((None, 1, d), lambda i: (gfn(i), 0, 0))]
    args = [x, g, shift, scale]
    out_shape = jax.ShapeDtypeStruct((n, d), BF16)
    out_specs = pl.BlockSpec((tm, d), lambda i: (i, 0))
    scratch = []
    if w_router is not None:
        e = w_router.shape[1]
        tri = (lax.broadcasted_iota(jnp.int32, (tm, tm), 0) < lax.broadcasted_iota(jnp.int32, (tm, tm), 1))
        in_specs += [pl.BlockSpec((e, d), lambda i: (0, 0)),
                     pl.BlockSpec((e, tm), lambda i: (0, 0)),
                     pl.BlockSpec((tm, tm), lambda i: (0, 0))]
        args += [w_router.T, jnp.broadcast_to(b_router.astype(F32)[:, None], (e, tm)), tri.astype(BF16)]
        col = lambda rows, dt: (jax.ShapeDtypeStruct((rows, n), dt), pl.BlockSpec((rows, tm), lambda i: (0, i)))
        extra = [col(TOP_K, jnp.int32), col(TOP_K, F32), col(TOP_K, jnp.int32),
                 (jax.ShapeDtypeStruct((e, V7X_LANES), F32), pl.BlockSpec((e, V7X_LANES), lambda i: (0, 0)))]
        out_shape = (out_shape,) + tuple(s for s, _ in extra)
        out_specs = (out_specs,) + tuple(b for _, b in extra)
        scratch = [pltpu.VMEM((e, V7X_LANES), F32)]
    return pl.pallas_call(
        functools.partial(_adanorm_body, has_router=w_router is not None),
        out_shape=out_shape,
        grid=(n // tm,),
        in_specs=in_specs,
        out_specs=out_specs,
        scratch_shapes=scratch,
        compiler_params=_cparams(("arbitrary",)),
        name=name,
    )(*args)


def _attn_body(*refs, hb, groups, dq, dv, tq, tk, scale, t_new, t_ctx, window, has_sink,
               diff_post_scale):
    it = iter(refs)
    q_ref, kn_ref, vn_ref = next(it), next(it), next(it)
    kc_ref = vc_ref = sink_ref = lam_ref = gsub_ref = None
    if t_ctx:
        kc_ref, vc_ref = next(it), next(it)
    if has_sink:
        sink_ref = next(it)
    diff = diff_post_scale is not None
    if diff:
        lam_ref, gsub_ref = next(it), next(it)
    o_ref = next(it)

    h = pl.program_id(1)
    i = pl.program_id(2)
    nstack = 2 * groups if diff else groups
    rows = nstack * tq

    def stack_rows(parts):
        return parts[0] if len(parts) == 1 else jnp.concatenate(parts, axis=0)

    if window is not None:
        wk = min(t_new, tq + 2 * window)
        if wk == t_new:
            wstart = 0
        else:
            wstart = pl.multiple_of(jnp.clip(i * tq - window, 0, t_new - wk), V7X_LANES)
        qi = i * tq + lax.broadcasted_iota(jnp.int32, (tq, wk), 0)
        kj = wstart + lax.broadcasted_iota(jnp.int32, (tq, wk), 1)
        wmask = stack_rows([jnp.abs(qi - kj) <= window] * nstack)

    for j in range(hb):
        qj = q_ref[:, j * groups * dq:(j + 1) * groups * dq]
        parts = [qj[:, g * dq:(g + 1) * dq] for g in range(groups)]
        if diff:
            lo = lax.broadcasted_iota(jnp.int32, (tq, dq), 1) < dq // 2
            zero = jnp.zeros((tq, dq), qj.dtype)
            parts = [jnp.where(lo, p, zero) for p in parts] + [jnp.where(lo, zero, p) for p in parts]
        qs = stack_rows(parts)

        if has_sink:
            sinks = []
            for g in range(groups):
                hh = (h * hb + j) * groups + g
                sinks.append(jnp.broadcast_to(sink_ref[pl.ds(hh, 1), :][:, :1], (tq, 1)))
            m = stack_rows(sinks)
            l = jnp.ones((rows, 1), F32)
        else:
            m = jnp.full((rows, 1), NEG_INF, F32)
            l = jnp.zeros((rows, 1), F32)
        acc = jnp.zeros((rows, dv), F32)

        def step(carry, kc, vc, mask):
            m, l, acc = carry
            s = lax.dot_general(qs, kc, (((1,), (1,)), ((), ())), preferred_element_type=F32) * scale
            if mask is not None:
                s = jnp.where(mask, s, NEG_INF)
            m_new = jnp.maximum(m, jnp.max(s, axis=-1, keepdims=True))
            alpha = jnp.exp(m - m_new)
            p = jnp.exp(s - m_new)
            l = alpha * l + jnp.sum(p, axis=-1, keepdims=True)
            acc = alpha * acc + jnp.dot(p.astype(BF16), vc, preferred_element_type=F32)
            return m_new, l, acc

        carry = (m, l, acc)
        kcols = slice(j * dq, (j + 1) * dq)
        vcols = slice(j * dv, (j + 1) * dv)
        if window is not None:
            carry = step(carry, kn_ref[pl.ds(wstart, wk), kcols], vn_ref[pl.ds(wstart, wk), vcols], wmask)
        else:
            for c in range(t_new // tk):
                carry = step(carry, kn_ref[c * tk:(c + 1) * tk, kcols], vn_ref[c * tk:(c + 1) * tk, vcols], None)
        if t_ctx:
            tkc = min(tk, t_ctx)
            for c in range(t_ctx // tkc):
                carry = step(carry, kc_ref[c * tkc:(c + 1) * tkc, kcols], vc_ref[c * tkc:(c + 1) * tkc, vcols], None)
        m, l, acc = carry
        o = acc / l
        if diff:
            half = groups * tq
            d = o[:half] - lam_ref[...] * o[half:]
            o = (d * lax.rsqrt(jnp.mean(d * d, axis=-1, keepdims=True) + EPS) * gsub_ref[...]) * diff_post_scale
        outs = [o[g * tq:(g + 1) * tq] for g in range(groups)]
        oj = outs[0] if groups == 1 else jnp.concatenate(outs, axis=1)
        o_ref[:, j * groups * dv:(j + 1) * groups * dv] = oj.astype(o_ref.dtype)


def _attention(q, q_off, k_new, v_new, kn_off, *, batch, seq, nkv, groups, dq, dv, scale, hb, tq,
               k_ctx=None, v_ctx=None, kc_off=0, t_ctx=0, sink=None, window=None,
               lam=None, g_sub=None, diff_post_scale=None, name):
    t_new = seq
    tq = min(tq, seq)
    tk = min(512, t_new)
    assert seq % tq == 0 and t_new % tk == 0 and nkv % hb == 0
    assert q_off % tq == 0 and kn_off % t_new == 0
    qb, nb = q_off // tq, kn_off // t_new
    spt = seq // tq
    in_specs = [pl.BlockSpec((tq, hb * groups * dq), lambda b, h, i: (qb + b * spt + i, h)),
                pl.BlockSpec((t_new, hb * dq), lambda b, h, i: (nb + b, h)),
                pl.BlockSpec((t_new, hb * dv), lambda b, h, i: (nb + b, h))]
    args = [q, k_new, v_new]
    if t_ctx:
        assert kc_off % t_ctx == 0
        cb = kc_off // t_ctx
        in_specs += [pl.BlockSpec((t_ctx, hb * dq), lambda b, h, i: (cb + b, h)),
                     pl.BlockSpec((t_ctx, hb * dv), lambda b, h, i: (cb + b, h))]
        args += [k_ctx, v_ctx]
    if sink is not None:
        in_specs.append(pl.BlockSpec(sink.shape, lambda b, h, i: (0, 0)))
        args.append(sink)
    if diff_post_scale is not None:
        in_specs += [pl.BlockSpec((1, dv), lambda b, h, i: (0, 0))] * 2
        args += [lam, g_sub]
    body = functools.partial(
        _attn_body, hb=hb, groups=groups, dq=dq, dv=dv, tq=tq, tk=tk, scale=scale, t_new=t_new,
        t_ctx=t_ctx, window=window, has_sink=sink is not None, diff_post_scale=diff_post_scale)
    return pl.pallas_call(
        body,
        out_shape=jax.ShapeDtypeStruct((batch * seq, nkv * groups * dv), BF16),
        grid=(batch, nkv // hb, spt),
        in_specs=in_specs,
        out_specs=pl.BlockSpec((tq, hb * groups * dv), lambda b, h, i: (b * spt + i, h)),
        compiler_params=_cparams(("arbitrary", "arbitrary", "arbitrary")),
        name=name,
    )(*args)


def _experts_body(te_ref, nv_ref, x_ref, g_ref, wg_ref, wu_ref, wd_ref, o_ref, wgb, wub, wdb):
    t = pl.program_id(0)
    valid = t < nv_ref[0]
    prev = te_ref[jnp.maximum(t - 1, 0)]
    first = jnp.logical_or(t == 0, te_ref[t] != prev)

    @pl.when(jnp.logical_and(first, valid))
    def _():
        wgb[...] = wg_ref[...].astype(BF16)
        wub[...] = wu_ref[...].astype(BF16)
        wdb[...] = wd_ref[...].astype(BF16)

    @pl.when(valid)
    def _():
        x = x_ref[...]
        a = jnp.dot(x, wgb[...], preferred_element_type=F32)
        u = jnp.dot(x, wub[...], preferred_element_type=F32)
        hcur = a * jax.nn.sigmoid(a) * u
        gate = g_ref[...]
        ff = hcur.shape[1]
        hcur = jnp.concatenate([hcur[:, c:c + V7X_LANES] * gate for c in range(0, ff, V7X_LANES)], axis=1)
        o_ref[...] = jnp.dot(hcur.astype(BF16), wdb[...], preferred_element_type=F32).astype(o_ref.dtype)


def _routed_experts(xs, row_gate, tile_expert, n_valid, w_gate, w_up, w_down, layer):
    r, d = xs.shape
    ff = w_gate.shape[-1]
    tm = MOE_ROW_TILE
    row_blk = lambda t, te, nv: (jnp.minimum(t, nv[0] - 1), 0)
    grid_spec = pltpu.PrefetchScalarGridSpec(
        num_scalar_prefetch=2,
        grid=(r // tm,),
        in_specs=[pl.BlockSpec((tm, d), row_blk),
                  pl.BlockSpec((tm, V7X_LANES), row_blk),
                  pl.BlockSpec((None, None, d, ff), lambda t, te, nv: (layer, te[t], 0, 0)),
                  pl.BlockSpec((None, None, d, ff), lambda t, te, nv: (layer, te[t], 0, 0)),
                  pl.BlockSpec((None, None, ff, d), lambda t, te, nv: (layer, te[t], 0, 0))],
        out_specs=pl.BlockSpec((tm, d), row_blk),
        scratch_shapes=[pltpu.VMEM((d, ff), BF16), pltpu.VMEM((d, ff), BF16), pltpu.VMEM((ff, d), BF16)],
    )
    return pl.pallas_call(
        _experts_body,
        out_shape=jax.ShapeDtypeStruct((r, d), BF16),
        grid_spec=grid_spec,
        compiler_params=_cparams(("arbitrary",)),
        name="routed_experts",
    )(tile_expert, n_valid, xs, row_gate, w_gate, w_up, w_down)


def _shared_body(t_ref, wg_ref, wu_ref, wd_ref, resid_ref, gate_ref, o_ref, h_ref):
    @pl.when(pl.program_id(1) == 0)
    def _():
        x = t_ref[...]
        a = jnp.dot(x, wg_ref[...], preferred_element_type=F32)
        u = jnp.dot(x, wu_ref[...], preferred_element_type=F32)
        h_ref[...] = (a * jax.nn.sigmoid(a) * u).astype(BF16)

    shared = jnp.dot(h_ref[...], wd_ref[...], preferred_element_type=F32)
    o_ref[...] = resid_ref[...] + gate_ref[...] * shared


def _shared_expert(t, ws_gate, ws_up, ws_down, resid, gates, group_fn):
    n, d = t.shape
    ff = ws_gate.shape[1]
    tm = _largest_tile(n, 512, 8)
    tn = _largest_tile(d, 1024, V7X_LANES)
    gfn = group_fn(tm)
    return pl.pallas_call(
        _shared_body,
        out_shape=jax.ShapeDtypeStruct((n, d), F32),
        grid=(n // tm, d // tn),
        in_specs=[pl.BlockSpec((tm, d), lambda i, j: (i, 0)),
                  pl.BlockSpec((d, ff), lambda i, j: (0, 0)),
                  pl.BlockSpec((d, ff), lambda i, j: (0, 0)),
                  pl.BlockSpec((ff, tn), lambda i, j: (0, j)),
                  pl.BlockSpec((tm, tn), lambda i, j: (i, j)),
                  pl.BlockSpec((None, 1, tn), lambda i, j: (gfn(i), 0, j))],
        out_specs=pl.BlockSpec((tm, tn), lambda i, j: (i, j)),
        scratch_shapes=[pltpu.VMEM((tm, ff), BF16)],
        compiler_params=_cparams(("arbitrary", "arbitrary")),
        name="shared_expert",
    )(t, ws_gate, ws_up, ws_down, resid, gates)


def _combine_body(rows_ref, base_ref, gate_ref, o_ref, *, k, d):
    acc = rows_ref[:, :d].astype(F32)
    for c in range(1, k):
        acc = acc + rows_ref[:, c * d:(c + 1) * d].astype(F32)
    o_ref[...] = base_ref[...] + gate_ref[...] * acc


def _combine(picked, base, gates, group_fn):
    n, d = base.shape
    k = picked.shape[1] // d
    tm = 128
    gfn = group_fn(tm)
    return pl.pallas_call(
        functools.partial(_combine_body, k=k, d=d),
        out_shape=jax.ShapeDtypeStruct((n, d), F32),
        grid=(n // tm,),
        in_specs=[pl.BlockSpec((tm, k * d), lambda i: (i, 0)),
                  pl.BlockSpec((tm, d), lambda i: (i, 0)),
                  pl.BlockSpec((None, 1, d), lambda i: (gfn(i), 0, 0))],
        out_specs=pl.BlockSpec((tm, d), lambda i: (i, 0)),
        compiler_params=_cparams(("arbitrary",)),
        name="moe_combine",
    )(picked, base, gates)


def _dispatch_plan(ids, gate_w, ranks, counts, tm):
    k, n = ids.shape
    e = counts.shape[0]
    n_rows = n * k + e * tm
    n_tiles = n_rows // tm
    tiles_e = (counts + tm - 1) // tm
    tile_end = jnp.cumsum(tiles_e)
    row_start = (tile_end - tiles_e) * tm
    onehot = ids[:, :, None] == jnp.arange(e, dtype=jnp.int32)
    pos = jnp.sum(jnp.where(onehot, row_start, 0), axis=-1) + ranks
    flat = pos.reshape(-1)
    tok = jnp.tile(jnp.arange(n, dtype=jnp.int32), k)
    packed = jnp.stack([tok.astype(F32), gate_w.reshape(-1)], axis=1)
    rows = jnp.zeros((n_rows, 2), F32).at[flat].set(packed)
    row_token = rows[:, 0].astype(jnp.int32)
    row_gate = rows[:, 1]
    tile_ids = jnp.arange(n_tiles, dtype=jnp.int32)
    tile_expert = jnp.sum((tile_end[None, :] <= tile_ids[:, None]).astype(jnp.int32), axis=1)
    tile_expert = jnp.minimum(tile_expert, e - 1)
    return pos, row_token, row_gate, tile_expert, tile_end[-1:].astype(jnp.int32)


def _moe(y, g_ffn, shift, scale, gates, group_fn, layer, w_router, b_router, w_gate, w_up, w_down,
         ws_gate, ws_up, ws_down):
    n, d = y.shape
    t, ids, gate_w, ranks, counts = _ada_norm(y, g_ffn, shift, scale, group_fn, w_router=w_router[layer],
                                              b_router=b_router[layer], name="ada_norm_route")
    base = _shared_expert(t, ws_gate[layer].astype(BF16), ws_up[layer].astype(BF16),
                          ws_down[layer].astype(BF16), y, gates, group_fn)
    pos, row_token, row_gate, tile_expert, n_valid = _dispatch_plan(
        ids, gate_w, ranks, counts[:, 0].astype(jnp.int32), MOE_ROW_TILE)
    xs = jnp.take(t, row_token, axis=0)
    row_gate_b = jnp.broadcast_to(row_gate[:, None], (row_gate.shape[0], V7X_LANES))
    rows = _routed_experts(xs, row_gate_b, tile_expert, n_valid, w_gate, w_up, w_down, layer)
    picked = jnp.take(rows, pos.T.reshape(-1), axis=0).reshape(n, TOP_K * d)
    return _combine(picked, base, gates, group_fn)


def _rms(x, g):
    return x * lax.rsqrt(jnp.mean(x * x, axis=-1, keepdims=True) + EPS) * g.astype(F32)


def _rope_2d(n_tok, rot_dim):
    rows = n_tok // GRID_W
    row = jnp.broadcast_to(jnp.arange(rows, dtype=F32)[:, None], (rows, GRID_W)).reshape(-1)
    col = jnp.broadcast_to(jnp.arange(GRID_W, dtype=F32)[None, :], (rows, GRID_W)).reshape(-1)
    n_freq = rot_dim // 4
    inv = ROPE_THETA ** (-jnp.arange(n_freq, dtype=F32) / n_freq)
    ang = jnp.concatenate([row[:, None] * inv, col[:, None] * inv], axis=-1)
    return jnp.cos(ang), jnp.sin(ang)


def _rope(x, cos, sin):
    shp = (cos.shape[0],) + (1,) * (x.ndim - 3) + (cos.shape[-1],)
    c, s = cos.reshape(shp), sin.reshape(shp)
    half = x.shape[-1] // 2
    x1, x2 = x[..., :half], x[..., half:]
    return jnp.concatenate([x1 * c - x2 * s, x2 * c + x1 * s], axis=-1)


def _rope_rows(x, n_ctx, dec_b, dec_s, cos, sin):
    lat = x[n_ctx:].reshape((dec_b, dec_s) + x.shape[1:])
    lat = _rope(lat, cos, sin).reshape((dec_b * dec_s,) + x.shape[1:])
    return jnp.concatenate([x[:n_ctx], lat], axis=0)


def _even_mixer(h, dims, cache, p):
    (bp, sp, dec_b, dec_s, past) = dims
    n_ctx = bp * sp
    n = h.shape[0]
    (w_in, g_q, w_qb, g_kv, w_kvb, g_qn, g_kn, g_wq, g_wk, sink) = p
    ckv_c, krope_c, wk_c, wv_c = cache
    o1 = MLA_Q_RANK
    o2 = o1 + MLA_KV_RANK
    o3 = o2 + MLA_ROPE
    o4 = o3 + WIN_HEADS * HEAD_DIM
    o5 = o4 + WIN_KV * HEAD_DIM
    pad = (-(w_in.shape[1])) % V7X_LANES
    w_in_p = jnp.concatenate([w_in[:, :o2], w_in[:, o3:], w_in[:, o2:o3],
                              jnp.zeros((w_in.shape[0], pad), w_in.dtype)], axis=1)
    proj = _matmul(h, w_in_p, out_dtype=F32, name="even_in_proj")
    c1 = o2 + WIN_HEADS * HEAD_DIM
    c2 = c1 + WIN_KV * HEAD_DIM
    c3 = c2 + WIN_KV * HEAD_DIM
    q_lat = _rms(proj[:, :o1], g_q)
    c_kv = _rms(proj[:, o1:o2], g_kv)
    wq = _rms(proj[:, o2:c1].reshape(n, WIN_HEADS, HEAD_DIM), g_wq)
    wk = _rms(proj[:, c1:c2].reshape(n, WIN_KV, HEAD_DIM), g_wk)
    wv = proj[:, c2:c3]
    k_rope = proj[:, c3:c3 + MLA_ROPE]

    cos_r, sin_r = _rope_2d(dec_s, MLA_ROPE)
    cos_h, sin_h = _rope_2d(dec_s, HEAD_DIM)

    w_qb_p = jnp.pad(w_qb.reshape(MLA_Q_RANK, MLA_HEADS, MLA_QK),
                     ((0, 0), (0, 0), (0, MLA_QK_PAD - MLA_QK))).reshape(MLA_Q_RANK, MLA_HEADS * MLA_QK_PAD)
    q_mla = _matmul(q_lat.astype(BF16), w_qb_p, out_dtype=F32, name="mla_q_up")
    q_mla = _rms(q_mla.reshape(n, MLA_HEADS, MLA_QK_PAD)[..., :MLA_QK], g_qn)
    q_tail = _rope_rows(q_mla[..., MLA_NOPE:], n_ctx, dec_b, dec_s, cos_r, sin_r)
    q_mla = jnp.concatenate([q_mla[..., :MLA_NOPE], q_tail,
                             jnp.zeros((n, MLA_HEADS, MLA_QK_PAD - MLA_QK), F32)], axis=-1)
    q_mla = q_mla.reshape(n, MLA_HEADS * MLA_QK_PAD).astype(BF16)

    n_all = n + dec_b * past
    ckv_all = jnp.concatenate([c_kv, ckv_c.reshape(dec_b * past, MLA_KV_RANK)], axis=0)
    krope_all = jnp.concatenate([k_rope, krope_c.reshape(dec_b * past, MLA_ROPE)], axis=0)
    kv = _matmul(ckv_all.astype(BF16), w_kvb, out_dtype=F32, name="mla_kv_up")
    kv = kv.reshape(n_all, MLA_HEADS, MLA_NOPE + MLA_V)
    k_pe = jnp.broadcast_to(krope_all[:, None, :], (n_all, MLA_HEADS, MLA_ROPE))
    mk = _rms(jnp.concatenate([kv[..., :MLA_NOPE], k_pe], axis=-1), g_kn)
    k_tail = mk[..., MLA_NOPE:]
    lat_tail = _rope(k_tail[n_ctx:n].reshape(dec_b, dec_s, MLA_HEADS, MLA_ROPE), cos_r, sin_r)
    k_tail = jnp.concatenate([k_tail[:n_ctx], lat_tail.reshape(dec_b * dec_s, MLA_HEADS, MLA_ROPE),
                              k_tail[n:]], axis=0)
    mk = jnp.concatenate([mk[..., :MLA_NOPE], k_tail,
                          jnp.zeros((n_all, MLA_HEADS, MLA_QK_PAD - MLA_QK), F32)], axis=-1)
    mk = mk.reshape(n_all, MLA_HEADS * MLA_QK_PAD).astype(BF16)
    mv = kv[..., MLA_NOPE:].reshape(n_all, MLA_HEADS * MLA_V).astype(BF16)

    mla_scale = MLA_QK ** -0.5
    a_ctx = _attention(q_mla, 0, mk, mv, 0, batch=bp, seq=sp, nkv=MLA_HEADS, groups=1,
                       dq=MLA_QK_PAD, dv=MLA_V, scale=mla_scale, hb=MLA_HEADS, tq=256, name="mla_attn_ctx")
    a_lat = _attention(q_mla, n_ctx, mk, mv, n_ctx, batch=dec_b, seq=dec_s, nkv=MLA_HEADS, groups=1,
                       dq=MLA_QK_PAD, dv=MLA_V, scale=mla_scale, hb=4, tq=512,
                       k_ctx=mk, v_ctx=mv, kc_off=n, t_ctx=past, name="mla_attn_lat")

    grp = WIN_HEADS // WIN_KV
    sink_b = jnp.broadcast_to(sink.astype(F32)[:, None], (WIN_HEADS, V7X_LANES))
    wq_r = _rope_rows(wq, n_ctx, dec_b, dec_s, cos_h, sin_h).reshape(n, WIN_HEADS * HEAD_DIM).astype(BF16)
    wk_r = _rope_rows(wk, n_ctx, dec_b, dec_s, cos_h, sin_h).reshape(n, WIN_KV * HEAD_DIM).astype(BF16)
    wv_b = wv.astype(BF16)
    win_scale = HEAD_DIM ** -0.5
    b_ctx = _attention(wq_r, 0, wk_r, wv_b, 0, batch=bp, seq=sp, nkv=WIN_KV, groups=grp,
                       dq=HEAD_DIM, dv=HEAD_DIM, scale=win_scale, hb=WIN_KV, tq=256, sink=sink_b,
                       name="win_attn_ctx")
    b_lat = _attention(wq_r, n_ctx, wk_r, wv_b, n_ctx, batch=dec_b, seq=dec_s, nkv=WIN_KV, groups=grp,
                       dq=HEAD_DIM, dv=HEAD_DIM, scale=win_scale, hb=WIN_KV, tq=256,
                       k_ctx=wk_c.reshape(dec_b * past, WIN_KV * HEAD_DIM).astype(BF16),
                       v_ctx=wv_c.reshape(dec_b * past, WIN_KV * HEAD_DIM).astype(BF16),
                       kc_off=0, t_ctx=past, sink=sink_b, window=WINDOW, name="win_attn_lat")

    out = jnp.concatenate([jnp.concatenate([a_ctx, a_lat], axis=0),
                           jnp.concatenate([b_ctx, b_lat], axis=0)], axis=1)
    state = (c_kv[:n_ctx].reshape(bp, 1, sp, MLA_KV_RANK),
             k_rope[:n_ctx].reshape(bp, 1, sp, MLA_ROPE),
             wk[:n_ctx].reshape(bp, 1, sp, WIN_KV, HEAD_DIM),
             wv[:n_ctx].reshape(bp, 1, sp, WIN_KV, HEAD_DIM))
    return out, state


def _odd_mixer(h, dims, cache, p, lam_init, layer):
    (bp, sp, dec_b, dec_s, past) = dims
    n_ctx = bp * sp
    n = h.shape[0]
    (w_in, g_dq, g_dk, lq1, lk1, lq2, lk2, g_sub, g_aq, g_ak) = p
    dk_c, dv_c, ak_c, av_c = cache
    o1 = DIFF_HEADS * HEAD_DIM
    o2 = o1 + DIFF_KV * HEAD_DIM
    o3 = o2 + DIFF_KV * HEAD_DIM
    o4 = o3 + AX_HEADS * HEAD_DIM
    o5 = o4 + AX_KV * HEAD_DIM
    proj = _matmul(h, w_in, layer=layer, out_dtype=F32, name="odd_in_proj")
    dq = _rms(proj[:, :o1].reshape(n, DIFF_HEADS, 2, DIFF_D), g_dq)
    dk = _rms(proj[:, o1:o2].reshape(n, DIFF_KV, 2, DIFF_D), g_dk)
    dv = proj[:, o2:o3]
    aq = _rms(proj[:, o3:o4].reshape(n, AX_HEADS, HEAD_DIM), g_aq)
    ak = _rms(proj[:, o4:o5].reshape(n, AX_KV, HEAD_DIM), g_ak)
    av = proj[:, o5:]

    cos_r, sin_r = _rope_2d(dec_s, DIFF_D)
    cos_h, sin_h = _rope_2d(dec_s, HEAD_DIM)
    dq_r = _rope_rows(dq, n_ctx, dec_b, dec_s, cos_r, sin_r).reshape(n, DIFF_HEADS * HEAD_DIM).astype(BF16)
    dk_r = _rope_rows(dk, n_ctx, dec_b, dec_s, cos_r, sin_r).reshape(n, DIFF_KV * HEAD_DIM).astype(BF16)
    aq_r = _rope_rows(aq, n_ctx, dec_b, dec_s, cos_h, sin_h).reshape(n, AX_HEADS * HEAD_DIM).astype(BF16)
    ak_r = _rope_rows(ak, n_ctx, dec_b, dec_s, cos_h, sin_h).reshape(n, AX_KV * HEAD_DIM).astype(BF16)
    dv_b = dv.astype(BF16)
    av_b = av.astype(BF16)

    lam = (jnp.exp(jnp.sum(lq1.astype(F32) * lk1.astype(F32)))
           - jnp.exp(jnp.sum(lq2.astype(F32) * lk2.astype(F32))) + lam_init)
    lam_b = jnp.broadcast_to(lam.astype(F32), (1, HEAD_DIM))
    g_sub_b = g_sub.astype(F32).reshape(1, HEAD_DIM)
    grp = DIFF_HEADS // DIFF_KV
    diff_kw = dict(nkv=DIFF_KV, groups=grp, dq=HEAD_DIM, dv=HEAD_DIM, scale=DIFF_D ** -0.5, hb=DIFF_KV,
                   lam=lam_b, g_sub=g_sub_b, diff_post_scale=1.0 - lam_init)
    d_ctx = _attention(dq_r, 0, dk_r, dv_b, 0, batch=bp, seq=sp, tq=256, name="diff_attn_ctx", **diff_kw)
    d_lat = _attention(dq_r, n_ctx, dk_r, dv_b, n_ctx, batch=dec_b, seq=dec_s, tq=128,
                       k_ctx=dk_c.reshape(dec_b * past, DIFF_KV * HEAD_DIM).astype(BF16),
                       v_ctx=dv_c.reshape(dec_b * past, DIFF_KV * HEAD_DIM).astype(BF16),
                       kc_off=0, t_ctx=past, name="diff_attn_lat", **diff_kw)
    agrp = AX_HEADS // AX_KV
    ax_kw = dict(nkv=AX_KV, groups=agrp, dq=HEAD_DIM, dv=HEAD_DIM, scale=HEAD_DIM ** -0.5, hb=AX_KV)
    x_ctx = _attention(aq_r, 0, ak_r, av_b, 0, batch=bp, seq=sp, tq=256, name="ax_attn_ctx", **ax_kw)
    x_lat = _attention(aq_r, n_ctx, ak_r, av_b, n_ctx, batch=dec_b, seq=dec_s, tq=256,
                       k_ctx=ak_c.reshape(dec_b * past, AX_KV * HEAD_DIM).astype(BF16),
                       v_ctx=av_c.reshape(dec_b * past, AX_KV * HEAD_DIM).astype(BF16),
                       kc_off=0, t_ctx=past, name="ax_attn_lat", **ax_kw)
    out = jnp.concatenate([jnp.concatenate([d_ctx, d_lat], axis=0),
                           jnp.concatenate([x_ctx, x_lat], axis=0)], axis=1)
    state = (dk[:n_ctx].reshape(bp, 1, sp, DIFF_KV, 2, DIFF_D),
             dv[:n_ctx].reshape(bp, 1, sp, DIFF_KV, HEAD_DIM),
             ak[:n_ctx].reshape(bp, 1, sp, AX_KV, HEAD_DIM),
             av[:n_ctx].reshape(bp, 1, sp, AX_KV, HEAD_DIM))
    return out, state


def kernel(x_prompt, x_sample, cache_mla_ckv, cache_mla_krope, cache_win_k, cache_win_v, cache_diff_k, cache_diff_v, cache_ax_k, cache_ax_v, c, c_ctx, w_mod, b_mod, g_norm_mix, g_norm_ffn, w_in_even, g_mla_q, w_mla_qb, g_mla_kv, w_mla_kvb, g_mla_qn, g_mla_kn, g_win_qn, g_win_kn, win_sink, w_out_even, w_in_odd, g_diff_qn, g_diff_kn, diff_lq1, diff_lk1, diff_lq2, diff_lk2, g_diff_sub, g_ax_qn, g_ax_kn, w_out_odd, w_router, b_router, w_exp_gate, w_exp_up, w_exp_down, w_sh_gate, w_sh_up, w_sh_down):
    bp, sp, d = x_prompt.shape
    dec_b, dec_s, _ = x_sample.shape
    depth = w_mod.shape[0]
    n_ctx = bp * sp
    n = n_ctx + dec_b * dec_s
    dims = (bp, sp, dec_b, dec_s, cache_mla_ckv.shape[2])
    group_fn = lambda tm: _group_index_fn(tm, n_ctx, dec_s)
    n_groups = 1 + dec_b

    y = jnp.concatenate([x_prompt.reshape(n_ctx, d), x_sample.reshape(dec_b * dec_s, d)], axis=0)

    cond = jnp.concatenate([c_ctx[None], c, jnp.zeros((8 - n_groups % 8, d), F32)], axis=0)
    cond = jax.nn.silu(cond)

    states_even, states_odd = [], []
    for l in range(depth):
        i = l // 2
        mod = _matmul(cond, w_mod, layer=l, out_dtype=F32, bias=b_mod.reshape(depth, 1, 6 * d),
                      name="modulation")
        mod = mod[:n_groups].reshape(n_groups, 6, 1, d)
        sh1, sc1, g1, sh2, sc2, g2 = (mod[:, j] for j in range(6))
        h = _ada_norm(y, g_norm_mix[l][None], sh1, sc1, group_fn, name="ada_norm_mix")
        if l % 2 == 0:
            pe = (w_in_even[i], g_mla_q[i], w_mla_qb[i], g_mla_kv[i], w_mla_kvb[i], g_mla_qn[i],
                  g_mla_kn[i], g_win_qn[i], g_win_kn[i], win_sink[i])
            cache = (cache_mla_ckv[:, i], cache_mla_krope[:, i], cache_win_k[:, i], cache_win_v[:, i])
            out, state = _even_mixer(h, dims, cache, pe)
            states_even.append(state)
            w_out = w_out_even
        else:
            po = (w_in_odd, g_diff_qn[i], g_diff_kn[i], diff_lq1[i], diff_lk1[i], diff_lq2[i],
                  diff_lk2[i], g_diff_sub[i], g_ax_qn[i], g_ax_kn[i])
            cache = (cache_diff_k[:, i], cache_diff_v[:, i], cache_ax_k[:, i], cache_ax_v[:, i])
            lam_init = 0.8 - 0.6 * math.exp(-0.3 * l)
            out, state = _odd_mixer(h, dims, cache, po, lam_init, i)
            states_odd.append(state)
            w_out = w_out_odd
        y = _matmul(out, w_out, layer=i, out_dtype=F32, resid=y, gates=g1, group_fn=group_fn,
                    name="mixer_out_proj")
        y = _moe(y, g_norm_ffn[l][None], sh2, sc2, g2, group_fn, l, w_router, b_router,
                 w_exp_gate, w_exp_up, w_exp_down, w_sh_gate, w_sh_up, w_sh_down)

    yp = y[:n_ctx].reshape(bp, sp, d)
    ys = y[n_ctx:].reshape(dec_b, dec_s, d)
    even = tuple(jnp.concatenate([s[j] for s in states_even], axis=1) for j in range(4))
    odd = tuple(jnp.concatenate([s[j] for s in states_odd], axis=1) for j in range(4))
    return (yp, ys) + even + odd
```

```python
import functools
import math

import jax
import jax.numpy as jnp
from jax import lax
from jax.experimental import pallas as pl
from jax.experimental.pallas import tpu as pltpu

F32 = jnp.float32
BF16 = jnp.bfloat16

GRID_W = 64
ROPE_THETA = 10000.0
EPS = 1e-6
NEG_INF = -1e30
HEAD_DIM = 128
MLA_HEADS = 16
MLA_Q_RANK = 768
MLA_KV_RANK = 512
MLA_NOPE = 128
MLA_ROPE = 64
MLA_V = 128
MLA_QK = MLA_NOPE + MLA_ROPE
MLA_QK_PAD = 256
WIN_HEADS = 16
WIN_KV = 4
WINDOW = 128
DIFF_HEADS = 16
DIFF_KV = 4
DIFF_D = HEAD_DIM // 2
AX_HEADS = 16
AX_KV = 4
N_EXPERTS = 64
N_EXPERT_GROUPS = 8
TOPK_GROUPS = 4
TOP_K = 8
ROUTED_SCALE = 2.5

V7X_LANES = 128
V7X_VMEM_LIMIT_BYTES = 56 * 1024 * 1024
MM_WEIGHT_TILE_BYTES = 32 * 1024 * 1024
MOE_ROW_TILE = 256


def _cparams(sem):
    return pltpu.CompilerParams(dimension_semantics=sem, vmem_limit_bytes=V7X_VMEM_LIMIT_BYTES)


def _largest_tile(n, cap, step):
    if n <= cap:
        return n
    t = (cap // step) * step
    while t >= step:
        if n % t == 0:
            return t
        t -= step
    raise ValueError(f"no tile for {n} under {cap}")


def _group_index_fn(tm, n_ctx_rows, dec_rows):
    assert n_ctx_rows % tm == 0 and dec_rows % tm == 0
    ctx_tiles = n_ctx_rows // tm
    per_dec = dec_rows // tm

    def fn(i):
        return jnp.where(i < ctx_tiles, 0, 1 + (jnp.maximum(i - ctx_tiles, 0)) // per_dec)

    return fn


def _mm_body(*refs, has_bias, has_resid):
    x_ref, w_ref = refs[0], refs[1]
    idx = 2
    if has_bias:
        bias_ref = refs[idx]
        idx += 1
    if has_resid:
        resid_ref, gate_ref = refs[idx], refs[idx + 1]
        idx += 2
    o_ref, wb_ref = refs[idx], refs[idx + 1]

    @pl.when(pl.program_id(1) == 0)
    def _():
        wb_ref[...] = w_ref[...].astype(BF16)

    acc = jnp.dot(x_ref[...].astype(BF16), wb_ref[...], preferred_element_type=F32)
    if has_bias:
        acc = acc + bias_ref[...]
    if has_resid:
        acc = resid_ref[...] + gate_ref[...] * acc
    o_ref[...] = acc.astype(o_ref.dtype)


def _matmul(x, w, *, out_dtype, layer=None, bias=None, resid=None, gates=None, group_fn=None, name):
    m, k = x.shape
    n = w.shape[-1]
    tm = _largest_tile(m, 512, 8)
    tn_cap = max(V7X_LANES, min(2048, MM_WEIGHT_TILE_BYTES // (k * 10)))
    tn = _largest_tile(n, tn_cap, V7X_LANES)
    grid = (n // tn, m // tm)
    if layer is None:
        w_spec = pl.BlockSpec((k, tn), lambda j, i: (0, j))
    else:
        w_spec = pl.BlockSpec((None, k, tn), lambda j, i: (layer, 0, j))
    in_specs = [pl.BlockSpec((tm, k), lambda j, i: (i, 0)), w_spec]
    args = [x, w]
    if bias is not None:
        in_specs.append(pl.BlockSpec((None, 1, tn), lambda j, i: (layer, 0, j)))
        args.append(bias)
    if resid is not None:
        gfn = group_fn(tm)
        in_specs.append(pl.BlockSpec((tm, tn), lambda j, i: (i, j)))
        in_specs.append(pl.BlockSpec((None, 1, tn), lambda j, i: (gfn(i), 0, j)))
        args += [resid, gates]
    return pl.pallas_call(
        functools.partial(_mm_body, has_bias=bias is not None, has_resid=resid is not None),
        out_shape=jax.ShapeDtypeStruct((m, n), out_dtype),
        grid=grid,
        in_specs=in_specs,
        out_specs=pl.BlockSpec((tm, tn), lambda j, i: (i, j)),
        scratch_shapes=[pltpu.VMEM((k, tn), BF16)],
        compiler_params=_cparams(("arbitrary", "arbitrary")),
        name=name,
    )(*args)


def _first_max(cur, sub, limit):
    m = jnp.max(cur, axis=0, keepdims=True)
    first = jnp.min(jnp.where(cur == m, sub, limit), axis=0, keepdims=True)
    return m, first, sub == first


def _route_columns(logits, bias, tri, carry):
    e, t = logits.shape
    per = e // N_EXPERT_GROUPS
    scores = jax.nn.sigmoid(logits)
    biased = scores + bias
    neg = -jnp.inf
    gsub = lax.broadcasted_iota(jnp.int32, (per, t), 0)
    grp = []
    for g in range(N_EXPERT_GROUPS):
        xg = biased[g * per:(g + 1) * per]
        m1, _, hit = _first_max(xg, gsub, per)
        m2 = jnp.max(jnp.where(hit, neg, xg), axis=0, keepdims=True)
        grp.append(m1 + m2)
    grp = jnp.concatenate(grp, axis=0)
    nsub = lax.broadcasted_iota(jnp.int32, (N_EXPERT_GROUPS, t), 0)
    gsel = jnp.zeros((N_EXPERT_GROUPS, t), F32)
    for _ in range(TOPK_GROUPS):
        _, _, hit = _first_max(grp, nsub, N_EXPERT_GROUPS)
        gsel = jnp.where(hit, 1.0, gsel)
        grp = jnp.where(hit, neg, grp)
    emask = jnp.concatenate([jnp.broadcast_to(gsel[g:g + 1], (per, t)) for g in range(N_EXPERT_GROUPS)], axis=0)
    cur = jnp.where(emask > 0.5, biased, neg)
    esub = lax.broadcasted_iota(jnp.int32, (e, t), 0)
    sel = jnp.zeros((e, t), F32)
    ids, ws, hits = [], [], []
    for _ in range(TOP_K):
        _, first, hit = _first_max(cur, esub, e)
        ids.append(first)
        ws.append(jnp.sum(jnp.where(hit, scores, 0.0), axis=0, keepdims=True))
        hits.append(hit)
        sel = jnp.where(hit, 1.0, sel)
        cur = jnp.where(hit, neg, cur)
    w = jnp.concatenate(ws, axis=0)
    gates = w / jnp.sum(w, axis=0, keepdims=True) * ROUTED_SCALE
    rank_all = jnp.dot(sel.astype(BF16), tri, preferred_element_type=F32) + carry
    ranks = [jnp.sum(jnp.where(hit, rank_all, 0.0), axis=0, keepdims=True) for hit in hits]
    counts = jnp.sum(sel, axis=1, keepdims=True)
    return (jnp.concatenate(ids, axis=0), gates, jnp.concatenate(ranks, axis=0).astype(jnp.int32), counts)


def _adanorm_body(x_ref, g_ref, shift_ref, scale_ref, *rest, has_router):
    x = x_ref[...]
    y = x * lax.rsqrt(jnp.mean(x * x, axis=-1, keepdims=True) + EPS) * g_ref[...]
    t = y * (1.0 + scale_ref[...]) + shift_ref[...]
    if has_router:
        wrt_ref, br_ref, tri_ref, o_ref, ids_ref, gate_ref, rank_ref, cnt_ref, carry_ref = rest

        @pl.when(pl.program_id(0) == 0)
        def _():
            carry_ref[...] = jnp.zeros_like(carry_ref)

        logits = lax.dot_general(wrt_ref[...], t, (((1,), (1,)), ((), ())),
                                 precision=lax.Precision.HIGHEST, preferred_element_type=F32)
        ids, gates, ranks, counts = _route_columns(logits, br_ref[...], tri_ref[...], carry_ref[:, :1])
        ids_ref[...] = ids
        gate_ref[...] = gates
        rank_ref[...] = ranks
        carry_ref[...] = carry_ref[...] + counts
        cnt_ref[...] = carry_ref[...]
    else:
        (o_ref,) = rest
    o_ref[...] = t.astype(o_ref.dtype)


def _ada_norm(x, g, shift, scale, group_fn, *, w_router=None, b_router=None, name):
    n, d = x.shape
    tm = 256
    gfn = group_fn(tm)
    in_specs = [pl.BlockSpec((tm, d), lambda i: (i, 0)),
                pl.BlockSpec((1, d), lambda i: (0, 0)),
                pl.BlockSpec((None, 1, d), lambda i: (gfn(i), 0, 0)),
                pl.BlockSpec((None, 1, d), lambda i: (gfn(i), 0, 0))]
    args = [x, g, shift, scale]
    out_shape = jax.ShapeDtypeStruct((n, d), BF16)
    out_specs = pl.BlockSpec((tm, d), lambda i: (i, 0))
    scratch = []
    if w_router is not None:
        e = w_router.shape[1]
        tri = (lax.broadcasted_iota(jnp.int32, (tm, tm), 0) < lax.broadcasted_iota(jnp.int32, (tm, tm), 1))
        in_specs += [pl.BlockSpec((e, d), lambda i: (0, 0)),
                     pl.BlockSpec((e, tm), lambda i: (0, 0)),
                     pl.BlockSpec((tm, tm), lambda i: (0, 0))]
        args += [w_router.T, jnp.broadcast_to(b_router.astype(F32)[:, None], (e, tm)), tri.astype(BF16)]
        col = lambda rows, dt: (jax.ShapeDtypeStruct((rows, n), dt), pl.BlockSpec((rows, tm), lambda i: (0, i)))
        extra = [col(TOP_K, jnp.int32), col(TOP_K, F32), col(TOP_K, jnp.int32),
                 (jax.ShapeDtypeStruct((e, V7X_LANES), F32), pl.BlockSpec((e, V7X_LANES), lambda i: (0, 0)))]
        out_shape = (out_shape,) + tuple(s for s, _ in extra)
        out_specs = (out_specs,) + tuple(b for _, b in extra)
        scratch = [pltpu.VMEM((e, V7X_LANES), F32)]
    return pl.pallas_call(
        functools.partial(_adanorm_body, has_router=w_router is not None),
        out_shape=out_shape,
        grid=(n // tm,),
        in_specs=in_specs,
        out_specs=out_specs,
        scratch_shapes=scratch,
        compiler_params=_cparams(("arbitrary",)),
        name=name,
    )(*args)


def _attn_body(*refs, hb, groups, dq, dv, tq, tk, scale, t_new, t_ctx, window, has_sink,
               diff_post_scale):
    it = iter(refs)
    q_ref, kn_ref, vn_ref = next(it), next(it), next(it)
    kc_ref = vc_ref = sink_ref = lam_ref = gsub_ref = None
    if t_ctx:
        kc_ref, vc_ref = next(it), next(it)
    if has_sink:
        sink_ref = next(it)
    diff = diff_post_scale is not None
    if diff:
        lam_ref, gsub_ref = next(it), next(it)
    o_ref = next(it)

    h = pl.program_id(1)
    i = pl.program_id(2)
    nstack = 2 * groups if diff else groups
    rows = nstack * tq

    def stack_rows(parts):
        return parts[0] if len(parts) == 1 else jnp.concatenate(parts, axis=0)

    if window is not None:
        wk = min(t_new, tq + 2 * window)
        if wk == t_new:
            wstart = 0
        else:
            wstart = pl.multiple_of(jnp.clip(i * tq - window, 0, t_new - wk), V7X_LANES)
        qi = i * tq + lax.broadcasted_iota(jnp.int32, (tq, wk), 0)
        kj = wstart + lax.broadcasted_iota(jnp.int32, (tq, wk), 1)
        wmask = stack_rows([jnp.abs(qi - kj) <= window] * nstack)

    for j in range(hb):
        qj = q_ref[:, j * groups * dq:(j + 1) * groups * dq]
        parts = [qj[:, g * dq:(g + 1) * dq] for g in range(groups)]
        if diff:
            lo = lax.broadcasted_iota(jnp.int32, (tq, dq), 1) < dq // 2
            zero = jnp.zeros((tq, dq), qj.dtype)
            parts = [jnp.where(lo, p, zero) for p in parts] + [jnp.where(lo, zero, p) for p in parts]
        qs = stack_rows(parts)

        if has_sink:
            sinks = []
            for g in range(groups):
                hh = (h * hb + j) * groups + g
                sinks.append(jnp.broadcast_to(sink_ref[pl.ds(hh, 1), :][:, :1], (tq, 1)))
            m = stack_rows(sinks)
            l = jnp.ones((rows, 1), F32)
        else:
            m = jnp.full((rows, 1), NEG_INF, F32)
            l = jnp.zeros((rows, 1), F32)
        acc = jnp.zeros((rows, dv), F32)

        def step(carry, kc, vc, mask):
            m, l, acc = carry
            s = lax.dot_general(qs, kc, (((1,), (1,)), ((), ())), preferred_element_type=F32) * scale
            if mask is not None:
                s = jnp.where(mask, s, NEG_INF)
            m_new = jnp.maximum(m, jnp.max(s, axis=-1, keepdims=True))
            alpha = jnp.exp(m - m_new)
            p = jnp.exp(s - m_new)
            l = alpha * l + jnp.sum(p, axis=-1, keepdims=True)
            acc = alpha * acc + jnp.dot(p.astype(BF16), vc, preferred_element_type=F32)
            return m_new, l, acc

        carry = (m, l, acc)
        kcols = slice(j * dq, (j + 1) * dq)
        vcols = slice(j * dv, (j + 1) * dv)
        if window is not None:
            carry = step(carry, kn_ref[pl.ds(wstart, wk), kcols], vn_ref[pl.ds(wstart, wk), vcols], wmask)
        else:
            for c in range(t_new // tk):
                carry = step(carry, kn_ref[c * tk:(c + 1) * tk, kcols], vn_ref[c * tk:(c + 1) * tk, vcols], None)
        if t_ctx:
            tkc = min(tk, t_ctx)
            for c in range(t_ctx // tkc):
                carry = step(carry, kc_ref[c * tkc:(c + 1) * tkc, kcols], vc_ref[c * tkc:(c + 1) * tkc, vcols], None)
        m, l, acc = carry
        o = acc / l
        if diff:
            half = groups * tq
            d = o[:half] - lam_ref[...] * o[half:]
            o = (d * lax.rsqrt(jnp.mean(d * d, axis=-1, keepdims=True) + EPS) * gsub_ref[...]) * diff_post_scale
        outs = [o[g * tq:(g + 1) * tq] for g in range(groups)]
        oj = outs[0] if groups == 1 else jnp.concatenate(outs, axis=1)
        o_ref[:, j * groups * dv:(j + 1) * groups * dv] = oj.astype(o_ref.dtype)


def _attention(q, q_off, k_new, v_new, kn_off, *, batch, seq, nkv, groups, dq, dv, scale, hb, tq,
               k_ctx=None, v_ctx=None, kc_off=0, t_ctx=0, sink=None, window=None,
               lam=None, g_sub=None, diff_post_scale=None, name):
    t_new = seq
    tq = min(tq, seq)
    tk = min(512, t_new)
    assert seq % tq == 0 and t_new % tk == 0 and nkv % hb == 0
    assert q_off % tq == 0 and kn_off % t_new == 0
    qb, nb = q_off // tq, kn_off // t_new
    spt = seq // tq
    in_specs = [pl.BlockSpec((tq, hb * groups * dq), lambda b, h, i: (qb + b * spt + i, h)),
                pl.BlockSpec((t_new, hb * dq), lambda b, h, i: (nb + b, h)),
                pl.BlockSpec((t_new, hb * dv), lambda b, h, i: (nb + b, h))]
    args = [q, k_new, v_new]
    if t_ctx:
        assert kc_off % t_ctx == 0
        cb = kc_off // t_ctx
        in_specs += [pl.BlockSpec((t_ctx, hb * dq), lambda b, h, i: (cb + b, h)),
                     pl.BlockSpec((t_ctx, hb * dv), lambda b, h, i: (cb + b, h))]
        args += [k_ctx, v_ctx]
    if sink is not None:
        in_specs.append(pl.BlockSpec(sink.shape, lambda b, h, i: (0, 0)))
        args.append(sink)
    if diff_post_scale is not None:
        in_specs += [pl.BlockSpec((1, dv), lambda b, h, i: (0, 0))] * 2
        args += [lam, g_sub]
    body = functools.partial(
        _attn_body, hb=hb, groups=groups, dq=dq, dv=dv, tq=tq, tk=tk, scale=scale, t_new=t_new,
        t_ctx=t_ctx, window=window, has_sink=sink is not None, diff_post_scale=diff_post_scale)
    return pl.pallas_call(
        body,
        out_shape=jax.ShapeDtypeStruct((batch * seq, nkv * groups * dv), BF16),
        grid=(batch, nkv // hb, spt),
        in_specs=in_specs,
        out_specs=pl.BlockSpec((tq, hb * groups * dv), lambda b, h, i: (b * spt + i, h)),
        compiler_params=_cparams(("arbitrary", "arbitrary", "arbitrary")),
        name=name,
    )(*args)


def _experts_body(te_ref, nv_ref, x_ref, wg_ref, wu_ref, wd_ref, o_ref, wgb, wub, wdb):
    t = pl.program_id(0)
    valid = t < nv_ref[0]
    prev = te_ref[jnp.maximum(t - 1, 0)]
    first = jnp.logical_or(t == 0, te_ref[t] != prev)

    @pl.when(jnp.logical_and(first, valid))
    def _():
        wgb[...] = wg_ref[...].astype(BF16)
        wub[...] = wu_ref[...].astype(BF16)
        wdb[...] = wd_ref[...].astype(BF16)

    @pl.when(valid)
    def _():
        x = x_ref[...]
        a = jnp.dot(x, wgb[...], preferred_element_type=F32)
        u = jnp.dot(x, wub[...], preferred_element_type=F32)
        hcur = a * jax.nn.sigmoid(a) * u
        o_ref[...] = jnp.dot(hcur.astype(BF16), wdb[...], preferred_element_type=F32).astype(o_ref.dtype)


def _routed_experts(xs, tile_expert, n_valid, w_gate, w_up, w_down, layer):
    r, d = xs.shape
    ff = w_gate.shape[-1]
    tm = MOE_ROW_TILE
    row_blk = lambda t, te, nv: (jnp.minimum(t, nv[0] - 1), 0)
    grid_spec = pltpu.PrefetchScalarGridSpec(
        num_scalar_prefetch=2,
        grid=(r // tm,),
        in_specs=[pl.BlockSpec((tm, d), row_blk),
                  pl.BlockSpec((None, None, d, ff), lambda t, te, nv: (layer, te[t], 0, 0)),
                  pl.BlockSpec((None, None, d, ff), lambda t, te, nv: (layer, te[t], 0, 0)),
                  pl.BlockSpec((None, None, ff, d), lambda t, te, nv: (layer, te[t], 0, 0))],
        out_specs=pl.BlockSpec((tm, d), row_blk),
        scratch_shapes=[pltpu.VMEM((d, ff), BF16), pltpu.VMEM((d, ff), BF16), pltpu.VMEM((ff, d), BF16)],
    )
    return pl.pallas_call(
        _experts_body,
        out_shape=jax.ShapeDtypeStruct((r, d), BF16),
        grid_spec=grid_spec,
        compiler_params=_cparams(("arbitrary",)),
        name="routed_experts",
    )(tile_expert, n_valid, xs, w_gate, w_up, w_down)


def _shared_body(t_ref, wg_ref, wu_ref, wd_ref, resid_ref, gate_ref, o_ref, h_ref):
    @pl.when(pl.program_id(1) == 0)
    def _():
        x = t_ref[...]
        a = jnp.dot(x, wg_ref[...], preferred_element_type=F32)
        u = jnp.dot(x, wu_ref[...], preferred_element_type=F32)
        h_ref[...] = (a * jax.nn.sigmoid(a) * u).astype(BF16)

    shared = jnp.dot(h_ref[...], wd_ref[...], preferred_element_type=F32)
    o_ref[...] = resid_ref[...] + gate_ref[...] * shared


def _shared_expert(t, ws_gate, ws_up, ws_down, resid, gates, group_fn):
    n, d = t.shape
    ff = ws_gate.shape[1]
    tm = _largest_tile(n, 512, 8)
    tn = _largest_tile(d, 1024, V7X_LANES)
    gfn = group_fn(tm)
    return pl.pallas_call(
        _shared_body,
        out_shape=jax.ShapeDtypeStruct((n, d), F32),
        grid=(n // tm, d // tn),
        in_specs=[pl.BlockSpec((tm, d), lambda i, j: (i, 0)),
                  pl.BlockSpec((d, ff), lambda i, j: (0, 0)),
                  pl.BlockSpec((d, ff), lambda i, j: (0, 0)),
                  pl.BlockSpec((ff, tn), lambda i, j: (0, j)),
                  pl.BlockSpec((tm, tn), lambda i, j: (i, j)),
                  pl.BlockSpec((None, 1, tn), lambda i, j: (gfn(i), 0, j))],
        out_specs=pl.BlockSpec((tm, tn), lambda i, j: (i, j)),
        scratch_shapes=[pltpu.VMEM((tm, ff), BF16)],
        compiler_params=_cparams(("arbitrary", "arbitrary")),
        name="shared_expert",
    )(t, ws_gate, ws_up, ws_down, resid, gates)


def _combine_body(rows_ref, w_ref, base_ref, gate_ref, o_ref):
    w = w_ref[...]
    acc = w[:, 0:1] * rows_ref[0].astype(F32)
    for c in range(1, rows_ref.shape[0]):
        acc = acc + w[:, c:c + 1] * rows_ref[c].astype(F32)
    o_ref[...] = base_ref[...] + gate_ref[...] * acc


def _combine(picked, weights, base, gates, group_fn):
    n, d = base.shape
    k = picked.shape[0]
    tm = 128
    gfn = group_fn(tm)
    return pl.pallas_call(
        _combine_body,
        out_shape=jax.ShapeDtypeStruct((n, d), F32),
        grid=(n // tm,),
        in_specs=[pl.BlockSpec((k, tm, d), lambda i: (0, i, 0)),
                  pl.BlockSpec((tm, k), lambda i: (i, 0)),
                  pl.BlockSpec((tm, d), lambda i: (i, 0)),
                  pl.BlockSpec((None, 1, d), lambda i: (gfn(i), 0, 0))],
        out_specs=pl.BlockSpec((tm, d), lambda i: (i, 0)),
        compiler_params=_cparams(("arbitrary",)),
        name="moe_combine",
    )(picked, weights, base, gates)


def _plan_body(rs_ref, ids_ref, rank_ref, pos_ref):
    ids = ids_ref[...]
    base = jnp.zeros_like(ids)
    for e in range(N_EXPERTS):
        base = jnp.where(ids == e, rs_ref[e], base)
    pos_ref[...] = base + rank_ref[...]


def _dispatch_plan(ids, ranks, counts, tm):
    k, n = ids.shape
    e = counts.shape[0]
    n_rows = n * k + e * tm
    n_tiles = n_rows // tm
    tiles_e = (counts + tm - 1) // tm
    tile_end = jnp.cumsum(tiles_e)
    row_start = ((tile_end - tiles_e) * tm).astype(jnp.int32)
    tn = _largest_tile(n, 2048, V7X_LANES)
    pos = pl.pallas_call(
        _plan_body,
        out_shape=jax.ShapeDtypeStruct((k, n), jnp.int32),
        grid_spec=pltpu.PrefetchScalarGridSpec(
            num_scalar_prefetch=1,
            grid=(n // tn,),
            in_specs=[pl.BlockSpec((k, tn), lambda i, rs: (0, i))] * 2,
            out_specs=pl.BlockSpec((k, tn), lambda i, rs: (0, i))),
        compiler_params=_cparams(("arbitrary",)),
        name="dispatch_rows",
    )(row_start, ids, ranks)
    flat = pos.reshape(-1)
    tok = jnp.tile(jnp.arange(n, dtype=jnp.int32), k)
    row_token = (jnp.arange(n_rows, dtype=jnp.int32) % n).at[flat].set(tok)
    tile_ids = jnp.arange(n_tiles, dtype=jnp.int32)
    tile_expert = jnp.sum((tile_end[None, :] <= tile_ids[:, None]).astype(jnp.int32), axis=1)
    tile_expert = jnp.minimum(tile_expert, e - 1)
    return flat, row_token, tile_expert, tile_end[-1:].astype(jnp.int32)


def _moe(y, g_ffn, shift, scale, gates, group_fn, layer, w_router, b_router, w_gate, w_up, w_down,
         ws_gate, ws_up, ws_down):
    n, d = y.shape
    t, ids, gate_w, ranks, counts = _ada_norm(y, g_ffn, shift, scale, group_fn, w_router=w_router[layer],
                                              b_router=b_router[layer], name="ada_norm_route")
    flat, row_token, tile_expert, n_valid = _dispatch_plan(
        ids, ranks, counts[:, 0].astype(jnp.int32), MOE_ROW_TILE)
    xs = jnp.take(t, row_token, axis=0)
    base = _shared_expert(t, ws_gate[layer].astype(BF16), ws_up[layer].astype(BF16),
                          ws_down[layer].astype(BF16), y, gates, group_fn)
    rows = _routed_experts(xs, tile_expert, n_valid, w_gate, w_up, w_down, layer)
    picked = jnp.take(rows, flat, axis=0).reshape(TOP_K, n, d)
    return _combine(picked, gate_w.T, base, gates, group_fn)


def _rope_2d(n_tok, rot_dim):
    rows = n_tok // GRID_W
    row = jnp.broadcast_to(jnp.arange(rows, dtype=F32)[:, None], (rows, GRID_W)).reshape(-1)
    col = jnp.broadcast_to(jnp.arange(GRID_W, dtype=F32)[None, :], (rows, GRID_W)).reshape(-1)
    n_freq = rot_dim // 4
    inv = ROPE_THETA ** (-jnp.arange(n_freq, dtype=F32) / n_freq)
    ang = jnp.concatenate([row[:, None] * inv, col[:, None] * inv], axis=-1)
    return jnp.cos(ang), jnp.sin(ang)


def _rope_tables(rot_dim, n_ctx, dec_b, dec_s, n_tail):
    c, s = _rope_2d(dec_s, rot_dim)
    reps = V7X_LANES // rot_dim
    cos = jnp.concatenate([c, c] * reps, axis=1)
    sin = jnp.concatenate([-s, s] * reps, axis=1)
    if n_tail:
        fill = V7X_LANES - rot_dim
        cos = jnp.concatenate([c, c, jnp.ones((dec_s, fill), F32)], axis=1)
        sin = jnp.concatenate([-s, s, jnp.zeros((dec_s, fill), F32)], axis=1)
    ones = lambda r: jnp.ones((r, V7X_LANES), F32)
    zeros = lambda r: jnp.zeros((r, V7X_LANES), F32)
    cos = jnp.concatenate([ones(n_ctx)] + [cos] * dec_b + [ones(n_tail)], axis=0)
    sin = jnp.concatenate([zeros(n_ctx)] + [sin] * dec_b + [zeros(n_tail)], axis=0)
    return cos, sin


def _rotate_half(y, seg):
    half = seg // 2
    if seg == V7X_LANES:
        return pltpu.roll(y, half, axis=1)
    lane = lax.broadcasted_iota(jnp.int32, y.shape, 1)
    return jnp.where(lane % seg < half, pltpu.roll(y, V7X_LANES - half, axis=1), pltpu.roll(y, half, axis=1))


def _rotate_half_tail(t):
    half = MLA_ROPE // 2
    lane = lax.broadcasted_iota(jnp.int32, t.shape, 1)
    return jnp.where(lane < half, pltpu.roll(t, V7X_LANES - half, axis=1), pltpu.roll(t, half, axis=1))


def _seg_rms(x, seg):
    sq = x * x
    if seg == V7X_LANES:
        return x * lax.rsqrt(jnp.mean(sq, axis=-1, keepdims=True) + EPS)
    assert seg * 2 == V7X_LANES
    lo = lax.broadcasted_iota(jnp.int32, x.shape, 1) < seg
    s_lo = jnp.sum(jnp.where(lo, sq, 0.0), axis=-1, keepdims=True)
    s_hi = jnp.sum(jnp.where(lo, 0.0, sq), axis=-1, keepdims=True)
    return x * lax.rsqrt(jnp.where(lo, s_lo, s_hi) / seg + EPS)


def _prep_body(proj_ref, *refs, plan):
    it = iter(refs)
    inputs = []
    for (_, _, kind, _) in plan:
        if kind in ("norm", "norm_f32"):
            inputs.append((next(it),))
        elif kind.startswith("heads"):
            inputs.append((next(it), next(it), next(it)))
        else:
            inputs.append(())
    outs = list(it)
    oi = 0
    for (col, width, kind, seg), ins in zip(plan, inputs):
        if kind in ("norm", "norm_f32"):
            (g_ref,) = ins
            x = proj_ref[:, col:col + width]
            y = x * lax.rsqrt(jnp.mean(x * x, axis=-1, keepdims=True) + EPS) * g_ref[...]
            outs[oi][...] = y.astype(outs[oi].dtype)
            oi += 1
        elif kind.startswith("heads"):
            g_ref, cos_ref, sin_ref = ins
            with_state = kind.endswith("+state")
            cos, sin, g = cos_ref[...], sin_ref[...], g_ref[...]
            for c0 in range(0, width, V7X_LANES):
                y = _seg_rms(proj_ref[:, col + c0:col + c0 + V7X_LANES], seg) * g
                if with_state:
                    outs[oi + 1][:, c0:c0 + V7X_LANES] = y
                y = y * cos + _rotate_half(y, seg) * sin
                outs[oi][:, c0:c0 + V7X_LANES] = y.astype(BF16)
            oi += 2 if with_state else 1
        else:
            outs[oi][...] = proj_ref[:, col:col + width].astype(BF16)
            oi += 1


def _prep(proj, plan, params, name):
    n, width_all = proj.shape
    tm = 256
    in_specs = [pl.BlockSpec((tm, width_all), lambda i: (i, 0))]
    args = [proj]
    out_shape, out_specs = [], []
    for (col, width, kind, seg), ps in zip(plan, params):
        row_spec = pl.BlockSpec((tm, width), lambda i: (i, 0))
        if kind in ("norm", "norm_f32"):
            in_specs.append(pl.BlockSpec((1, width), lambda i: (0, 0)))
            args.append(ps[0].astype(F32).reshape(1, width))
            out_shape.append(jax.ShapeDtypeStruct((n, width), F32 if kind == "norm_f32" else BF16))
            out_specs.append(row_spec)
        elif kind.startswith("heads"):
            g, cos, sin = ps
            in_specs += [pl.BlockSpec((1, V7X_LANES), lambda i: (0, 0)),
                         pl.BlockSpec((tm, V7X_LANES), lambda i: (i, 0)),
                         pl.BlockSpec((tm, V7X_LANES), lambda i: (i, 0))]
            args += [jnp.tile(g.astype(F32), V7X_LANES // seg).reshape(1, V7X_LANES), cos, sin]
            out_shape.append(jax.ShapeDtypeStruct((n, width), BF16))
            out_specs.append(row_spec)
            if kind.endswith("+state"):
                out_shape.append(jax.ShapeDtypeStruct((n, width), F32))
                out_specs.append(row_spec)
        else:
            out_shape.append(jax.ShapeDtypeStruct((n, width), BF16))
            out_specs.append(row_spec)
    return pl.pallas_call(
        functools.partial(_prep_body, plan=plan),
        out_shape=tuple(out_shape),
        grid=(n // tm,),
        in_specs=in_specs,
        out_specs=tuple(out_specs),
        compiler_params=_cparams(("arbitrary",)),
        name=name,
    )(*args)


def _mla_q_body(x_ref, w_ref, g_ref, cos_ref, sin_ref, o_ref):
    acc = jnp.dot(x_ref[...], w_ref[...], preferred_element_type=F32)
    cos, sin = cos_ref[...], sin_ref[...]
    for h in range(MLA_HEADS):
        c0 = h * MLA_QK_PAD
        nope = acc[:, c0:c0 + MLA_NOPE]
        tail = acc[:, c0 + MLA_NOPE:c0 + MLA_QK_PAD]
        ss = jnp.sum(nope * nope, axis=-1, keepdims=True) + jnp.sum(tail * tail, axis=-1, keepdims=True)
        r = lax.rsqrt(ss / MLA_QK + EPS)
        o_ref[:, c0:c0 + MLA_NOPE] = (nope * r * g_ref[:, :MLA_NOPE]).astype(BF16)
        t = tail * r * g_ref[:, MLA_NOPE:]
        t = t * cos + _rotate_half_tail(t) * sin
        o_ref[:, c0 + MLA_NOPE:c0 + MLA_QK_PAD] = t.astype(BF16)


def _mla_q_up(q_lat, w_qb_p, g_qn_p, cos, sin):
    n, k = q_lat.shape
    width = w_qb_p.shape[1]
    tm = 256
    return pl.pallas_call(
        _mla_q_body,
        out_shape=jax.ShapeDtypeStruct((n, width), BF16),
        grid=(n // tm,),
        in_specs=[pl.BlockSpec((tm, k), lambda i: (i, 0)),
                  pl.BlockSpec((k, width), lambda i: (0, 0)),
                  pl.BlockSpec((1, MLA_QK_PAD), lambda i: (0, 0)),
                  pl.BlockSpec((tm, V7X_LANES), lambda i: (i, 0)),
                  pl.BlockSpec((tm, V7X_LANES), lambda i: (i, 0))],
        out_specs=pl.BlockSpec((tm, width), lambda i: (i, 0)),
        compiler_params=_cparams(("arbitrary",)),
        name="mla_q_up",
    )(q_lat, w_qb_p, g_qn_p, cos, sin)


def _mla_kv_body(x_ref, w_ref, kr_ref, g_ref, cos_ref, sin_ref, k_ref, v_ref):
    acc = jnp.dot(x_ref[...].astype(BF16), w_ref[...], preferred_element_type=F32)
    cos, sin = cos_ref[...], sin_ref[...]
    kr = kr_ref[...]
    kr_ss = jnp.sum(kr * kr, axis=-1, keepdims=True)
    per = MLA_NOPE + MLA_V
    for h in range(MLA_HEADS):
        nope = acc[:, h * per:h * per + MLA_NOPE]
        r = lax.rsqrt((jnp.sum(nope * nope, axis=-1, keepdims=True) + kr_ss) / MLA_QK + EPS)
        c0 = h * MLA_QK_PAD
        k_ref[:, c0:c0 + MLA_NOPE] = (nope * r * g_ref[:, :MLA_NOPE]).astype(BF16)
        t = kr * r * g_ref[:, MLA_NOPE:]
        t = t * cos + _rotate_half_tail(t) * sin
        k_ref[:, c0 + MLA_NOPE:c0 + MLA_QK_PAD] = t.astype(BF16)
        v_ref[:, h * MLA_V:(h + 1) * MLA_V] = acc[:, h * per + MLA_NOPE:(h + 1) * per].astype(BF16)


def _mla_kv_up(c_kv, w_kvb, k_rope_p, g_kn_p, cos, sin):
    n, k = c_kv.shape
    tm = 256
    return pl.pallas_call(
        _mla_kv_body,
        out_shape=(jax.ShapeDtypeStruct((n, MLA_HEADS * MLA_QK_PAD), BF16),
                   jax.ShapeDtypeStruct((n, MLA_HEADS * MLA_V), BF16)),
        grid=(n // tm,),
        in_specs=[pl.BlockSpec((tm, k), lambda i: (i, 0)),
                  pl.BlockSpec(w_kvb.shape, lambda i: (0, 0)),
                  pl.BlockSpec((tm, V7X_LANES), lambda i: (i, 0)),
                  pl.BlockSpec((1, MLA_QK_PAD), lambda i: (0, 0)),
                  pl.BlockSpec((tm, V7X_LANES), lambda i: (i, 0)),
                  pl.BlockSpec((tm, V7X_LANES), lambda i: (i, 0))],
        out_specs=(pl.BlockSpec((tm, MLA_HEADS * MLA_QK_PAD), lambda i: (i, 0)),
                   pl.BlockSpec((tm, MLA_HEADS * MLA_V), lambda i: (i, 0))),
        compiler_params=_cparams(("arbitrary",)),
        name="mla_kv_up",
    )(c_kv, w_kvb, k_rope_p, g_kn_p, cos, sin)


def _even_mixer(h, dims, cache, p, tables):
    (bp, sp, dec_b, dec_s, past) = dims
    n_ctx = bp * sp
    n = h.shape[0]
    (w_in, g_q, w_qb, g_kv, w_kvb, g_qn, g_kn, g_wq, g_wk, sink) = p
    ckv_c, krope_c, wk_c, wv_c = cache
    (cos_h, sin_h), (cos_m, sin_m) = tables["head"], tables["mla"]
    o1 = MLA_Q_RANK
    o2 = o1 + MLA_KV_RANK
    o3 = o2 + MLA_ROPE
    pad = (-(w_in.shape[1])) % V7X_LANES
    w_in_p = jnp.concatenate([w_in[:, :o2], w_in[:, o3:], w_in[:, o2:o3],
                              jnp.zeros((w_in.shape[0], pad), w_in.dtype)], axis=1)
    proj = _matmul(h, w_in_p, out_dtype=F32, name="even_in_proj")
    c1 = o2 + WIN_HEADS * HEAD_DIM
    c2 = c1 + WIN_KV * HEAD_DIM
    c3 = c2 + WIN_KV * HEAD_DIM
    plan = ((0, o1, "norm", 0), (o1, MLA_KV_RANK, "norm_f32", 0),
            (o2, WIN_HEADS * HEAD_DIM, "heads", HEAD_DIM),
            (c1, WIN_KV * HEAD_DIM, "heads+state", HEAD_DIM),
            (c2, WIN_KV * HEAD_DIM, "cast", 0))
    q_lat, c_kv, wq_r, wk_r, wk, wv_b = _prep(
        proj, plan, ((g_q,), (g_kv,), (g_wq, cos_h, sin_h), (g_wk, cos_h, sin_h), ()), "even_prep")
    wv = proj[:n_ctx, c2:c3]
    k_rope_p = proj[:, c3:c3 + V7X_LANES]

    head_pad = ((0, 0), (0, 0), (0, MLA_QK_PAD - MLA_QK))
    w_qb_p = jnp.pad(w_qb.reshape(MLA_Q_RANK, MLA_HEADS, MLA_QK), head_pad)
    w_qb_p = w_qb_p.reshape(MLA_Q_RANK, MLA_HEADS * MLA_QK_PAD).astype(BF16)
    g_qn_p = jnp.pad(g_qn.astype(F32), (0, MLA_QK_PAD - MLA_QK)).reshape(1, MLA_QK_PAD)
    g_kn_p = jnp.pad(g_kn.astype(F32), (0, MLA_QK_PAD - MLA_QK)).reshape(1, MLA_QK_PAD)
    q_mla = _mla_q_up(q_lat, w_qb_p, g_qn_p, cos_m, sin_m)

    ckv_all = jnp.concatenate([c_kv, ckv_c.reshape(dec_b * past, MLA_KV_RANK)], axis=0)
    krope_cache = jnp.pad(krope_c.reshape(dec_b * past, MLA_ROPE), ((0, 0), (0, V7X_LANES - MLA_ROPE)))
    krope_all = jnp.concatenate([k_rope_p, krope_cache], axis=0)
    mk, mv = _mla_kv_up(ckv_all, w_kvb.astype(BF16), krope_all, g_kn_p, cos_m, sin_m)

    mla_scale = MLA_QK ** -0.5
    a_ctx = _attention(q_mla, 0, mk, mv, 0, batch=bp, seq=sp, nkv=MLA_HEADS, groups=1,
                       dq=MLA_QK_PAD, dv=MLA_V, scale=mla_scale, hb=MLA_HEADS, tq=256, name="mla_attn_ctx")
    a_lat = _attention(q_mla, n_ctx, mk, mv, n_ctx, batch=dec_b, seq=dec_s, nkv=MLA_HEADS, groups=1,
                       dq=MLA_QK_PAD, dv=MLA_V, scale=mla_scale, hb=4, tq=512,
                       k_ctx=mk, v_ctx=mv, kc_off=n, t_ctx=past, name="mla_attn_lat")

    grp = WIN_HEADS // WIN_KV
    sink_b = jnp.broadcast_to(sink.astype(F32)[:, None], (WIN_HEADS, V7X_LANES))
    win_scale = HEAD_DIM ** -0.5
    b_ctx = _attention(wq_r, 0, wk_r, wv_b, 0, batch=bp, seq=sp, nkv=WIN_KV, groups=grp,
                       dq=HEAD_DIM, dv=HEAD_DIM, scale=win_scale, hb=WIN_KV, tq=256, sink=sink_b,
                       name="win_attn_ctx")
    b_lat = _attention(wq_r, n_ctx, wk_r, wv_b, n_ctx, batch=dec_b, seq=dec_s, nkv=WIN_KV, groups=grp,
                       dq=HEAD_DIM, dv=HEAD_DIM, scale=win_scale, hb=WIN_KV, tq=256,
                       k_ctx=wk_c.reshape(dec_b * past, WIN_KV * HEAD_DIM).astype(BF16),
                       v_ctx=wv_c.reshape(dec_b * past, WIN_KV * HEAD_DIM).astype(BF16),
                       kc_off=0, t_ctx=past, sink=sink_b, window=WINDOW, name="win_attn_lat")

    out = jnp.concatenate([jnp.concatenate([a_ctx, a_lat], axis=0),
                           jnp.concatenate([b_ctx, b_lat], axis=0)], axis=1)
    state = (c_kv[:n_ctx].reshape(bp, 1, sp, MLA_KV_RANK),
             k_rope_p[:n_ctx, :MLA_ROPE].reshape(bp, 1, sp, MLA_ROPE),
             wk[:n_ctx].reshape(bp, 1, sp, WIN_KV, HEAD_DIM),
             wv.reshape(bp, 1, sp, WIN_KV, HEAD_DIM))
    return out, state


def _odd_mixer(h, dims, cache, p, lam_init, layer, tables):
    (bp, sp, dec_b, dec_s, past) = dims
    n_ctx = bp * sp
    (w_in, g_dq, g_dk, lq1, lk1, lq2, lk2, g_sub, g_aq, g_ak) = p
    dk_c, dv_c, ak_c, av_c = cache
    (cos_h, sin_h), (cos_d, sin_d) = tables["head"], tables["diff"]
    o1 = DIFF_HEADS * HEAD_DIM
    o2 = o1 + DIFF_KV * HEAD_DIM
    o3 = o2 + DIFF_KV * HEAD_DIM
    o4 = o3 + AX_HEADS * HEAD_DIM
    o5 = o4 + AX_KV * HEAD_DIM
    proj = _matmul(h, w_in, layer=layer, out_dtype=F32, name="odd_in_proj")
    plan = ((0, o1, "heads", DIFF_D), (o1, o2 - o1, "heads+state", DIFF_D), (o2, o3 - o2, "cast", 0),
            (o3, o4 - o3, "heads", HEAD_DIM), (o4, o5 - o4, "heads+state", HEAD_DIM),
            (o5, AX_KV * HEAD_DIM, "cast", 0))
    dq_r, dk_r, dk, dv_b, aq_r, ak_r, ak, av_b = _prep(
        proj, plan, ((g_dq, cos_d, sin_d), (g_dk, cos_d, sin_d), (), (g_aq, cos_h, sin_h),
                     (g_ak, cos_h, sin_h), ()), "odd_prep")
    dv = proj[:n_ctx, o2:o3]
    av = proj[:n_ctx, o5:]

    lam = (jnp.exp(jnp.sum(lq1.astype(F32) * lk1.astype(F32)))
           - jnp.exp(jnp.sum(lq2.astype(F32) * lk2.astype(F32))) + lam_init)
    lam_b = jnp.broadcast_to(lam.astype(F32), (1, HEAD_DIM))
    g_sub_b = g_sub.astype(F32).reshape(1, HEAD_DIM)
    grp = DIFF_HEADS // DIFF_KV
    diff_kw = dict(nkv=DIFF_KV, groups=grp, dq=HEAD_DIM, dv=HEAD_DIM, scale=DIFF_D ** -0.5, hb=DIFF_KV,
                   lam=lam_b, g_sub=g_sub_b, diff_post_scale=1.0 - lam_init)
    d_ctx = _attention(dq_r, 0, dk_r, dv_b, 0, batch=bp, seq=sp, tq=256, name="diff_attn_ctx", **diff_kw)
    d_lat = _attention(dq_r, n_ctx, dk_r, dv_b, n_ctx, batch=dec_b, seq=dec_s, tq=128,
                       k_ctx=dk_c.reshape(dec_b * past, DIFF_KV * HEAD_DIM).astype(BF16),
                       v_ctx=dv_c.reshape(dec_b * past, DIFF_KV * HEAD_DIM).astype(BF16),
                       kc_off=0, t_ctx=past, name="diff_attn_lat", **diff_kw)
    agrp = AX_HEADS // AX_KV
    ax_kw = dict(nkv=AX_KV, groups=agrp, dq=HEAD_DIM, dv=HEAD_DIM, scale=HEAD_DIM ** -0.5, hb=AX_KV)
    x_ctx = _attention(aq_r, 0, ak_r, av_b, 0, batch=bp, seq=sp, tq=256, name="ax_attn_ctx", **ax_kw)
    x_lat = _attention(aq_r, n_ctx, ak_r, av_b, n_ctx, batch=dec_b, seq=dec_s, tq=256,
                       k_ctx=ak_c.reshape(dec_b * past, AX_KV * HEAD_DIM).astype(BF16),
                       v_ctx=av_c.reshape(dec_b * past, AX_KV * HEAD_DIM).astype(BF16),
                       kc_off=0, t_ctx=past, name="ax_attn_lat", **ax_kw)
    out = jnp.concatenate([jnp.concatenate([d_ctx, d_lat], axis=0),
                           jnp.concatenate([x_ctx, x_lat], axis=0)], axis=1)
    state = (dk[:n_ctx].reshape(bp, 1, sp, DIFF_KV, 2, DIFF_D),
             dv.reshape(bp, 1, sp, DIFF_KV, HEAD_DIM),
             ak[:n_ctx].reshape(bp, 1, sp, AX_KV, HEAD_DIM),
             av.reshape(bp, 1, sp, AX_KV, HEAD_DIM))
    return out, state


def kernel(x_prompt, x_sample, cache_mla_ckv, cache_mla_krope, cache_win_k, cache_win_v, cache_diff_k, cache_diff_v, cache_ax_k, cache_ax_v, c, c_ctx, w_mod, b_mod, g_norm_mix, g_norm_ffn, w_in_even, g_mla_q, w_mla_qb, g_mla_kv, w_mla_kvb, g_mla_qn, g_mla_kn, g_win_qn, g_win_kn, win_sink, w_out_even, w_in_odd, g_diff_qn, g_diff_kn, diff_lq1, diff_lk1, diff_lq2, diff_lk2, g_diff_sub, g_ax_qn, g_ax_kn, w_out_odd, w_router, b_router, w_exp_gate, w_exp_up, w_exp_down, w_sh_gate, w_sh_up, w_sh_down):
    bp, sp, d = x_prompt.shape
    dec_b, dec_s, _ = x_sample.shape
    depth = w_mod.shape[0]
    n_ctx = bp * sp
    n = n_ctx + dec_b * dec_s
    dims = (bp, sp, dec_b, dec_s, cache_mla_ckv.shape[2])
    group_fn = lambda tm: _group_index_fn(tm, n_ctx, dec_s)
    n_groups = 1 + dec_b

    y = jnp.concatenate([x_prompt.reshape(n_ctx, d), x_sample.reshape(dec_b * dec_s, d)], axis=0)

    cond = jnp.concatenate([c_ctx[None], c, jnp.zeros((8 - n_groups % 8, d), F32)], axis=0)
    cond = jax.nn.silu(cond)

    past = cache_mla_ckv.shape[2]
    tables = {"head": _rope_tables(HEAD_DIM, n_ctx, dec_b, dec_s, 0),
              "diff": _rope_tables(DIFF_D, n_ctx, dec_b, dec_s, 0),
              "mla": _rope_tables(MLA_ROPE, n_ctx, dec_b, dec_s, dec_b * past)}

    states_even, states_odd = [], []
    for l in range(depth):
        i = l // 2
        mod = _matmul(cond, w_mod, layer=l, out_dtype=F32, bias=b_mod.reshape(depth, 1, 6 * d),
                      name="modulation")
        mod = mod[:n_groups].reshape(n_groups, 6, 1, d)
        sh1, sc1, g1, sh2, sc2, g2 = (mod[:, j] for j in range(6))
        h = _ada_norm(y, g_norm_mix[l][None], sh1, sc1, group_fn, name="ada_norm_mix")
        if l % 2 == 0:
            pe = (w_in_even[i], g_mla_q[i], w_mla_qb[i], g_mla_kv[i], w_mla_kvb[i], g_mla_qn[i],
                  g_mla_kn[i], g_win_qn[i], g_win_kn[i], win_sink[i])
            cache = (cache_mla_ckv[:, i], cache_mla_krope[:, i], cache_win_k[:, i], cache_win_v[:, i])
            out, state = _even_mixer(h, dims, cache, pe, tables)
            states_even.append(state)
            w_out = w_out_even
        else:
            po = (w_in_odd, g_diff_qn[i], g_diff_kn[i], diff_lq1[i], diff_lk1[i], diff_lq2[i],
                  diff_lk2[i], g_diff_sub[i], g_ax_qn[i], g_ax_kn[i])
            cache = (cache_diff_k[:, i], cache_diff_v[:, i], cache_ax_k[:, i], cache_ax_v[:, i])
            lam_init = 0.8 - 0.6 * math.exp(-0.3 * l)
            out, state = _odd_mixer(h, dims, cache, po, lam_init, i, tables)
            states_odd.append(state)
            w_out = w_out_odd
        y = _matmul(out, w_out, layer=i, out_dtype=F32, resid=y, gates=g1, group_fn=group_fn,
                    name="mixer_out_proj")
        y = _moe(y, g_norm_ffn[l][None], sh2, sc2, g2, group_fn, l, w_router, b_router,
                 w_exp_gate, w_exp_up, w_exp_down, w_sh_gate, w_sh_up, w_sh_down)

    yp = y[:n_ctx].reshape(bp, sp, d)
    ys = y[n_ctx:].reshape(dec_b, dec_s, d)
    even = tuple(jnp.concatenate([s[j] for s in states_even], axis=1) for j in range(4))
    odd = tuple(jnp.concatenate([s[j] for s in states_odd], axis=1) for j in range(4))
    return (yp, ys) + even + odd
```

```python
import functools
import math

import jax
import jax.numpy as jnp
from jax import lax
from jax.experimental import pallas as pl
from jax.experimental.pallas import tpu as pltpu

F32 = jnp.float32
BF16 = jnp.bfloat16

GRID_W = 64
ROPE_THETA = 10000.0
EPS = 1e-6
NEG_INF = -1e30
HEAD_DIM = 128
MLA_HEADS = 16
MLA_Q_RANK = 768
MLA_KV_RANK = 512
MLA_NOPE = 128
MLA_ROPE = 64
MLA_V = 128
MLA_QK = MLA_NOPE + MLA_ROPE
MLA_QK_PAD = 256
WIN_HEADS = 16
WIN_KV = 4
WINDOW = 128
DIFF_HEADS = 16
DIFF_KV = 4
DIFF_D = HEAD_DIM // 2
AX_HEADS = 16
AX_KV = 4
N_EXPERTS = 64
N_EXPERT_GROUPS = 8
TOPK_GROUPS = 4
TOP_K = 8
ROUTED_SCALE = 2.5

V7X_LANES = 128
V7X_VMEM_LIMIT_BYTES = 56 * 1024 * 1024
MM_WEIGHT_TILE_BYTES = 32 * 1024 * 1024
MOE_ROW_TILE = 256
MOE_CHUNKS = 4


def _cparams(sem):
    return pltpu.CompilerParams(dimension_semantics=sem, vmem_limit_bytes=V7X_VMEM_LIMIT_BYTES)


def _largest_tile(n, cap, step):
    if n <= cap:
        return n
    t = (cap // step) * step
    while t >= step:
        if n % t == 0:
            return t
        t -= step
    raise ValueError(f"no tile for {n} under {cap}")


def _group_index_fn(tm, n_ctx_rows, dec_rows):
    assert n_ctx_rows % tm == 0 and dec_rows % tm == 0
    ctx_tiles = n_ctx_rows // tm
    per_dec = dec_rows // tm

    def fn(i):
        return jnp.where(i < ctx_tiles, 0, 1 + (jnp.maximum(i - ctx_tiles, 0)) // per_dec)

    return fn


def _mm_body(*refs, has_bias, has_resid):
    x_ref, w_ref = refs[0], refs[1]
    idx = 2
    if has_bias:
        bias_ref = refs[idx]
        idx += 1
    if has_resid:
        resid_ref, gate_ref = refs[idx], refs[idx + 1]
        idx += 2
    o_ref, wb_ref = refs[idx], refs[idx + 1]

    @pl.when(pl.program_id(1) == 0)
    def _():
        wb_ref[...] = w_ref[...].astype(BF16)

    acc = jnp.dot(x_ref[...].astype(BF16), wb_ref[...], preferred_element_type=F32)
    if has_bias:
        acc = acc + bias_ref[...]
    if has_resid:
        acc = resid_ref[...] + gate_ref[...] * acc
    o_ref[...] = acc.astype(o_ref.dtype)


def _matmul(x, w, *, out_dtype, layer=None, bias=None, resid=None, gates=None, group_fn=None, name):
    m, k = x.shape
    n = w.shape[-1]
    tm = _largest_tile(m, 512, 8)
    tn_cap = max(V7X_LANES, min(2048, MM_WEIGHT_TILE_BYTES // (k * 10)))
    tn = _largest_tile(n, tn_cap, V7X_LANES)
    grid = (n // tn, m // tm)
    if layer is None:
        w_spec = pl.BlockSpec((k, tn), lambda j, i: (0, j))
    else:
        w_spec = pl.BlockSpec((None, k, tn), lambda j, i: (layer, 0, j))
    in_specs = [pl.BlockSpec((tm, k), lambda j, i: (i, 0)), w_spec]
    args = [x, w]
    if bias is not None:
        in_specs.append(pl.BlockSpec((None, 1, tn), lambda j, i: (layer, 0, j)))
        args.append(bias)
    if resid is not None:
        gfn = group_fn(tm)
        in_specs.append(pl.BlockSpec((tm, tn), lambda j, i: (i, j)))
        in_specs.append(pl.BlockSpec((None, 1, tn), lambda j, i: (gfn(i), 0, j)))
        args += [resid, gates]
    return pl.pallas_call(
        functools.partial(_mm_body, has_bias=bias is not None, has_resid=resid is not None),
        out_shape=jax.ShapeDtypeStruct((m, n), out_dtype),
        grid=grid,
        in_specs=in_specs,
        out_specs=pl.BlockSpec((tm, tn), lambda j, i: (i, j)),
        scratch_shapes=[pltpu.VMEM((k, tn), BF16)],
        compiler_params=_cparams(("arbitrary", "arbitrary")),
        name=name,
    )(*args)


def _first_max(cur, sub, limit):
    m = jnp.max(cur, axis=0, keepdims=True)
    first = jnp.min(jnp.where(cur == m, sub, limit), axis=0, keepdims=True)
    return m, first, sub == first


def _route_columns(logits, bias, tri, carry):
    e, t = logits.shape
    per = e // N_EXPERT_GROUPS
    scores = jax.nn.sigmoid(logits)
    biased = scores + bias
    neg = -jnp.inf
    gsub = lax.broadcasted_iota(jnp.int32, (per, t), 0)
    grp = []
    for g in range(N_EXPERT_GROUPS):
        xg = biased[g * per:(g + 1) * per]
        m1, _, hit = _first_max(xg, gsub, per)
        m2 = jnp.max(jnp.where(hit, neg, xg), axis=0, keepdims=True)
        grp.append(m1 + m2)
    grp = jnp.concatenate(grp, axis=0)
    nsub = lax.broadcasted_iota(jnp.int32, (N_EXPERT_GROUPS, t), 0)
    gsel = jnp.zeros((N_EXPERT_GROUPS, t), F32)
    for _ in range(TOPK_GROUPS):
        _, _, hit = _first_max(grp, nsub, N_EXPERT_GROUPS)
        gsel = jnp.where(hit, 1.0, gsel)
        grp = jnp.where(hit, neg, grp)
    emask = jnp.concatenate([jnp.broadcast_to(gsel[g:g + 1], (per, t)) for g in range(N_EXPERT_GROUPS)], axis=0)
    cur = jnp.where(emask > 0.5, biased, neg)
    esub = lax.broadcasted_iota(jnp.int32, (e, t), 0)
    sel = jnp.zeros((e, t), F32)
    ids, ws, hits = [], [], []
    for _ in range(TOP_K):
        _, first, hit = _first_max(cur, esub, e)
        ids.append(first)
        ws.append(jnp.sum(jnp.where(hit, scores, 0.0), axis=0, keepdims=True))
        hits.append(hit)
        sel = jnp.where(hit, 1.0, sel)
        cur = jnp.where(hit, neg, cur)
    w = jnp.concatenate(ws, axis=0)
    gates = w / jnp.sum(w, axis=0, keepdims=True) * ROUTED_SCALE
    rank_all = jnp.dot(sel.astype(BF16), tri, preferred_element_type=F32) + carry
    ranks = [jnp.sum(jnp.where(hit, rank_all, 0.0), axis=0, keepdims=True) for hit in hits]
    counts = jnp.sum(sel, axis=1, keepdims=True)
    return (jnp.concatenate(ids, axis=0), gates, jnp.concatenate(ranks, axis=0).astype(jnp.int32), counts)


def _adanorm_body(x_ref, g_ref, shift_ref, scale_ref, *rest, has_router):
    x = x_ref[...]
    y = x * lax.rsqrt(jnp.mean(x * x, axis=-1, keepdims=True) + EPS) * g_ref[...]
    t = y * (1.0 + scale_ref[...]) + shift_ref[...]
    if has_router:
        wrt_ref, br_ref, tri_ref, o_ref, ids_ref, gate_ref, rank_ref, cnt_ref, carry_ref = rest

        @pl.when(pl.program_id(0) == 0)
        def _():
            carry_ref[...] = jnp.zeros_like(carry_ref)

        logits = lax.dot_general(wrt_ref[...], t, (((1,), (1,)), ((), ())),
                                 precision=lax.Precision.HIGHEST, preferred_element_type=F32)
        ids, gates, ranks, counts = _route_columns(logits, br_ref[...], tri_ref[...], carry_ref[:, :1])
        ids_ref[...] = ids
        gate_ref[...] = gates
        rank_ref[...] = ranks
        carry_ref[...] = carry_ref[...] + counts
        cnt_ref[...] = carry_ref[...]
    else:
        (o_ref,) = rest
    o_ref[...] = t.astype(o_ref.dtype)


def _ada_norm(x, g, shift, scale, group_fn, *, w_router=None, b_router=None, name):
    n, d = x.shape
    tm = 256
    gfn = group_fn(tm)
    in_specs = [pl.BlockSpec((tm, d), lambda i: (i, 0)),
                pl.BlockSpec((1, d), lambda i: (0, 0)),
                pl.BlockSpec((None, 1, d), lambda i: (gfn(i), 0, 0)),
                pl.BlockSpec((None, 1, d), lambda i: (gfn(i), 0, 0))]
    args = [x, g, shift, scale]
    out_shape = jax.ShapeDtypeStruct((n, d), BF16)
    out_specs = pl.BlockSpec((tm, d), lambda i: (i, 0))
    scratch = []
    if w_router is not None:
        e = w_router.shape[1]
        tri = (lax.broadcasted_iota(jnp.int32, (tm, tm), 0) < lax.broadcasted_iota(jnp.int32, (tm, tm), 1))
        in_specs += [pl.BlockSpec((e, d), lambda i: (0, 0)),
                     pl.BlockSpec((e, tm), lambda i: (0, 0)),
                     pl.BlockSpec((tm, tm), lambda i: (0, 0))]
        args += [w_router.T, jnp.broadcast_to(b_router.astype(F32)[:, None], (e, tm)), tri.astype(BF16)]
        col = lambda rows, dt: (jax.ShapeDtypeStruct((rows, n), dt), pl.BlockSpec((rows, tm), lambda i: (0, i)))
        extra = [col(TOP_K, jnp.int32), col(TOP_K, F32), col(TOP_K, jnp.int32),
                 (jax.ShapeDtypeStruct((e, V7X_LANES), F32), pl.BlockSpec((e, V7X_LANES), lambda i: (0, 0)))]
        out_shape = (out_shape,) + tuple(s for s, _ in extra)
        out_specs = (out_specs,) + tuple(b for _, b in extra)
        scratch = [pltpu.VMEM((e, V7X_LANES), F32)]
    return pl.pallas_call(
        functools.partial(_adanorm_body, has_router=w_router is not None),
        out_shape=out_shape,
        grid=(n // tm,),
        in_specs=in_specs,
        out_specs=out_specs,
        scratch_shapes=scratch,
        compiler_params=_cparams(("arbitrary",)),
        name=name,
    )(*args)


def _attn_body(*refs, hb, groups, dq, dv, tq, tk, scale, t_new, t_ctx, window, has_sink,
               diff_post_scale, has_out_buf):
    it = iter(refs)
    q_ref, kn_ref, vn_ref = next(it), next(it), next(it)
    kc_ref = vc_ref = sink_ref = lam_ref = gsub_ref = None
    if t_ctx:
        kc_ref, vc_ref = next(it), next(it)
    if has_sink:
        sink_ref = next(it)
    diff = diff_post_scale is not None
    if diff:
        lam_ref, gsub_ref = next(it), next(it)
    if has_out_buf:
        next(it)
    o_ref = next(it)

    h = pl.program_id(1)
    i = pl.program_id(2)
    nstack = 2 * groups if diff else groups
    rows = nstack * tq

    def stack_rows(parts):
        return parts[0] if len(parts) == 1 else jnp.concatenate(parts, axis=0)

    if window is not None:
        wk = min(t_new, tq + 2 * window)
        if wk == t_new:
            wstart = 0
        else:
            wstart = pl.multiple_of(jnp.clip(i * tq - window, 0, t_new - wk), V7X_LANES)
        qi = i * tq + lax.broadcasted_iota(jnp.int32, (tq, wk), 0)
        kj = wstart + lax.broadcasted_iota(jnp.int32, (tq, wk), 1)
        wmask = stack_rows([jnp.abs(qi - kj) <= window] * nstack)

    for j in range(hb):
        qj = q_ref[:, j * groups * dq:(j + 1) * groups * dq]
        parts = [qj[:, g * dq:(g + 1) * dq] for g in range(groups)]
        if diff:
            lo = lax.broadcasted_iota(jnp.int32, (tq, dq), 1) < dq // 2
            zero = jnp.zeros((tq, dq), qj.dtype)
            parts = [jnp.where(lo, p, zero) for p in parts] + [jnp.where(lo, zero, p) for p in parts]
        qs = stack_rows(parts)

        if has_sink:
            sinks = []
            for g in range(groups):
                hh = (h * hb + j) * groups + g
                sinks.append(jnp.broadcast_to(sink_ref[pl.ds(hh, 1), :][:, :1], (tq, 1)))
            m = stack_rows(sinks)
            l = jnp.ones((rows, 1), F32)
        else:
            m = jnp.full((rows, 1), NEG_INF, F32)
            l = jnp.zeros((rows, 1), F32)
        acc = jnp.zeros((rows, dv), F32)

        def step(carry, kc, vc, mask):
            m, l, acc = carry
            s = lax.dot_general(qs, kc, (((1,), (1,)), ((), ())), preferred_element_type=F32) * scale
            if mask is not None:
                s = jnp.where(mask, s, NEG_INF)
            m_new = jnp.maximum(m, jnp.max(s, axis=-1, keepdims=True))
            alpha = jnp.exp(m - m_new)
            p = jnp.exp(s - m_new)
            l = alpha * l + jnp.sum(p, axis=-1, keepdims=True)
            acc = alpha * acc + jnp.dot(p.astype(BF16), vc, preferred_element_type=F32)
            return m_new, l, acc

        carry = (m, l, acc)
        kcols = slice(j * dq, (j + 1) * dq)
        vcols = slice(j * dv, (j + 1) * dv)
        if window is not None:
            carry = step(carry, kn_ref[pl.ds(wstart, wk), kcols], vn_ref[pl.ds(wstart, wk), vcols], wmask)
        else:
            for c in range(t_new // tk):
                carry = step(carry, kn_ref[c * tk:(c + 1) * tk, kcols], vn_ref[c * tk:(c + 1) * tk, vcols], None)
        if t_ctx:
            tkc = min(tk, t_ctx)
            for c in range(t_ctx // tkc):
                carry = step(carry, kc_ref[c * tkc:(c + 1) * tkc, kcols], vc_ref[c * tkc:(c + 1) * tkc, vcols], None)
        m, l, acc = carry
        o = acc / l
        if diff:
            half = groups * tq
            d = o[:half] - lam_ref[...] * o[half:]
            o = (d * lax.rsqrt(jnp.mean(d * d, axis=-1, keepdims=True) + EPS) * gsub_ref[...]) * diff_post_scale
        outs = [o[g * tq:(g + 1) * tq] for g in range(groups)]
        oj = outs[0] if groups == 1 else jnp.concatenate(outs, axis=1)
        o_ref[:, j * groups * dv:(j + 1) * groups * dv] = oj.astype(o_ref.dtype)


def _attention(q, q_off, k_new, v_new, kn_off, *, batch, seq, nkv, groups, dq, dv, scale, hb, tq,
               k_ctx=None, v_ctx=None, kc_off=0, t_ctx=0, sink=None, window=None,
               lam=None, g_sub=None, diff_post_scale=None, out=None, out_shape=None, out_off=(0, 0), name):
    t_new = seq
    tq = min(tq, seq)
    tk = min(512, t_new)
    assert seq % tq == 0 and t_new % tk == 0 and nkv % hb == 0
    assert q_off % tq == 0 and kn_off % t_new == 0
    qb, nb = q_off // tq, kn_off // t_new
    spt = seq // tq
    in_specs = [pl.BlockSpec((tq, hb * groups * dq), lambda b, h, i: (qb + b * spt + i, h)),
                pl.BlockSpec((t_new, hb * dq), lambda b, h, i: (nb + b, h)),
                pl.BlockSpec((t_new, hb * dv), lambda b, h, i: (nb + b, h))]
    args = [q, k_new, v_new]
    if t_ctx:
        assert kc_off % t_ctx == 0
        cb = kc_off // t_ctx
        in_specs += [pl.BlockSpec((t_ctx, hb * dq), lambda b, h, i: (cb + b, h)),
                     pl.BlockSpec((t_ctx, hb * dv), lambda b, h, i: (cb + b, h))]
        args += [k_ctx, v_ctx]
    if sink is not None:
        in_specs.append(pl.BlockSpec(sink.shape, lambda b, h, i: (0, 0)))
        args.append(sink)
    if diff_post_scale is not None:
        in_specs += [pl.BlockSpec((1, dv), lambda b, h, i: (0, 0))] * 2
        args += [lam, g_sub]
    aliases = {}
    if out is not None:
        out_shape = out.shape
        in_specs.append(pl.BlockSpec(memory_space=pl.ANY))
        args.append(out)
        aliases = {len(args) - 1: 0}
    wblk = hb * groups * dv
    assert out_off[0] % tq == 0 and out_off[1] % wblk == 0
    ob, oc = out_off[0] // tq, out_off[1] // wblk
    body = functools.partial(
        _attn_body, hb=hb, groups=groups, dq=dq, dv=dv, tq=tq, tk=tk, scale=scale, t_new=t_new,
        t_ctx=t_ctx, window=window, has_sink=sink is not None, diff_post_scale=diff_post_scale,
        has_out_buf=out is not None)
    return pl.pallas_call(
        body,
        out_shape=jax.ShapeDtypeStruct(out_shape, BF16),
        grid=(batch, nkv // hb, spt),
        in_specs=in_specs,
        out_specs=pl.BlockSpec((tq, wblk), lambda b, h, i: (ob + b * spt + i, oc + h)),
        input_output_aliases=aliases,
        compiler_params=_cparams(("arbitrary", "arbitrary", "arbitrary")),
        name=name,
    )(*args)


def _experts_body(te_ref, nv_ref, x_ref, wg_ref, wu_ref, wd_ref, *rest, tile0):
    o_ref, wgb, wub, wdb = rest[-4:]
    t = pl.program_id(0)
    g = tile0 + t
    valid = g < nv_ref[0]
    prev = te_ref[jnp.maximum(g - 1, 0)]
    first = jnp.logical_or(t == 0, te_ref[g] != prev)

    @pl.when(jnp.logical_and(first, valid))
    def _():
        wgb[...] = wg_ref[...].astype(BF16)
        wub[...] = wu_ref[...].astype(BF16)
        wdb[...] = wd_ref[...].astype(BF16)

    @pl.when(valid)
    def _():
        x = x_ref[...]
        a = jnp.dot(x, wgb[...], preferred_element_type=F32)
        u = jnp.dot(x, wub[...], preferred_element_type=F32)
        hcur = a * jax.nn.sigmoid(a) * u
        o_ref[...] = jnp.dot(hcur.astype(BF16), wdb[...], preferred_element_type=F32).astype(o_ref.dtype)

    @pl.when(jnp.logical_not(valid))
    def _():
        o_ref[...] = jnp.zeros_like(o_ref)


def _routed_experts(xs, tile_expert, n_valid, w_gate, w_up, w_down, layer, tile0, n_tiles, rows_buf):
    r, d = xs.shape
    ff = w_gate.shape[-1]
    tm = MOE_ROW_TILE
    chunk_tiles = r // tm

    def x_blk(t, te, nv):
        last = jnp.clip(nv[0] - tile0 - 1, 0, chunk_tiles - 1)
        return (jnp.minimum(t, last), 0)

    def o_blk(t, te, nv):
        return (jnp.where(tile0 + t < nv[0], tile0 + t, n_tiles), 0)

    w_blk = lambda t, te, nv: (layer, te[tile0 + t], 0, 0)
    in_specs = [pl.BlockSpec((tm, d), x_blk),
                pl.BlockSpec((None, None, d, ff), w_blk),
                pl.BlockSpec((None, None, d, ff), w_blk),
                pl.BlockSpec((None, None, ff, d), w_blk)]
    args = [tile_expert, n_valid, xs, w_gate, w_up, w_down]
    aliases = {}
    if rows_buf is not None:
        in_specs.append(pl.BlockSpec(memory_space=pl.ANY))
        args.append(rows_buf)
        aliases = {len(args) - 1: 0}
    grid_spec = pltpu.PrefetchScalarGridSpec(
        num_scalar_prefetch=2,
        grid=(chunk_tiles,),
        in_specs=in_specs,
        out_specs=pl.BlockSpec((tm, d), o_blk),
        scratch_shapes=[pltpu.VMEM((d, ff), BF16), pltpu.VMEM((d, ff), BF16), pltpu.VMEM((ff, d), BF16)],
    )
    return pl.pallas_call(
        functools.partial(_experts_body, tile0=tile0),
        out_shape=jax.ShapeDtypeStruct(((n_tiles + 1) * tm, d), BF16),
        grid_spec=grid_spec,
        input_output_aliases=aliases,
        compiler_params=_cparams(("arbitrary",)),
        name="routed_experts",
    )(*args)


def _shared_body(t_ref, wg_ref, wu_ref, wd_ref, resid_ref, gate_ref, o_ref, h_ref):
    @pl.when(pl.program_id(1) == 0)
    def _():
        x = t_ref[...]
        a = jnp.dot(x, wg_ref[...], preferred_element_type=F32)
        u = jnp.dot(x, wu_ref[...], preferred_element_type=F32)
        h_ref[...] = (a * jax.nn.sigmoid(a) * u).astype(BF16)

    shared = jnp.dot(h_ref[...], wd_ref[...], preferred_element_type=F32)
    o_ref[...] = resid_ref[...] + gate_ref[...] * shared


def _shared_expert(t, ws_gate, ws_up, ws_down, resid, gates, group_fn):
    n, d = t.shape
    ff = ws_gate.shape[1]
    tm = _largest_tile(n, 512, 8)
    tn = _largest_tile(d, 1024, V7X_LANES)
    gfn = group_fn(tm)
    return pl.pallas_call(
        _shared_body,
        out_shape=jax.ShapeDtypeStruct((n, d), F32),
        grid=(n // tm, d // tn),
        in_specs=[pl.BlockSpec((tm, d), lambda i, j: (i, 0)),
                  pl.BlockSpec((d, ff), lambda i, j: (0, 0)),
                  pl.BlockSpec((d, ff), lambda i, j: (0, 0)),
                  pl.BlockSpec((ff, tn), lambda i, j: (0, j)),
                  pl.BlockSpec((tm, tn), lambda i, j: (i, j)),
                  pl.BlockSpec((None, 1, tn), lambda i, j: (gfn(i), 0, j))],
        out_specs=pl.BlockSpec((tm, tn), lambda i, j: (i, j)),
        scratch_shapes=[pltpu.VMEM((tm, ff), BF16)],
        compiler_params=_cparams(("arbitrary", "arbitrary")),
        name="shared_expert",
    )(t, ws_gate, ws_up, ws_down, resid, gates)


def _combine_body(rows_ref, w_ref, base_ref, gate_ref, o_ref):
    w = w_ref[...]
    acc = w[:, 0:1] * rows_ref[0].astype(F32)
    for c in range(1, rows_ref.shape[0]):
        acc = acc + w[:, c:c + 1] * rows_ref[c].astype(F32)
    o_ref[...] = base_ref[...] + gate_ref[...] * acc


def _combine(picked, weights, base, gates, group_fn):
    n, d = base.shape
    k = picked.shape[0]
    tm = 128
    gfn = group_fn(tm)
    return pl.pallas_call(
        _combine_body,
        out_shape=jax.ShapeDtypeStruct((n, d), F32),
        grid=(n // tm,),
        in_specs=[pl.BlockSpec((k, tm, d), lambda i: (0, i, 0)),
                  pl.BlockSpec((tm, k), lambda i: (i, 0)),
                  pl.BlockSpec((tm, d), lambda i: (i, 0)),
                  pl.BlockSpec((None, 1, d), lambda i: (gfn(i), 0, 0))],
        out_specs=pl.BlockSpec((tm, d), lambda i: (i, 0)),
        compiler_params=_cparams(("arbitrary",)),
        name="moe_combine",
    )(picked, weights, base, gates)


def _plan_body(rs_ref, ids_ref, rank_ref, pos_ref):
    ids = ids_ref[...]
    base = jnp.zeros_like(ids)
    for e in range(N_EXPERTS):
        base = jnp.where(ids == e, rs_ref[e], base)
    pos_ref[...] = base + rank_ref[...]


def _dispatch_plan(ids, ranks, counts, tm):
    k, n = ids.shape
    e = counts.shape[0]
    n_rows = n * k + e * tm
    n_tiles = n_rows // tm
    tiles_e = (counts + tm - 1) // tm
    tile_end = jnp.cumsum(tiles_e)
    row_start = ((tile_end - tiles_e) * tm).astype(jnp.int32)
    tn = _largest_tile(n, 2048, V7X_LANES)
    pos = pl.pallas_call(
        _plan_body,
        out_shape=jax.ShapeDtypeStruct((k, n), jnp.int32),
        grid_spec=pltpu.PrefetchScalarGridSpec(
            num_scalar_prefetch=1,
            grid=(n // tn,),
            in_specs=[pl.BlockSpec((k, tn), lambda i, rs: (0, i))] * 2,
            out_specs=pl.BlockSpec((k, tn), lambda i, rs: (0, i))),
        compiler_params=_cparams(("arbitrary",)),
        name="dispatch_rows",
    )(row_start, ids, ranks)
    flat = pos.reshape(-1)
    tok = jnp.tile(jnp.arange(n, dtype=jnp.int32), k)
    row_token = (jnp.arange(n_rows, dtype=jnp.int32) % n).at[flat].set(tok)
    tile_ids = jnp.arange(n_tiles, dtype=jnp.int32)
    tile_expert = jnp.sum((tile_end[None, :] <= tile_ids[:, None]).astype(jnp.int32), axis=1)
    tile_expert = jnp.minimum(tile_expert, e - 1)
    return flat, row_token, tile_expert, tile_end[-1:].astype(jnp.int32)


def _moe(y, g_ffn, shift, scale, gates, group_fn, layer, w_router, b_router, w_gate, w_up, w_down,
         ws_gate, ws_up, ws_down):
    n, d = y.shape
    t, ids, gate_w, ranks, counts = _ada_norm(y, g_ffn, shift, scale, group_fn, w_router=w_router[layer],
                                              b_router=b_router[layer], name="ada_norm_route")
    flat, row_token, tile_expert, n_valid = _dispatch_plan(
        ids, ranks, counts[:, 0].astype(jnp.int32), MOE_ROW_TILE)
    base = _shared_expert(t, ws_gate[layer].astype(BF16), ws_up[layer].astype(BF16),
                          ws_down[layer].astype(BF16), y, gates, group_fn)
    n_tiles = tile_expert.shape[0]
    chunk_tiles = n_tiles // MOE_CHUNKS
    chunk_rows = chunk_tiles * MOE_ROW_TILE
    rows = None
    for c in range(MOE_CHUNKS):
        xs = t.at[row_token[c * chunk_rows:(c + 1) * chunk_rows]].get(mode="promise_in_bounds")
        rows = _routed_experts(xs, tile_expert, n_valid, w_gate, w_up, w_down, layer,
                               c * chunk_tiles, n_tiles, rows)
    picked = rows.at[flat].get(mode="promise_in_bounds").reshape(TOP_K, n, d)
    return _combine(picked, gate_w.T, base, gates, group_fn)


def _rope_2d(n_tok, rot_dim):
    rows = n_tok // GRID_W
    row = jnp.broadcast_to(jnp.arange(rows, dtype=F32)[:, None], (rows, GRID_W)).reshape(-1)
    col = jnp.broadcast_to(jnp.arange(GRID_W, dtype=F32)[None, :], (rows, GRID_W)).reshape(-1)
    n_freq = rot_dim // 4
    inv = ROPE_THETA ** (-jnp.arange(n_freq, dtype=F32) / n_freq)
    ang = jnp.concatenate([row[:, None] * inv, col[:, None] * inv], axis=-1)
    return jnp.cos(ang), jnp.sin(ang)


def _rope_tables(rot_dim, n_ctx, dec_b, dec_s, n_tail):
    c, s = _rope_2d(dec_s, rot_dim)
    reps = V7X_LANES // rot_dim
    cos = jnp.concatenate([c, c] * reps, axis=1)
    sin = jnp.concatenate([-s, s] * reps, axis=1)
    if n_tail:
        fill = V7X_LANES - rot_dim
        cos = jnp.concatenate([c, c, jnp.ones((dec_s, fill), F32)], axis=1)
        sin = jnp.concatenate([-s, s, jnp.zeros((dec_s, fill), F32)], axis=1)
    ones = lambda r: jnp.ones((r, V7X_LANES), F32)
    zeros = lambda r: jnp.zeros((r, V7X_LANES), F32)
    cos = jnp.concatenate([ones(n_ctx)] + [cos] * dec_b + [ones(n_tail)], axis=0)
    sin = jnp.concatenate([zeros(n_ctx)] + [sin] * dec_b + [zeros(n_tail)], axis=0)
    return cos, sin


def _rotate_half(y, seg):
    half = seg // 2
    if seg == V7X_LANES:
        return pltpu.roll(y, half, axis=1)
    lane = lax.broadcasted_iota(jnp.int32, y.shape, 1)
    return jnp.where(lane % seg < half, pltpu.roll(y, V7X_LANES - half, axis=1), pltpu.roll(y, half, axis=1))


def _rotate_half_tail(t):
    half = MLA_ROPE // 2
    lane = lax.broadcasted_iota(jnp.int32, t.shape, 1)
    return jnp.where(lane < half, pltpu.roll(t, V7X_LANES - half, axis=1), pltpu.roll(t, half, axis=1))


def _seg_rms(x, seg):
    sq = x * x
    if seg == V7X_LANES:
        return x * lax.rsqrt(jnp.mean(sq, axis=-1, keepdims=True) + EPS)
    assert seg * 2 == V7X_LANES
    lo = lax.broadcasted_iota(jnp.int32, x.shape, 1) < seg
    s_lo = jnp.sum(jnp.where(lo, sq, 0.0), axis=-1, keepdims=True)
    s_hi = jnp.sum(jnp.where(lo, 0.0, sq), axis=-1, keepdims=True)
    return x * lax.rsqrt(jnp.where(lo, s_lo, s_hi) / seg + EPS)


def _prep_body(proj_ref, *refs, plan):
    it = iter(refs)
    inputs = []
    for (_, _, kind, _) in plan:
        if kind in ("norm", "norm_f32"):
            inputs.append((next(it),))
        elif kind.startswith("heads"):
            inputs.append((next(it), next(it), next(it)))
        else:
            inputs.append(())
    outs = list(it)
    oi = 0
    for (col, width, kind, seg), ins in zip(plan, inputs):
        if kind in ("norm", "norm_f32"):
            (g_ref,) = ins
            x = proj_ref[:, col:col + width]
            y = x * lax.rsqrt(jnp.mean(x * x, axis=-1, keepdims=True) + EPS) * g_ref[...]
            outs[oi][...] = y.astype(outs[oi].dtype)
            oi += 1
        elif kind.startswith("heads"):
            g_ref, cos_ref, sin_ref = ins
            with_state = kind.endswith("+state")
            cos, sin, g = cos_ref[...], sin_ref[...], g_ref[...]
            for c0 in range(0, width, V7X_LANES):
                y = _seg_rms(proj_ref[:, col + c0:col + c0 + V7X_LANES], seg) * g
                if with_state:
                    outs[oi + 1][:, c0:c0 + V7X_LANES] = y
                y = y * cos + _rotate_half(y, seg) * sin
                outs[oi][:, c0:c0 + V7X_LANES] = y.astype(BF16)
            oi += 2 if with_state else 1
        else:
            outs[oi][...] = proj_ref[:, col:col + width].astype(BF16)
            oi += 1


def _prep(proj, plan, params, name):
    n, width_all = proj.shape
    tm = 256
    in_specs = [pl.BlockSpec((tm, width_all), lambda i: (i, 0))]
    args = [proj]
    out_shape, out_specs = [], []
    for (col, width, kind, seg), ps in zip(plan, params):
        row_spec = pl.BlockSpec((tm, width), lambda i: (i, 0))
        if kind in ("norm", "norm_f32"):
            in_specs.append(pl.BlockSpec((1, width), lambda i: (0, 0)))
            args.append(ps[0].astype(F32).reshape(1, width))
            out_shape.append(jax.ShapeDtypeStruct((n, width), F32 if kind == "norm_f32" else BF16))
            out_specs.append(row_spec)
        elif kind.startswith("heads"):
            g, cos, sin = ps
            in_specs += [pl.BlockSpec((1, V7X_LANES), lambda i: (0, 0)),
                         pl.BlockSpec((tm, V7X_LANES), lambda i: (i, 0)),
                         pl.BlockSpec((tm, V7X_LANES), lambda i: (i, 0))]
            args += [jnp.tile(g.astype(F32), V7X_LANES // seg).reshape(1, V7X_LANES), cos, sin]
            out_shape.append(jax.ShapeDtypeStruct((n, width), BF16))
            out_specs.append(row_spec)
            if kind.endswith("+state"):
                out_shape.append(jax.ShapeDtypeStruct((n, width), F32))
                out_specs.append(row_spec)
        else:
            out_shape.append(jax.ShapeDtypeStruct((n, width), BF16))
            out_specs.append(row_spec)
    return pl.pallas_call(
        functools.partial(_prep_body, plan=plan),
        out_shape=tuple(out_shape),
        grid=(n // tm,),
        in_specs=in_specs,
        out_specs=tuple(out_specs),
        compiler_params=_cparams(("arbitrary",)),
        name=name,
    )(*args)


def _mla_q_body(x_ref, w_ref, g_ref, cos_ref, sin_ref, o_ref):
    acc = jnp.dot(x_ref[...], w_ref[...], preferred_element_type=F32)
    cos, sin = cos_ref[...], sin_ref[...]
    for h in range(MLA_HEADS):
        c0 = h * MLA_QK_PAD
        nope = acc[:, c0:c0 + MLA_NOPE]
        tail = acc[:, c0 + MLA_NOPE:c0 + MLA_QK_PAD]
        ss = jnp.sum(nope * nope, axis=-1, keepdims=True) + jnp.sum(tail * tail, axis=-1, keepdims=True)
        r = lax.rsqrt(ss / MLA_QK + EPS)
        o_ref[:, c0:c0 + MLA_NOPE] = (nope * r * g_ref[:, :MLA_NOPE]).astype(BF16)
        t = tail * r * g_ref[:, MLA_NOPE:]
        t = t * cos + _rotate_half_tail(t) * sin
        o_ref[:, c0 + MLA_NOPE:c0 + MLA_QK_PAD] = t.astype(BF16)


def _mla_q_up(q_lat, w_qb_p, g_qn_p, cos, sin):
    n, k = q_lat.shape
    width = w_qb_p.shape[1]
    tm = 256
    return pl.pallas_call(
        _mla_q_body,
        out_shape=jax.ShapeDtypeStruct((n, width), BF16),
        grid=(n // tm,),
        in_specs=[pl.BlockSpec((tm, k), lambda i: (i, 0)),
                  pl.BlockSpec((k, width), lambda i: (0, 0)),
                  pl.BlockSpec((1, MLA_QK_PAD), lambda i: (0, 0)),
                  pl.BlockSpec((tm, V7X_LANES), lambda i: (i, 0)),
                  pl.BlockSpec((tm, V7X_LANES), lambda i: (i, 0))],
        out_specs=pl.BlockSpec((tm, width), lambda i: (i, 0)),
        compiler_params=_cparams(("arbitrary",)),
        name="mla_q_up",
    )(q_lat, w_qb_p, g_qn_p, cos, sin)


def _mla_kv_body(x_ref, w_ref, kr_ref, g_ref, cos_ref, sin_ref, k_ref, v_ref):
    acc = jnp.dot(x_ref[...].astype(BF16), w_ref[...], preferred_element_type=F32)
    cos, sin = cos_ref[...], sin_ref[...]
    kr = kr_ref[...]
    kr_ss = jnp.sum(kr * kr, axis=-1, keepdims=True)
    per = MLA_NOPE + MLA_V
    for h in range(MLA_HEADS):
        nope = acc[:, h * per:h * per + MLA_NOPE]
        r = lax.rsqrt((jnp.sum(nope * nope, axis=-1, keepdims=True) + kr_ss) / MLA_QK + EPS)
        c0 = h * MLA_QK_PAD
        k_ref[:, c0:c0 + MLA_NOPE] = (nope * r * g_ref[:, :MLA_NOPE]).astype(BF16)
        t = kr * r * g_ref[:, MLA_NOPE:]
        t = t * cos + _rotate_half_tail(t) * sin
        k_ref[:, c0 + MLA_NOPE:c0 + MLA_QK_PAD] = t.astype(BF16)
        v_ref[:, h * MLA_V:(h + 1) * MLA_V] = acc[:, h * per + MLA_NOPE:(h + 1) * per].astype(BF16)


def _mla_kv_up(c_kv, w_kvb, k_rope_p, g_kn_p, cos, sin):
    n, k = c_kv.shape
    tm = 256
    return pl.pallas_call(
        _mla_kv_body,
        out_shape=(jax.ShapeDtypeStruct((n, MLA_HEADS * MLA_QK_PAD), BF16),
                   jax.ShapeDtypeStruct((n, MLA_HEADS * MLA_V), BF16)),
        grid=(n // tm,),
        in_specs=[pl.BlockSpec((tm, k), lambda i: (i, 0)),
                  pl.BlockSpec(w_kvb.shape, lambda i: (0, 0)),
                  pl.BlockSpec((tm, V7X_LANES), lambda i: (i, 0)),
                  pl.BlockSpec((1, MLA_QK_PAD), lambda i: (0, 0)),
                  pl.BlockSpec((tm, V7X_LANES), lambda i: (i, 0)),
                  pl.BlockSpec((tm, V7X_LANES), lambda i: (i, 0))],
        out_specs=(pl.BlockSpec((tm, MLA_HEADS * MLA_QK_PAD), lambda i: (i, 0)),
                   pl.BlockSpec((tm, MLA_HEADS * MLA_V), lambda i: (i, 0))),
        compiler_params=_cparams(("arbitrary",)),
        name="mla_kv_up",
    )(c_kv, w_kvb, k_rope_p, g_kn_p, cos, sin)


def _even_mixer(h, dims, cache, p, tables):
    (bp, sp, dec_b, dec_s, past) = dims
    n_ctx = bp * sp
    n = h.shape[0]
    (w_in, g_q, w_qb, g_kv, w_kvb, g_qn, g_kn, g_wq, g_wk, sink) = p
    ckv_c, krope_c, wk_c, wv_c = cache
    (cos_h, sin_h), (cos_m, sin_m) = tables["head"], tables["mla"]
    o1 = MLA_Q_RANK
    o2 = o1 + MLA_KV_RANK
    o3 = o2 + MLA_ROPE
    pad = (-(w_in.shape[1])) % V7X_LANES
    w_in_p = jnp.concatenate([w_in[:, :o2], w_in[:, o3:], w_in[:, o2:o3],
                              jnp.zeros((w_in.shape[0], pad), w_in.dtype)], axis=1)
    proj = _matmul(h, w_in_p, out_dtype=F32, name="even_in_proj")
    c1 = o2 + WIN_HEADS * HEAD_DIM
    c2 = c1 + WIN_KV * HEAD_DIM
    c3 = c2 + WIN_KV * HEAD_DIM
    plan = ((0, o1, "norm", 0), (o1, MLA_KV_RANK, "norm_f32", 0),
            (o2, WIN_HEADS * HEAD_DIM, "heads", HEAD_DIM),
            (c1, WIN_KV * HEAD_DIM, "heads+state", HEAD_DIM),
            (c2, WIN_KV * HEAD_DIM, "cast", 0))
    q_lat, c_kv, wq_r, wk_r, wk, wv_b = _prep(
        proj, plan, ((g_q,), (g_kv,), (g_wq, cos_h, sin_h), (g_wk, cos_h, sin_h), ()), "even_prep")
    wv = proj[:n_ctx, c2:c3]
    k_rope_p = proj[:, c3:c3 + V7X_LANES]

    head_pad = ((0, 0), (0, 0), (0, MLA_QK_PAD - MLA_QK))
    w_qb_p = jnp.pad(w_qb.reshape(MLA_Q_RANK, MLA_HEADS, MLA_QK), head_pad)
    w_qb_p = w_qb_p.reshape(MLA_Q_RANK, MLA_HEADS * MLA_QK_PAD).astype(BF16)
    g_qn_p = jnp.pad(g_qn.astype(F32), (0, MLA_QK_PAD - MLA_QK)).reshape(1, MLA_QK_PAD)
    g_kn_p = jnp.pad(g_kn.astype(F32), (0, MLA_QK_PAD - MLA_QK)).reshape(1, MLA_QK_PAD)
    q_mla = _mla_q_up(q_lat, w_qb_p, g_qn_p, cos_m, sin_m)

    ckv_all = jnp.concatenate([c_kv, ckv_c.reshape(dec_b * past, MLA_KV_RANK)], axis=0)
    krope_cache = jnp.pad(krope_c.reshape(dec_b * past, MLA_ROPE), ((0, 0), (0, V7X_LANES - MLA_ROPE)))
    krope_all = jnp.concatenate([k_rope_p, krope_cache], axis=0)
    mk, mv = _mla_kv_up(ckv_all, w_kvb.astype(BF16), krope_all, g_kn_p, cos_m, sin_m)

    mla_scale = MLA_QK ** -0.5
    mla_cols = MLA_HEADS * MLA_V
    out = _attention(q_mla, 0, mk, mv, 0, batch=bp, seq=sp, nkv=MLA_HEADS, groups=1,
                     dq=MLA_QK_PAD, dv=MLA_V, scale=mla_scale, hb=MLA_HEADS, tq=256,
                     out_shape=(n, mla_cols + WIN_HEADS * HEAD_DIM), name="mla_attn_ctx")
    out = _attention(q_mla, n_ctx, mk, mv, n_ctx, batch=dec_b, seq=dec_s, nkv=MLA_HEADS, groups=1,
                     dq=MLA_QK_PAD, dv=MLA_V, scale=mla_scale, hb=4, tq=512,
                     k_ctx=mk, v_ctx=mv, kc_off=n, t_ctx=past, out=out, out_off=(n_ctx, 0),
                     name="mla_attn_lat")

    grp = WIN_HEADS // WIN_KV
    sink_b = jnp.broadcast_to(sink.astype(F32)[:, None], (WIN_HEADS, V7X_LANES))
    win_scale = HEAD_DIM ** -0.5
    out = _attention(wq_r, 0, wk_r, wv_b, 0, batch=bp, seq=sp, nkv=WIN_KV, groups=grp,
                     dq=HEAD_DIM, dv=HEAD_DIM, scale=win_scale, hb=WIN_KV, tq=256, sink=sink_b,
                     out=out, out_off=(0, mla_cols), name="win_attn_ctx")
    out = _attention(wq_r, n_ctx, wk_r, wv_b, n_ctx, batch=dec_b, seq=dec_s, nkv=WIN_KV, groups=grp,
                     dq=HEAD_DIM, dv=HEAD_DIM, scale=win_scale, hb=WIN_KV, tq=256,
                     k_ctx=wk_c.reshape(dec_b * past, WIN_KV * HEAD_DIM).astype(BF16),
                     v_ctx=wv_c.reshape(dec_b * past, WIN_KV * HEAD_DIM).astype(BF16),
                     kc_off=0, t_ctx=past, sink=sink_b, window=WINDOW,
                     out=out, out_off=(n_ctx, mla_cols), name="win_attn_lat")
    state = (c_kv[:n_ctx].reshape(bp, 1, sp, MLA_KV_RANK),
             k_rope_p[:n_ctx, :MLA_ROPE].reshape(bp, 1, sp, MLA_ROPE),
             wk[:n_ctx].reshape(bp, 1, sp, WIN_KV, HEAD_DIM),
             wv.reshape(bp, 1, sp, WIN_KV, HEAD_DIM))
    return out, state


def _odd_mixer(h, dims, cache, p, lam_init, layer, tables):
    (bp, sp, dec_b, dec_s, past) = dims
    n_ctx = bp * sp
    (w_in, g_dq, g_dk, lq1, lk1, lq2, lk2, g_sub, g_aq, g_ak) = p
    dk_c, dv_c, ak_c, av_c = cache
    (cos_h, sin_h), (cos_d, sin_d) = tables["head"], tables["diff"]
    o1 = DIFF_HEADS * HEAD_DIM
    o2 = o1 + DIFF_KV * HEAD_DIM
    o3 = o2 + DIFF_KV * HEAD_DIM
    o4 = o3 + AX_HEADS * HEAD_DIM
    o5 = o4 + AX_KV * HEAD_DIM
    proj = _matmul(h, w_in, layer=layer, out_dtype=F32, name="odd_in_proj")
    plan = ((0, o1, "heads", DIFF_D), (o1, o2 - o1, "heads+state", DIFF_D), (o2, o3 - o2, "cast", 0),
            (o3, o4 - o3, "heads", HEAD_DIM), (o4, o5 - o4, "heads+state", HEAD_DIM),
            (o5, AX_KV * HEAD_DIM, "cast", 0))
    dq_r, dk_r, dk, dv_b, aq_r, ak_r, ak, av_b = _prep(
        proj, plan, ((g_dq, cos_d, sin_d), (g_dk, cos_d, sin_d), (), (g_aq, cos_h, sin_h),
                     (g_ak, cos_h, sin_h), ()), "odd_prep")
    dv = proj[:n_ctx, o2:o3]
    av = proj[:n_ctx, o5:]

    lam = (jnp.exp(jnp.sum(lq1.astype(F32) * lk1.astype(F32)))
           - jnp.exp(jnp.sum(lq2.astype(F32) * lk2.astype(F32))) + lam_init)
    lam_b = jnp.broadcast_to(lam.astype(F32), (1, HEAD_DIM))
    g_sub_b = g_sub.astype(F32).reshape(1, HEAD_DIM)
    grp = DIFF_HEADS // DIFF_KV
    diff_kw = dict(nkv=DIFF_KV, groups=grp, dq=HEAD_DIM, dv=HEAD_DIM, scale=DIFF_D ** -0.5, hb=DIFF_KV,
                   lam=lam_b, g_sub=g_sub_b, diff_post_scale=1.0 - lam_init)
    n = n_ctx + dec_b * dec_s
    diff_cols = DIFF_HEADS * HEAD_DIM
    out = _attention(dq_r, 0, dk_r, dv_b, 0, batch=bp, seq=sp, tq=256,
                     out_shape=(n, diff_cols + AX_HEADS * HEAD_DIM), name="diff_attn_ctx", **diff_kw)
    out = _attention(dq_r, n_ctx, dk_r, dv_b, n_ctx, batch=dec_b, seq=dec_s, tq=128,
                     k_ctx=dk_c.reshape(dec_b * past, DIFF_KV * HEAD_DIM).astype(BF16),
                     v_ctx=dv_c.reshape(dec_b * past, DIFF_KV * HEAD_DIM).astype(BF16),
                     kc_off=0, t_ctx=past, out=out, out_off=(n_ctx, 0), name="diff_attn_lat", **diff_kw)
    agrp = AX_HEADS // AX_KV
    ax_kw = dict(nkv=AX_KV, groups=agrp, dq=HEAD_DIM, dv=HEAD_DIM, scale=HEAD_DIM ** -0.5, hb=AX_KV)
    out = _attention(aq_r, 0, ak_r, av_b, 0, batch=bp, seq=sp, tq=256,
                     out=out, out_off=(0, diff_cols), name="ax_attn_ctx", **ax_kw)
    out = _attention(aq_r, n_ctx, ak_r, av_b, n_ctx, batch=dec_b, seq=dec_s, tq=256,
                     k_ctx=ak_c.reshape(dec_b * past, AX_KV * HEAD_DIM).astype(BF16),
                     v_ctx=av_c.reshape(dec_b * past, AX_KV * HEAD_DIM).astype(BF16),
                     kc_off=0, t_ctx=past, out=out, out_off=(n_ctx, diff_cols), name="ax_attn_lat", **ax_kw)
    state = (dk[:n_ctx].reshape(bp, 1, sp, DIFF_KV, 2, DIFF_D),
             dv.reshape(bp, 1, sp, DIFF_KV, HEAD_DIM),
             ak[:n_ctx].reshape(bp, 1, sp, AX_KV, HEAD_DIM),
             av.reshape(bp, 1, sp, AX_KV, HEAD_DIM))
    return out, state


def kernel(x_prompt, x_sample, cache_mla_ckv, cache_mla_krope, cache_win_k, cache_win_v, cache_diff_k, cache_diff_v, cache_ax_k, cache_ax_v, c, c_ctx, w_mod, b_mod, g_norm_mix, g_norm_ffn, w_in_even, g_mla_q, w_mla_qb, g_mla_kv, w_mla_kvb, g_mla_qn, g_mla_kn, g_win_qn, g_win_kn, win_sink, w_out_even, w_in_odd, g_diff_qn, g_diff_kn, diff_lq1, diff_lk1, diff_lq2, diff_lk2, g_diff_sub, g_ax_qn, g_ax_kn, w_out_odd, w_router, b_router, w_exp_gate, w_exp_up, w_exp_down, w_sh_gate, w_sh_up, w_sh_down):
    bp, sp, d = x_prompt.shape
    dec_b, dec_s, _ = x_sample.shape
    depth = w_mod.shape[0]
    n_ctx = bp * sp
    n = n_ctx + dec_b * dec_s
    dims = (bp, sp, dec_b, dec_s, cache_mla_ckv.shape[2])
    group_fn = lambda tm: _group_index_fn(tm, n_ctx, dec_s)
    n_groups = 1 + dec_b

    y = jnp.concatenate([x_prompt.reshape(n_ctx, d), x_sample.reshape(dec_b * dec_s, d)], axis=0)

    cond = jnp.concatenate([c_ctx[None], c, jnp.zeros((8 - n_groups % 8, d), F32)], axis=0)
    cond = jax.nn.silu(cond)

    past = cache_mla_ckv.shape[2]
    tables = {"head": _rope_tables(HEAD_DIM, n_ctx, dec_b, dec_s, 0),
              "diff": _rope_tables(DIFF_D, n_ctx, dec_b, dec_s, 0),
              "mla": _rope_tables(MLA_ROPE, n_ctx, dec_b, dec_s, dec_b * past)}

    states_even, states_odd = [], []
    for l in range(depth):
        i = l // 2
        mod = _matmul(cond, w_mod, layer=l, out_dtype=F32, bias=b_mod.reshape(depth, 1, 6 * d),
                      name="modulation")
        mod = mod[:n_groups].reshape(n_groups, 6, 1, d)
        sh1, sc1, g1, sh2, sc2, g2 = (mod[:, j] for j in range(6))
        h = _ada_norm(y, g_norm_mix[l][None], sh1, sc1, group_fn, name="ada_norm_mix")
        if l % 2 == 0:
            pe = (w_in_even[i], g_mla_q[i], w_mla_qb[i], g_mla_kv[i], w_mla_kvb[i], g_mla_qn[i],
                  g_mla_kn[i], g_win_qn[i], g_win_kn[i], win_sink[i])
            cache = (cache_mla_ckv[:, i], cache_mla_krope[:, i], cache_win_k[:, i], cache_win_v[:, i])
            out, state = _even_mixer(h, dims, cache, pe, tables)
            states_even.append(state)
            w_out = w_out_even
        else:
            po = (w_in_odd, g_diff_qn[i], g_diff_kn[i], diff_lq1[i], diff_lk1[i], diff_lq2[i],
                  diff_lk2[i], g_diff_sub[i], g_ax_qn[i], g_ax_kn[i])
            cache = (cache_diff_k[:, i], cache_diff_v[:, i], cache_ax_k[:, i], cache_ax_v[:, i])
            lam_init = 0.8 - 0.6 * math.exp(-0.3 * l)
            out, state = _odd_mixer(h, dims, cache, po, lam_init, i, tables)
            states_odd.append(state)
            w_out = w_out_odd
        y = _matmul(out, w_out, layer=i, out_dtype=F32, resid=y, gates=g1, group_fn=group_fn,
                    name="mixer_out_proj")
        y = _moe(y, g_norm_ffn[l][None], sh2, sc2, g2, group_fn, l, w_router, b_router,
                 w_exp_gate, w_exp_up, w_exp_down, w_sh_gate, w_sh_up, w_sh_down)

    yp = y[:n_ctx].reshape(bp, sp, d)
    ys = y[n_ctx:].reshape(dec_b, dec_s, d)
    even = tuple(jnp.concatenate([s[j] for s in states_even], axis=1) for j in range(4))
    odd = tuple(jnp.concatenate([s[j] for s in states_odd], axis=1) for j in range(4))
    return (yp, ys) + even + odd
```

```python
import functools
import math

import jax
import jax.numpy as jnp
from jax import lax
from jax.experimental import pallas as pl
from jax.experimental.pallas import tpu as pltpu

F32 = jnp.float32
BF16 = jnp.bfloat16

GRID_W = 64
ROPE_THETA = 10000.0
EPS = 1e-6
NEG_INF = -1e30
LOG2E = math.log2(math.e)
HEAD_DIM = 128
MLA_HEADS = 16
MLA_Q_RANK = 768
MLA_KV_RANK = 512
MLA_NOPE = 128
MLA_ROPE = 64
MLA_V = 128
MLA_QK = MLA_NOPE + MLA_ROPE
MLA_QK_PAD = 256
WIN_HEADS = 16
WIN_KV = 4
WINDOW = 128
DIFF_HEADS = 16
DIFF_KV = 4
DIFF_D = HEAD_DIM // 2
AX_HEADS = 16
AX_KV = 4
N_EXPERTS = 64
N_EXPERT_GROUPS = 8
TOPK_GROUPS = 4
TOP_K = 8
ROUTED_SCALE = 2.5

V7X_LANES = 128
V7X_VMEM_LIMIT_BYTES = 56 * 1024 * 1024
MM_WEIGHT_TILE_BYTES = 32 * 1024 * 1024
MOE_ROW_TILE = 256
MOE_CHUNKS = 8


def _cparams(sem):
    return pltpu.CompilerParams(dimension_semantics=sem, vmem_limit_bytes=V7X_VMEM_LIMIT_BYTES)


def _largest_tile(n, cap, step):
    if n <= cap:
        return n
    t = (cap // step) * step
    while t >= step:
        if n % t == 0:
            return t
        t -= step
    raise ValueError(f"no tile for {n} under {cap}")


def _group_index_fn(tm, n_ctx_rows, dec_rows):
    assert n_ctx_rows % tm == 0 and dec_rows % tm == 0
    ctx_tiles = n_ctx_rows // tm
    per_dec = dec_rows // tm

    def fn(i):
        return jnp.where(i < ctx_tiles, 0, 1 + (jnp.maximum(i - ctx_tiles, 0)) // per_dec)

    return fn


def _mm_body(*refs, has_bias, has_resid):
    x_ref, w_ref = refs[0], refs[1]
    idx = 2
    if has_bias:
        bias_ref = refs[idx]
        idx += 1
    if has_resid:
        resid_ref, gate_ref = refs[idx], refs[idx + 1]
        idx += 2
    o_ref, wb_ref = refs[idx], refs[idx + 1]

    @pl.when(pl.program_id(1) == 0)
    def _():
        wb_ref[...] = w_ref[...].astype(BF16)

    acc = jnp.dot(x_ref[...].astype(BF16), wb_ref[...], preferred_element_type=F32)
    if has_bias:
        acc = acc + bias_ref[...]
    if has_resid:
        acc = resid_ref[...] + gate_ref[...] * acc
    o_ref[...] = acc.astype(o_ref.dtype)


def _matmul(x, w, *, out_dtype, layer=None, bias=None, resid=None, gates=None, group_fn=None, name):
    m, k = x.shape
    n = w.shape[-1]
    tm = _largest_tile(m, 512, 8)
    tn_cap = max(V7X_LANES, min(2048, MM_WEIGHT_TILE_BYTES // (k * 10)))
    tn = _largest_tile(n, tn_cap, V7X_LANES)
    grid = (n // tn, m // tm)
    if layer is None:
        w_spec = pl.BlockSpec((k, tn), lambda j, i: (0, j))
    else:
        w_spec = pl.BlockSpec((None, k, tn), lambda j, i: (layer, 0, j))
    in_specs = [pl.BlockSpec((tm, k), lambda j, i: (i, 0)), w_spec]
    args = [x, w]
    if bias is not None:
        in_specs.append(pl.BlockSpec((None, 1, tn), lambda j, i: (layer, 0, j)))
        args.append(bias)
    if resid is not None:
        gfn = group_fn(tm)
        in_specs.append(pl.BlockSpec((tm, tn), lambda j, i: (i, j)))
        in_specs.append(pl.BlockSpec((None, 1, tn), lambda j, i: (gfn(i), 0, j)))
        args += [resid, gates]
    return pl.pallas_call(
        functools.partial(_mm_body, has_bias=bias is not None, has_resid=resid is not None),
        out_shape=jax.ShapeDtypeStruct((m, n), out_dtype),
        grid=grid,
        in_specs=in_specs,
        out_specs=pl.BlockSpec((tm, tn), lambda j, i: (i, j)),
        scratch_shapes=[pltpu.VMEM((k, tn), BF16)],
        compiler_params=_cparams(("arbitrary", "arbitrary")),
        name=name,
    )(*args)


def _first_max(cur, sub, limit):
    m = jnp.max(cur, axis=0, keepdims=True)
    first = jnp.min(jnp.where(cur == m, sub, limit), axis=0, keepdims=True)
    return m, first, sub == first


def _route_columns(logits, bias, tri, carry):
    e, t = logits.shape
    per = e // N_EXPERT_GROUPS
    scores = jax.nn.sigmoid(logits)
    biased = scores + bias
    neg = -jnp.inf
    gsub = lax.broadcasted_iota(jnp.int32, (per, t), 0)
    grp = []
    for g in range(N_EXPERT_GROUPS):
        xg = biased[g * per:(g + 1) * per]
        m1, _, hit = _first_max(xg, gsub, per)
        m2 = jnp.max(jnp.where(hit, neg, xg), axis=0, keepdims=True)
        grp.append(m1 + m2)
    grp = jnp.concatenate(grp, axis=0)
    nsub = lax.broadcasted_iota(jnp.int32, (N_EXPERT_GROUPS, t), 0)
    gsel = jnp.zeros((N_EXPERT_GROUPS, t), F32)
    for _ in range(TOPK_GROUPS):
        _, _, hit = _first_max(grp, nsub, N_EXPERT_GROUPS)
        gsel = jnp.where(hit, 1.0, gsel)
        grp = jnp.where(hit, neg, grp)
    emask = jnp.concatenate([jnp.broadcast_to(gsel[g:g + 1], (per, t)) for g in range(N_EXPERT_GROUPS)], axis=0)
    cur = jnp.where(emask > 0.5, biased, neg)
    esub = lax.broadcasted_iota(jnp.int32, (e, t), 0)
    sel = jnp.zeros((e, t), F32)
    ids, ws, hits = [], [], []
    for _ in range(TOP_K):
        _, first, hit = _first_max(cur, esub, e)
        ids.append(first)
        ws.append(jnp.sum(jnp.where(hit, scores, 0.0), axis=0, keepdims=True))
        hits.append(hit)
        sel = jnp.where(hit, 1.0, sel)
        cur = jnp.where(hit, neg, cur)
    w = jnp.concatenate(ws, axis=0)
    gates = w / jnp.sum(w, axis=0, keepdims=True) * ROUTED_SCALE
    rank_all = jnp.dot(sel.astype(BF16), tri, preferred_element_type=F32) + carry
    ranks = [jnp.sum(jnp.where(hit, rank_all, 0.0), axis=0, keepdims=True) for hit in hits]
    counts = jnp.sum(sel, axis=1, keepdims=True)
    return (jnp.concatenate(ids, axis=0), gates, jnp.concatenate(ranks, axis=0).astype(jnp.int32), counts)


def _adanorm_body(x_ref, g_ref, shift_ref, scale_ref, *rest, has_router):
    x = x_ref[...]
    y = x * lax.rsqrt(jnp.mean(x * x, axis=-1, keepdims=True) + EPS) * g_ref[...]
    t = y * (1.0 + scale_ref[...]) + shift_ref[...]
    if has_router:
        whi_ref, wlo_ref, br_ref, tri_ref, o_ref, ids_ref, gate_ref, rank_ref, cnt_ref, carry_ref = rest

        @pl.when(pl.program_id(0) == 0)
        def _():
            carry_ref[...] = jnp.zeros_like(carry_ref)

        t_hi = t.astype(BF16)
        t_lo = (t - t_hi.astype(F32)).astype(BF16)
        nt = lambda a, b: lax.dot_general(a, b, (((1,), (1,)), ((), ())), preferred_element_type=F32)
        logits = nt(whi_ref[...], t_hi) + (nt(wlo_ref[...], t_hi) + nt(whi_ref[...], t_lo))
        ids, gates, ranks, counts = _route_columns(logits, br_ref[...], tri_ref[...], carry_ref[:, :1])
        ids_ref[...] = ids
        gate_ref[...] = gates
        rank_ref[...] = ranks
        carry_ref[...] = carry_ref[...] + counts
        cnt_ref[...] = carry_ref[...]
    else:
        (o_ref,) = rest
    o_ref[...] = t.astype(o_ref.dtype)


def _ada_norm(x, g, shift, scale, group_fn, *, w_router=None, b_router=None, name):
    n, d = x.shape
    tm = 256
    gfn = group_fn(tm)
    in_specs = [pl.BlockSpec((tm, d), lambda i: (i, 0)),
                pl.BlockSpec((1, d), lambda i: (0, 0)),
                pl.BlockSpec((None, 1, d), lambda i: (gfn(i), 0, 0)),
                pl.BlockSpec((None, 1, d), lambda i: (gfn(i), 0, 0))]
    args = [x, g, shift, scale]
    out_shape = jax.ShapeDtypeStruct((n, d), BF16)
    out_specs = pl.BlockSpec((tm, d), lambda i: (i, 0))
    scratch = []
    if w_router is not None:
        e = w_router.shape[1]
        tri = (lax.broadcasted_iota(jnp.int32, (tm, tm), 0) < lax.broadcasted_iota(jnp.int32, (tm, tm), 1))
        in_specs += [pl.BlockSpec((e, d), lambda i: (0, 0)),
                     pl.BlockSpec((e, d), lambda i: (0, 0)),
                     pl.BlockSpec((e, tm), lambda i: (0, 0)),
                     pl.BlockSpec((tm, tm), lambda i: (0, 0))]
        w_t = w_router.T.astype(F32)
        w_hi = w_t.astype(BF16)
        w_lo = (w_t - w_hi.astype(F32)).astype(BF16)
        args += [w_hi, w_lo, jnp.broadcast_to(b_router.astype(F32)[:, None], (e, tm)), tri.astype(BF16)]
        col = lambda rows, dt: (jax.ShapeDtypeStruct((rows, n), dt), pl.BlockSpec((rows, tm), lambda i: (0, i)))
        extra = [col(TOP_K, jnp.int32), col(TOP_K, F32), col(TOP_K, jnp.int32),
                 (jax.ShapeDtypeStruct((e, V7X_LANES), F32), pl.BlockSpec((e, V7X_LANES), lambda i: (0, 0)))]
        out_shape = (out_shape,) + tuple(s for s, _ in extra)
        out_specs = (out_specs,) + tuple(b for _, b in extra)
        scratch = [pltpu.VMEM((e, V7X_LANES), F32)]
    return pl.pallas_call(
        functools.partial(_adanorm_body, has_router=w_router is not None),
        out_shape=out_shape,
        grid=(n // tm,),
        in_specs=in_specs,
        out_specs=out_specs,
        scratch_shapes=scratch,
        compiler_params=_cparams(("arbitrary",)),
        name=name,
    )(*args)


def _attn_body(*refs, hb, groups, dq, dv, tq, tk, scale, t_new, t_ctx, window, has_sink,
               diff_post_scale, has_out_buf):
    it = iter(refs)
    q_ref, kn_ref, vn_ref = next(it), next(it), next(it)
    kc_ref = vc_ref = sink_ref = lam_ref = gsub_ref = None
    if t_ctx:
        kc_ref, vc_ref = next(it), next(it)
    if has_sink:
        sink_ref = next(it)
    diff = diff_post_scale is not None
    if diff:
        lam_ref, gsub_ref = next(it), next(it)
    if has_out_buf:
        next(it)
    o_ref = next(it)

    h = pl.program_id(1)
    i = pl.program_id(2)
    nstack = 2 * groups if diff else groups
    rows = nstack * tq

    def stack_rows(parts):
        return parts[0] if len(parts) == 1 else jnp.concatenate(parts, axis=0)

    if window is not None:
        wk = min(t_new, tq + 2 * window)
        if wk == t_new:
            wstart = 0
        else:
            wstart = pl.multiple_of(jnp.clip(i * tq - window, 0, t_new - wk), V7X_LANES)
        qi = i * tq + lax.broadcasted_iota(jnp.int32, (tq, wk), 0)
        kj = wstart + lax.broadcasted_iota(jnp.int32, (tq, wk), 1)
        wmask = stack_rows([jnp.abs(qi - kj) <= window] * nstack)

    for j in range(hb):
        qj = q_ref[:, j * groups * dq:(j + 1) * groups * dq]
        parts = [qj[:, g * dq:(g + 1) * dq] for g in range(groups)]
        if diff:
            lo = lax.broadcasted_iota(jnp.int32, (tq, dq), 1) < dq // 2
            zero = jnp.zeros((tq, dq), qj.dtype)
            parts = [jnp.where(lo, p, zero) for p in parts] + [jnp.where(lo, zero, p) for p in parts]
        qs = stack_rows(parts)

        if has_sink:
            sinks = []
            for g in range(groups):
                hh = (h * hb + j) * groups + g
                sinks.append(jnp.broadcast_to(sink_ref[pl.ds(hh, 1), :][:, :1], (tq, 1)))
            m = stack_rows(sinks) * LOG2E
            l = jnp.ones((rows, 1), F32)
        else:
            m = jnp.full((rows, 1), NEG_INF, F32)
            l = jnp.zeros((rows, 1), F32)
        acc = jnp.zeros((rows, dv), F32)

        def step(carry, kc, vc, mask):
            m, l, acc = carry
            s = lax.dot_general(qs, kc, (((1,), (1,)), ((), ())), preferred_element_type=F32) * (scale * LOG2E)
            if mask is not None:
                s = jnp.where(mask, s, NEG_INF)
            m_new = jnp.maximum(m, jnp.max(s, axis=-1, keepdims=True))
            alpha = jnp.exp2(m - m_new)
            p = jnp.exp2(s - m_new)
            l = alpha * l + jnp.sum(p, axis=-1, keepdims=True)
            acc = alpha * acc + jnp.dot(p.astype(BF16), vc, preferred_element_type=F32)
            return m_new, l, acc

        carry = (m, l, acc)
        kcols = slice(j * dq, (j + 1) * dq)
        vcols = slice(j * dv, (j + 1) * dv)
        if window is not None:
            carry = step(carry, kn_ref[pl.ds(wstart, wk), kcols], vn_ref[pl.ds(wstart, wk), vcols], wmask)
        else:
            for c in range(t_new // tk):
                carry = step(carry, kn_ref[c * tk:(c + 1) * tk, kcols], vn_ref[c * tk:(c + 1) * tk, vcols], None)
        if t_ctx:
            tkc = min(tk, t_ctx)
            for c in range(t_ctx // tkc):
                carry = step(carry, kc_ref[c * tkc:(c + 1) * tkc, kcols], vc_ref[c * tkc:(c + 1) * tkc, vcols], None)
        m, l, acc = carry
        o = acc / l
        if diff:
            half = groups * tq
            d = o[:half] - lam_ref[...] * o[half:]
            o = (d * lax.rsqrt(jnp.mean(d * d, axis=-1, keepdims=True) + EPS) * gsub_ref[...]) * diff_post_scale
        outs = [o[g * tq:(g + 1) * tq] for g in range(groups)]
        oj = outs[0] if groups == 1 else jnp.concatenate(outs, axis=1)
        o_ref[:, j * groups * dv:(j + 1) * groups * dv] = oj.astype(o_ref.dtype)


def _attention(q, q_off, k_new, v_new, kn_off, *, batch, seq, nkv, groups, dq, dv, scale, hb, tq,
               k_ctx=None, v_ctx=None, kc_off=0, t_ctx=0, sink=None, window=None,
               lam=None, g_sub=None, diff_post_scale=None, out=None, out_shape=None, out_off=(0, 0), name):
    t_new = seq
    tq = min(tq, seq)
    tk = min(512, t_new)
    assert seq % tq == 0 and t_new % tk == 0 and nkv % hb == 0
    assert q_off % tq == 0 and kn_off % t_new == 0
    qb, nb = q_off // tq, kn_off // t_new
    spt = seq // tq
    in_specs = [pl.BlockSpec((tq, hb * groups * dq), lambda b, h, i: (qb + b * spt + i, h)),
                pl.BlockSpec((t_new, hb * dq), lambda b, h, i: (nb + b, h)),
                pl.BlockSpec((t_new, hb * dv), lambda b, h, i: (nb + b, h))]
    args = [q, k_new, v_new]
    if t_ctx:
        assert kc_off % t_ctx == 0
        cb = kc_off // t_ctx
        in_specs += [pl.BlockSpec((t_ctx, hb * dq), lambda b, h, i: (cb + b, h)),
                     pl.BlockSpec((t_ctx, hb * dv), lambda b, h, i: (cb + b, h))]
        args += [k_ctx, v_ctx]
    if sink is not None:
        in_specs.append(pl.BlockSpec(sink.shape, lambda b, h, i: (0, 0)))
        args.append(sink)
    if diff_post_scale is not None:
        in_specs += [pl.BlockSpec((1, dv), lambda b, h, i: (0, 0))] * 2
        args += [lam, g_sub]
    aliases = {}
    if out is not None:
        out_shape = out.shape
        in_specs.append(pl.BlockSpec(memory_space=pl.ANY))
        args.append(out)
        aliases = {len(args) - 1: 0}
    wblk = hb * groups * dv
    assert out_off[0] % tq == 0 and out_off[1] % wblk == 0
    ob, oc = out_off[0] // tq, out_off[1] // wblk
    body = functools.partial(
        _attn_body, hb=hb, groups=groups, dq=dq, dv=dv, tq=tq, tk=tk, scale=scale, t_new=t_new,
        t_ctx=t_ctx, window=window, has_sink=sink is not None, diff_post_scale=diff_post_scale,
        has_out_buf=out is not None)
    return pl.pallas_call(
        body,
        out_shape=jax.ShapeDtypeStruct(out_shape, BF16),
        grid=(batch, nkv // hb, spt),
        in_specs=in_specs,
        out_specs=pl.BlockSpec((tq, wblk), lambda b, h, i: (ob + b * spt + i, oc + h)),
        input_output_aliases=aliases,
        compiler_params=_cparams(("arbitrary", "arbitrary", "arbitrary")),
        name=name,
    )(*args)


def _experts_body(te_ref, nv_ref, x_ref, wg_ref, wu_ref, wd_ref, *rest, tile0):
    o_ref, wgb, wub, wdb = rest[-4:]
    t = pl.program_id(0)
    g = tile0 + t
    valid = g < nv_ref[0]
    prev = te_ref[jnp.maximum(g - 1, 0)]
    first = jnp.logical_or(t == 0, te_ref[g] != prev)

    @pl.when(jnp.logical_and(first, valid))
    def _():
        wgb[...] = wg_ref[...].astype(BF16)
        wub[...] = wu_ref[...].astype(BF16)
        wdb[...] = wd_ref[...].astype(BF16)

    @pl.when(valid)
    def _():
        x = x_ref[...]
        a = jnp.dot(x, wgb[...], preferred_element_type=F32)
        u = jnp.dot(x, wub[...], preferred_element_type=F32)
        hcur = a * jax.nn.sigmoid(a) * u
        o_ref[...] = jnp.dot(hcur.astype(BF16), wdb[...], preferred_element_type=F32).astype(o_ref.dtype)

    @pl.when(jnp.logical_not(valid))
    def _():
        o_ref[...] = jnp.zeros_like(o_ref)


def _routed_experts(xs, tile_expert, n_valid, w_gate, w_up, w_down, layer, tile0, n_tiles, rows_buf):
    r, d = xs.shape
    ff = w_gate.shape[-1]
    tm = MOE_ROW_TILE
    chunk_tiles = r // tm

    def x_blk(t, te, nv):
        last = jnp.clip(nv[0] - tile0 - 1, 0, chunk_tiles - 1)
        return (jnp.minimum(t, last), 0)

    def o_blk(t, te, nv):
        return (jnp.where(tile0 + t < nv[0], tile0 + t, n_tiles), 0)

    w_blk = lambda t, te, nv: (layer, te[tile0 + t], 0, 0)
    in_specs = [pl.BlockSpec((tm, d), x_blk),
                pl.BlockSpec((None, None, d, ff), w_blk),
                pl.BlockSpec((None, None, d, ff), w_blk),
                pl.BlockSpec((None, None, ff, d), w_blk)]
    args = [tile_expert, n_valid, xs, w_gate, w_up, w_down]
    aliases = {}
    if rows_buf is not None:
        in_specs.append(pl.BlockSpec(memory_space=pl.ANY))
        args.append(rows_buf)
        aliases = {len(args) - 1: 0}
    grid_spec = pltpu.PrefetchScalarGridSpec(
        num_scalar_prefetch=2,
        grid=(chunk_tiles,),
        in_specs=in_specs,
        out_specs=pl.BlockSpec((tm, d), o_blk),
        scratch_shapes=[pltpu.VMEM((d, ff), BF16), pltpu.VMEM((d, ff), BF16), pltpu.VMEM((ff, d), BF16)],
    )
    return pl.pallas_call(
        functools.partial(_experts_body, tile0=tile0),
        out_shape=jax.ShapeDtypeStruct(((n_tiles + 1) * tm, d), BF16),
        grid_spec=grid_spec,
        input_output_aliases=aliases,
        compiler_params=_cparams(("arbitrary",)),
        name="routed_experts",
    )(*args)


def _shared_body(t_ref, wg_ref, wu_ref, wd_ref, resid_ref, gate_ref, o_ref):
    x = t_ref[...]
    a = jnp.dot(x, wg_ref[...], preferred_element_type=F32)
    u = jnp.dot(x, wu_ref[...], preferred_element_type=F32)
    hcur = (a * jax.nn.sigmoid(a) * u).astype(BF16)
    shared = jnp.dot(hcur, wd_ref[...], preferred_element_type=F32)
    o_ref[...] = resid_ref[...] + gate_ref[...] * shared


def _shared_expert(t, ws_gate, ws_up, ws_down, resid, gates, group_fn):
    n, d = t.shape
    ff = ws_gate.shape[1]
    tm = 256
    gfn = group_fn(tm)
    return pl.pallas_call(
        _shared_body,
        out_shape=jax.ShapeDtypeStruct((n, d), F32),
        grid=(n // tm,),
        in_specs=[pl.BlockSpec((tm, d), lambda i: (i, 0)),
                  pl.BlockSpec((d, ff), lambda i: (0, 0)),
                  pl.BlockSpec((d, ff), lambda i: (0, 0)),
                  pl.BlockSpec((ff, d), lambda i: (0, 0)),
                  pl.BlockSpec((tm, d), lambda i: (i, 0)),
                  pl.BlockSpec((None, 1, d), lambda i: (gfn(i), 0, 0))],
        out_specs=pl.BlockSpec((tm, d), lambda i: (i, 0)),
        compiler_params=_cparams(("arbitrary",)),
        name="shared_expert",
    )(t, ws_gate, ws_up, ws_down, resid, gates)


def _combine_body(rows_ref, w_ref, base_ref, gate_ref, o_ref):
    w = w_ref[...]
    acc = w[:, 0:1] * rows_ref[0].astype(F32)
    for c in range(1, rows_ref.shape[0]):
        acc = acc + w[:, c:c + 1] * rows_ref[c].astype(F32)
    o_ref[...] = base_ref[...] + gate_ref[...] * acc


def _combine(picked, weights, base, gates, group_fn, row_start, row_count):
    d = base.shape[1]
    k = picked.shape[0]
    tm = 128
    assert row_start % tm == 0 and row_count % tm == 0
    r0 = row_start // tm
    gfn = group_fn(tm)
    return pl.pallas_call(
        _combine_body,
        out_shape=jax.ShapeDtypeStruct((row_count, d), F32),
        grid=(row_count // tm,),
        in_specs=[pl.BlockSpec((k, tm, d), lambda i: (0, r0 + i, 0)),
                  pl.BlockSpec((tm, k), lambda i: (r0 + i, 0)),
                  pl.BlockSpec((tm, d), lambda i: (r0 + i, 0)),
                  pl.BlockSpec((None, 1, d), lambda i: (gfn(r0 + i), 0, 0))],
        out_specs=pl.BlockSpec((tm, d), lambda i: (i, 0)),
        compiler_params=_cparams(("arbitrary",)),
        name="moe_combine",
    )(picked, weights, base, gates)


def _plan_body(rs_ref, ids_ref, rank_ref, pos_ref):
    ids = ids_ref[...]
    base = jnp.zeros_like(ids)
    for e in range(N_EXPERTS):
        base = jnp.where(ids == e, rs_ref[e], base)
    pos_ref[...] = base + rank_ref[...]


def _dispatch_plan(ids, ranks, counts, tm):
    k, n = ids.shape
    e = counts.shape[0]
    n_rows = n * k + e * tm
    n_tiles = n_rows // tm
    tiles_e = (counts + tm - 1) // tm
    tile_end = jnp.cumsum(tiles_e)
    row_start = ((tile_end - tiles_e) * tm).astype(jnp.int32)
    tn = _largest_tile(n, 2048, V7X_LANES)
    pos = pl.pallas_call(
        _plan_body,
        out_shape=jax.ShapeDtypeStruct((k, n), jnp.int32),
        grid_spec=pltpu.PrefetchScalarGridSpec(
            num_scalar_prefetch=1,
            grid=(n // tn,),
            in_specs=[pl.BlockSpec((k, tn), lambda i, rs: (0, i))] * 2,
            out_specs=pl.BlockSpec((k, tn), lambda i, rs: (0, i))),
        compiler_params=_cparams(("arbitrary",)),
        name="dispatch_rows",
    )(row_start, ids, ranks)
    flat = pos.reshape(-1)
    tok = jnp.tile(jnp.arange(n, dtype=jnp.int32), k)
    row_token = (jnp.arange(n_rows, dtype=jnp.int32) % n).at[flat].set(tok)
    tile_ids = jnp.arange(n_tiles, dtype=jnp.int32)
    tile_expert = jnp.sum((tile_end[None, :] <= tile_ids[:, None]).astype(jnp.int32), axis=1)
    tile_expert = jnp.minimum(tile_expert, e - 1)
    return flat, row_token, tile_expert, tile_end[-1:].astype(jnp.int32)


def _moe(y, g_ffn, shift, scale, gates, group_fn, layer, w_router, b_router, w_gate, w_up, w_down,
         ws_gate, ws_up, ws_down, row_splits):
    n, d = y.shape
    t, ids, gate_w, ranks, counts = _ada_norm(y, g_ffn, shift, scale, group_fn, w_router=w_router[layer],
                                              b_router=b_router[layer], name="ada_norm_route")
    flat, row_token, tile_expert, n_valid = _dispatch_plan(
        ids, ranks, counts[:, 0].astype(jnp.int32), MOE_ROW_TILE)
    base = _shared_expert(t, ws_gate[layer].astype(BF16), ws_up[layer].astype(BF16),
                          ws_down[layer].astype(BF16), y, gates, group_fn)
    n_tiles = tile_expert.shape[0]
    chunk_tiles = n_tiles // MOE_CHUNKS
    chunk_rows = chunk_tiles * MOE_ROW_TILE
    rows = None
    for c in range(MOE_CHUNKS):
        xs = t.at[row_token[c * chunk_rows:(c + 1) * chunk_rows]].get(mode="promise_in_bounds")
        rows = _routed_experts(xs, tile_expert, n_valid, w_gate, w_up, w_down, layer,
                               c * chunk_tiles, n_tiles, rows)
    picked = rows.at[flat].get(mode="promise_in_bounds").reshape(TOP_K, n, d)
    weights = gate_w.T
    return [_combine(picked, weights, base, gates, group_fn, r0, cnt) for (r0, cnt) in row_splits]


def _rope_2d(n_tok, rot_dim):
    rows = n_tok // GRID_W
    row = jnp.broadcast_to(jnp.arange(rows, dtype=F32)[:, None], (rows, GRID_W)).reshape(-1)
    col = jnp.broadcast_to(jnp.arange(GRID_W, dtype=F32)[None, :], (rows, GRID_W)).reshape(-1)
    n_freq = rot_dim // 4
    inv = ROPE_THETA ** (-jnp.arange(n_freq, dtype=F32) / n_freq)
    ang = jnp.concatenate([row[:, None] * inv, col[:, None] * inv], axis=-1)
    return jnp.cos(ang), jnp.sin(ang)


def _rope_tables(rot_dim, n_ctx, dec_b, dec_s, n_tail):
    c, s = _rope_2d(dec_s, rot_dim)
    reps = V7X_LANES // rot_dim
    cos = jnp.concatenate([c, c] * reps, axis=1)
    sin = jnp.concatenate([-s, s] * reps, axis=1)
    if n_tail:
        fill = V7X_LANES - rot_dim
        cos = jnp.concatenate([c, c, jnp.ones((dec_s, fill), F32)], axis=1)
        sin = jnp.concatenate([-s, s, jnp.zeros((dec_s, fill), F32)], axis=1)
    ones = lambda r: jnp.ones((r, V7X_LANES), F32)
    zeros = lambda r: jnp.zeros((r, V7X_LANES), F32)
    cos = jnp.concatenate([ones(n_ctx)] + [cos] * dec_b + [ones(n_tail)], axis=0)
    sin = jnp.concatenate([zeros(n_ctx)] + [sin] * dec_b + [zeros(n_tail)], axis=0)
    return cos, sin


def _rotate_half(y, seg):
    half = seg // 2
    if seg == V7X_LANES:
        return pltpu.roll(y, half, axis=1)
    lane = lax.broadcasted_iota(jnp.int32, y.shape, 1)
    return jnp.where(lane % seg < half, pltpu.roll(y, V7X_LANES - half, axis=1), pltpu.roll(y, half, axis=1))


def _rotate_half_tail(t):
    half = MLA_ROPE // 2
    lane = lax.broadcasted_iota(jnp.int32, t.shape, 1)
    return jnp.where(lane < half, pltpu.roll(t, V7X_LANES - half, axis=1), pltpu.roll(t, half, axis=1))


def _seg_rms(x, seg):
    sq = x * x
    if seg == V7X_LANES:
        return x * lax.rsqrt(jnp.mean(sq, axis=-1, keepdims=True) + EPS)
    assert seg * 2 == V7X_LANES
    lo = lax.broadcasted_iota(jnp.int32, x.shape, 1) < seg
    s_lo = jnp.sum(jnp.where(lo, sq, 0.0), axis=-1, keepdims=True)
    s_hi = jnp.sum(jnp.where(lo, 0.0, sq), axis=-1, keepdims=True)
    return x * lax.rsqrt(jnp.where(lo, s_lo, s_hi) / seg + EPS)


def _prep_body(proj_ref, *refs, plan):
    it = iter(refs)
    inputs = []
    for (_, _, kind, _) in plan:
        if kind in ("norm", "norm_f32"):
            inputs.append((next(it),))
        elif kind.startswith("heads"):
            inputs.append((next(it), next(it), next(it)))
        else:
            inputs.append(())
    outs = list(it)
    oi = 0
    for (col, width, kind, seg), ins in zip(plan, inputs):
        if kind in ("norm", "norm_f32"):
            (g_ref,) = ins
            x = proj_ref[:, col:col + width]
            y = x * lax.rsqrt(jnp.mean(x * x, axis=-1, keepdims=True) + EPS) * g_ref[...]
            outs[oi][...] = y.astype(outs[oi].dtype)
            oi += 1
        elif kind.startswith("heads"):
            g_ref, cos_ref, sin_ref = ins
            with_state = kind.endswith("+state")
            cos, sin, g = cos_ref[...], sin_ref[...], g_ref[...]
            for c0 in range(0, width, V7X_LANES):
                y = _seg_rms(proj_ref[:, col + c0:col + c0 + V7X_LANES], seg) * g
                if with_state:
                    outs[oi + 1][:, c0:c0 + V7X_LANES] = y
                y = y * cos + _rotate_half(y, seg) * sin
                outs[oi][:, c0:c0 + V7X_LANES] = y.astype(BF16)
            oi += 2 if with_state else 1
        else:
            outs[oi][...] = proj_ref[:, col:col + width].astype(BF16)
            oi += 1


def _prep(proj, plan, params, name):
    n, width_all = proj.shape
    tm = 256
    in_specs = [pl.BlockSpec((tm, width_all), lambda i: (i, 0))]
    args = [proj]
    out_shape, out_specs = [], []
    for (col, width, kind, seg), ps in zip(plan, params):
        row_spec = pl.BlockSpec((tm, width), lambda i: (i, 0))
        if kind in ("norm", "norm_f32"):
            in_specs.append(pl.BlockSpec((1, width), lambda i: (0, 0)))
            args.append(ps[0].astype(F32).reshape(1, width))
            out_shape.append(jax.ShapeDtypeStruct((n, width), F32 if kind == "norm_f32" else BF16))
            out_specs.append(row_spec)
        elif kind.startswith("heads"):
            g, cos, sin = ps
            in_specs += [pl.BlockSpec((1, V7X_LANES), lambda i: (0, 0)),
                         pl.BlockSpec((tm, V7X_LANES), lambda i: (i, 0)),
                         pl.BlockSpec((tm, V7X_LANES), lambda i: (i, 0))]
            args += [jnp.tile(g.astype(F32), V7X_LANES // seg).reshape(1, V7X_LANES), cos, sin]
            out_shape.append(jax.ShapeDtypeStruct((n, width), BF16))
            out_specs.append(row_spec)
            if kind.endswith("+state"):
                out_shape.append(jax.ShapeDtypeStruct((n, width), F32))
                out_specs.append(row_spec)
        else:
            out_shape.append(jax.ShapeDtypeStruct((n, width), BF16))
            out_specs.append(row_spec)
    return pl.pallas_call(
        functools.partial(_prep_body, plan=plan),
        out_shape=tuple(out_shape),
        grid=(n // tm,),
        in_specs=in_specs,
        out_specs=tuple(out_specs),
        compiler_params=_cparams(("arbitrary",)),
        name=name,
    )(*args)


def _mla_q_body(x_ref, w_ref, g_ref, cos_ref, sin_ref, o_ref):
    acc = jnp.dot(x_ref[...], w_ref[...], preferred_element_type=F32)
    cos, sin = cos_ref[...], sin_ref[...]
    for h in range(MLA_HEADS):
        c0 = h * MLA_QK_PAD
        nope = acc[:, c0:c0 + MLA_NOPE]
        tail = acc[:, c0 + MLA_NOPE:c0 + MLA_QK_PAD]
        ss = jnp.sum(nope * nope, axis=-1, keepdims=True) + jnp.sum(tail * tail, axis=-1, keepdims=True)
        r = lax.rsqrt(ss / MLA_QK + EPS)
        o_ref[:, c0:c0 + MLA_NOPE] = (nope * r * g_ref[:, :MLA_NOPE]).astype(BF16)
        t = tail * r * g_ref[:, MLA_NOPE:]
        t = t * cos + _rotate_half_tail(t) * sin
        o_ref[:, c0 + MLA_NOPE:c0 + MLA_QK_PAD] = t.astype(BF16)


def _mla_q_up(q_lat, w_qb_p, g_qn_p, cos, sin):
    n, k = q_lat.shape
    width = w_qb_p.shape[1]
    tm = 256
    return pl.pallas_call(
        _mla_q_body,
        out_shape=jax.ShapeDtypeStruct((n, width), BF16),
        grid=(n // tm,),
        in_specs=[pl.BlockSpec((tm, k), lambda i: (i, 0)),
                  pl.BlockSpec((k, width), lambda i: (0, 0)),
                  pl.BlockSpec((1, MLA_QK_PAD), lambda i: (0, 0)),
                  pl.BlockSpec((tm, V7X_LANES), lambda i: (i, 0)),
                  pl.BlockSpec((tm, V7X_LANES), lambda i: (i, 0))],
        out_specs=pl.BlockSpec((tm, width), lambda i: (i, 0)),
        compiler_params=_cparams(("arbitrary",)),
        name="mla_q_up",
    )(q_lat, w_qb_p, g_qn_p, cos, sin)


def _mla_kv_body(x_ref, w_ref, kr_ref, g_ref, cos_ref, sin_ref, k_ref, v_ref):
    acc = jnp.dot(x_ref[...].astype(BF16), w_ref[...], preferred_element_type=F32)
    cos, sin = cos_ref[...], sin_ref[...]
    kr = kr_ref[...]
    kr_ss = jnp.sum(kr * kr, axis=-1, keepdims=True)
    per = MLA_NOPE + MLA_V
    for h in range(MLA_HEADS):
        nope = acc[:, h * per:h * per + MLA_NOPE]
        r = lax.rsqrt((jnp.sum(nope * nope, axis=-1, keepdims=True) + kr_ss) / MLA_QK + EPS)
        c0 = h * MLA_QK_PAD
        k_ref[:, c0:c0 + MLA_NOPE] = (nope * r * g_ref[:, :MLA_NOPE]).astype(BF16)
        t = kr * r * g_ref[:, MLA_NOPE:]
        t = t * cos + _rotate_half_tail(t) * sin
        k_ref[:, c0 + MLA_NOPE:c0 + MLA_QK_PAD] = t.astype(BF16)
        v_ref[:, h * MLA_V:(h + 1) * MLA_V] = acc[:, h * per + MLA_NOPE:(h + 1) * per].astype(BF16)


def _mla_kv_up(c_kv, w_kvb, k_rope_p, g_kn_p, cos, sin):
    n, k = c_kv.shape
    tm = 256
    return pl.pallas_call(
        _mla_kv_body,
        out_shape=(jax.ShapeDtypeStruct((n, MLA_HEADS * MLA_QK_PAD), BF16),
                   jax.ShapeDtypeStruct((n, MLA_HEADS * MLA_V), BF16)),
        grid=(n // tm,),
        in_specs=[pl.BlockSpec((tm, k), lambda i: (i, 0)),
                  pl.BlockSpec(w_kvb.shape, lambda i: (0, 0)),
                  pl.BlockSpec((tm, V7X_LANES), lambda i: (i, 0)),
                  pl.BlockSpec((1, MLA_QK_PAD), lambda i: (0, 0)),
                  pl.BlockSpec((tm, V7X_LANES), lambda i: (i, 0)),
                  pl.BlockSpec((tm, V7X_LANES), lambda i: (i, 0))],
        out_specs=(pl.BlockSpec((tm, MLA_HEADS * MLA_QK_PAD), lambda i: (i, 0)),
                   pl.BlockSpec((tm, MLA_HEADS * MLA_V), lambda i: (i, 0))),
        compiler_params=_cparams(("arbitrary",)),
        name="mla_kv_up",
    )(c_kv, w_kvb, k_rope_p, g_kn_p, cos, sin)


def _even_mixer(h, dims, cache, p, tables):
    (bp, sp, dec_b, dec_s, past) = dims
    n_ctx = bp * sp
    n = h.shape[0]
    (w_in, g_q, w_qb, g_kv, w_kvb, g_qn, g_kn, g_wq, g_wk, sink) = p
    ckv_c, krope_c, wk_c, wv_c = cache
    (cos_h, sin_h), (cos_m, sin_m) = tables["head"], tables["mla"]
    o1 = MLA_Q_RANK
    o2 = o1 + MLA_KV_RANK
    o3 = o2 + MLA_ROPE
    pad = (-(w_in.shape[1])) % V7X_LANES
    w_in_p = jnp.concatenate([w_in[:, :o2], w_in[:, o3:], w_in[:, o2:o3],
                              jnp.zeros((w_in.shape[0], pad), w_in.dtype)], axis=1)
    proj = _matmul(h, w_in_p, out_dtype=F32, name="even_in_proj")
    c1 = o2 + WIN_HEADS * HEAD_DIM
    c2 = c1 + WIN_KV * HEAD_DIM
    c3 = c2 + WIN_KV * HEAD_DIM
    plan = ((0, o1, "norm", 0), (o1, MLA_KV_RANK, "norm_f32", 0),
            (o2, WIN_HEADS * HEAD_DIM, "heads", HEAD_DIM),
            (c1, WIN_KV * HEAD_DIM, "heads+state", HEAD_DIM),
            (c2, WIN_KV * HEAD_DIM, "cast", 0))
    q_lat, c_kv, wq_r, wk_r, wk, wv_b = _prep(
        proj, plan, ((g_q,), (g_kv,), (g_wq, cos_h, sin_h), (g_wk, cos_h, sin_h), ()), "even_prep")
    wv = proj[:n_ctx, c2:c3]
    k_rope_p = proj[:, c3:c3 + V7X_LANES]

    head_pad = ((0, 0), (0, 0), (0, MLA_QK_PAD - MLA_QK))
    w_qb_p = jnp.pad(w_qb.reshape(MLA_Q_RANK, MLA_HEADS, MLA_QK), head_pad)
    w_qb_p = w_qb_p.reshape(MLA_Q_RANK, MLA_HEADS * MLA_QK_PAD).astype(BF16)
    g_qn_p = jnp.pad(g_qn.astype(F32), (0, MLA_QK_PAD - MLA_QK)).reshape(1, MLA_QK_PAD)
    g_kn_p = jnp.pad(g_kn.astype(F32), (0, MLA_QK_PAD - MLA_QK)).reshape(1, MLA_QK_PAD)
    q_mla = _mla_q_up(q_lat, w_qb_p, g_qn_p, cos_m, sin_m)

    ckv_all = jnp.concatenate([c_kv, ckv_c.reshape(dec_b * past, MLA_KV_RANK)], axis=0)
    krope_cache = jnp.pad(krope_c.reshape(dec_b * past, MLA_ROPE), ((0, 0), (0, V7X_LANES - MLA_ROPE)))
    krope_all = jnp.concatenate([k_rope_p, krope_cache], axis=0)
    mk, mv = _mla_kv_up(ckv_all, w_kvb.astype(BF16), krope_all, g_kn_p, cos_m, sin_m)

    mla_scale = MLA_QK ** -0.5
    mla_cols = MLA_HEADS * MLA_V
    out = _attention(q_mla, 0, mk, mv, 0, batch=bp, seq=sp, nkv=MLA_HEADS, groups=1,
                     dq=MLA_QK_PAD, dv=MLA_V, scale=mla_scale, hb=MLA_HEADS, tq=256,
                     out_shape=(n, mla_cols + WIN_HEADS * HEAD_DIM), name="mla_attn_ctx")
    out = _attention(q_mla, n_ctx, mk, mv, n_ctx, batch=dec_b, seq=dec_s, nkv=MLA_HEADS, groups=1,
                     dq=MLA_QK_PAD, dv=MLA_V, scale=mla_scale, hb=4, tq=512,
                     k_ctx=mk, v_ctx=mv, kc_off=n, t_ctx=past, out=out, out_off=(n_ctx, 0),
                     name="mla_attn_lat")

    grp = WIN_HEADS // WIN_KV
    sink_b = jnp.broadcast_to(sink.astype(F32)[:, None], (WIN_HEADS, V7X_LANES))
    win_scale = HEAD_DIM ** -0.5
    out = _attention(wq_r, 0, wk_r, wv_b, 0, batch=bp, seq=sp, nkv=WIN_KV, groups=grp,
                     dq=HEAD_DIM, dv=HEAD_DIM, scale=win_scale, hb=WIN_KV, tq=256, sink=sink_b,
                     out=out, out_off=(0, mla_cols), name="win_attn_ctx")
    out = _attention(wq_r, n_ctx, wk_r, wv_b, n_ctx, batch=dec_b, seq=dec_s, nkv=WIN_KV, groups=grp,
                     dq=HEAD_DIM, dv=HEAD_DIM, scale=win_scale, hb=WIN_KV, tq=256,
                     k_ctx=wk_c.reshape(dec_b * past, WIN_KV * HEAD_DIM).astype(BF16),
                     v_ctx=wv_c.reshape(dec_b * past, WIN_KV * HEAD_DIM).astype(BF16),
                     kc_off=0, t_ctx=past, sink=sink_b, window=WINDOW,
                     out=out, out_off=(n_ctx, mla_cols), name="win_attn_lat")
    state = (c_kv[:n_ctx].reshape(bp, 1, sp, MLA_KV_RANK),
             k_rope_p[:n_ctx, :MLA_ROPE].reshape(bp, 1, sp, MLA_ROPE),
             wk[:n_ctx].reshape(bp, 1, sp, WIN_KV, HEAD_DIM),
             wv.reshape(bp, 1, sp, WIN_KV, HEAD_DIM))
    return out, state


def _odd_mixer(h, dims, cache, p, lam_init, layer, tables):
    (bp, sp, dec_b, dec_s, past) = dims
    n_ctx = bp * sp
    (w_in, g_dq, g_dk, lq1, lk1, lq2, lk2, g_sub, g_aq, g_ak) = p
    dk_c, dv_c, ak_c, av_c = cache
    (cos_h, sin_h), (cos_d, sin_d) = tables["head"], tables["diff"]
    o1 = DIFF_HEADS * HEAD_DIM
    o2 = o1 + DIFF_KV * HEAD_DIM
    o3 = o2 + DIFF_KV * HEAD_DIM
    o4 = o3 + AX_HEADS * HEAD_DIM
    o5 = o4 + AX_KV * HEAD_DIM
    proj = _matmul(h, w_in, layer=layer, out_dtype=F32, name="odd_in_proj")
    plan = ((0, o1, "heads", DIFF_D), (o1, o2 - o1, "heads+state", DIFF_D), (o2, o3 - o2, "cast", 0),
            (o3, o4 - o3, "heads", HEAD_DIM), (o4, o5 - o4, "heads+state", HEAD_DIM),
            (o5, AX_KV * HEAD_DIM, "cast", 0))
    dq_r, dk_r, dk, dv_b, aq_r, ak_r, ak, av_b = _prep(
        proj, plan, ((g_dq, cos_d, sin_d), (g_dk, cos_d, sin_d), (), (g_aq, cos_h, sin_h),
                     (g_ak, cos_h, sin_h), ()), "odd_prep")
    dv = proj[:n_ctx, o2:o3]
    av = proj[:n_ctx, o5:]

    lam = (jnp.exp(jnp.sum(lq1.astype(F32) * lk1.astype(F32)))
           - jnp.exp(jnp.sum(lq2.astype(F32) * lk2.astype(F32))) + lam_init)
    lam_b = jnp.broadcast_to(lam.astype(F32), (1, HEAD_DIM))
    g_sub_b = g_sub.astype(F32).reshape(1, HEAD_DIM)
    grp = DIFF_HEADS // DIFF_KV
    diff_kw = dict(nkv=DIFF_KV, groups=grp, dq=HEAD_DIM, dv=HEAD_DIM, scale=DIFF_D ** -0.5, hb=DIFF_KV,
                   lam=lam_b, g_sub=g_sub_b, diff_post_scale=1.0 - lam_init)
    n = n_ctx + dec_b * dec_s
    diff_cols = DIFF_HEADS * HEAD_DIM
    out = _attention(dq_r, 0, dk_r, dv_b, 0, batch=bp, seq=sp, tq=256,
                     out_shape=(n, diff_cols + AX_HEADS * HEAD_DIM), name="diff_attn_ctx", **diff_kw)
    out = _attention(dq_r, n_ctx, dk_r, dv_b, n_ctx, batch=dec_b, seq=dec_s, tq=128,
                     k_ctx=dk_c.reshape(dec_b * past, DIFF_KV * HEAD_DIM).astype(BF16),
                     v_ctx=dv_c.reshape(dec_b * past, DIFF_KV * HEAD_DIM).astype(BF16),
                     kc_off=0, t_ctx=past, out=out, out_off=(n_ctx, 0), name="diff_attn_lat", **diff_kw)
    agrp = AX_HEADS // AX_KV
    ax_kw = dict(nkv=AX_KV, groups=agrp, dq=HEAD_DIM, dv=HEAD_DIM, scale=HEAD_DIM ** -0.5, hb=AX_KV)
    out = _attention(aq_r, 0, ak_r, av_b, 0, batch=bp, seq=sp, tq=256,
                     out=out, out_off=(0, diff_cols), name="ax_attn_ctx", **ax_kw)
    out = _attention(aq_r, n_ctx, ak_r, av_b, n_ctx, batch=dec_b, seq=dec_s, tq=256,
                     k_ctx=ak_c.reshape(dec_b * past, AX_KV * HEAD_DIM).astype(BF16),
                     v_ctx=av_c.reshape(dec_b * past, AX_KV * HEAD_DIM).astype(BF16),
                     kc_off=0, t_ctx=past, out=out, out_off=(n_ctx, diff_cols), name="ax_attn_lat", **ax_kw)
    state = (dk[:n_ctx].reshape(bp, 1, sp, DIFF_KV, 2, DIFF_D),
             dv.reshape(bp, 1, sp, DIFF_KV, HEAD_DIM),
             ak[:n_ctx].reshape(bp, 1, sp, AX_KV, HEAD_DIM),
             av.reshape(bp, 1, sp, AX_KV, HEAD_DIM))
    return out, state


def kernel(x_prompt, x_sample, cache_mla_ckv, cache_mla_krope, cache_win_k, cache_win_v, cache_diff_k, cache_diff_v, cache_ax_k, cache_ax_v, c, c_ctx, w_mod, b_mod, g_norm_mix, g_norm_ffn, w_in_even, g_mla_q, w_mla_qb, g_mla_kv, w_mla_kvb, g_mla_qn, g_mla_kn, g_win_qn, g_win_kn, win_sink, w_out_even, w_in_odd, g_diff_qn, g_diff_kn, diff_lq1, diff_lk1, diff_lq2, diff_lk2, g_diff_sub, g_ax_qn, g_ax_kn, w_out_odd, w_router, b_router, w_exp_gate, w_exp_up, w_exp_down, w_sh_gate, w_sh_up, w_sh_down):
    bp, sp, d = x_prompt.shape
    dec_b, dec_s, _ = x_sample.shape
    depth = w_mod.shape[0]
    n_ctx = bp * sp
    n = n_ctx + dec_b * dec_s
    dims = (bp, sp, dec_b, dec_s, cache_mla_ckv.shape[2])
    group_fn = lambda tm: _group_index_fn(tm, n_ctx, dec_s)
    n_groups = 1 + dec_b

    y = jnp.concatenate([x_prompt.reshape(n_ctx, d), x_sample.reshape(dec_b * dec_s, d)], axis=0)

    cond = jnp.concatenate([c_ctx[None], c, jnp.zeros((8 - n_groups % 8, d), F32)], axis=0)
    cond = jax.nn.silu(cond)

    past = cache_mla_ckv.shape[2]
    tables = {"head": _rope_tables(HEAD_DIM, n_ctx, dec_b, dec_s, 0),
              "diff": _rope_tables(DIFF_D, n_ctx, dec_b, dec_s, 0),
              "mla": _rope_tables(MLA_ROPE, n_ctx, dec_b, dec_s, dec_b * past)}

    states_even, states_odd = [], []
    for l in range(depth):
        i = l // 2
        mod = _matmul(cond, w_mod, layer=l, out_dtype=F32, bias=b_mod.reshape(depth, 1, 6 * d),
                      name="modulation")
        mod = mod[:n_groups].reshape(n_groups, 6, 1, d)
        sh1, sc1, g1, sh2, sc2, g2 = (mod[:, j] for j in range(6))
        h = _ada_norm(y, g_norm_mix[l][None], sh1, sc1, group_fn, name="ada_norm_mix")
        if l % 2 == 0:
            pe = (w_in_even[i], g_mla_q[i], w_mla_qb[i], g_mla_kv[i], w_mla_kvb[i], g_mla_qn[i],
                  g_mla_kn[i], g_win_qn[i], g_win_kn[i], win_sink[i])
            cache = (cache_mla_ckv[:, i], cache_mla_krope[:, i], cache_win_k[:, i], cache_win_v[:, i])
            out, state = _even_mixer(h, dims, cache, pe, tables)
            states_even.append(state)
            w_out = w_out_even
        else:
            po = (w_in_odd, g_diff_qn[i], g_diff_kn[i], diff_lq1[i], diff_lk1[i], diff_lq2[i],
                  diff_lk2[i], g_diff_sub[i], g_ax_qn[i], g_ax_kn[i])
            cache = (cache_diff_k[:, i], cache_diff_v[:, i], cache_ax_k[:, i], cache_ax_v[:, i])
            lam_init = 0.8 - 0.6 * math.exp(-0.3 * l)
            out, state = _odd_mixer(h, dims, cache, po, lam_init, i, tables)
            states_odd.append(state)
            w_out = w_out_odd
        y = _matmul(out, w_out, layer=i, out_dtype=F32, resid=y, gates=g1, group_fn=group_fn,
                    name="mixer_out_proj")
        splits = [(0, n)] if l + 1 < depth else [(0, n_ctx), (n_ctx, n - n_ctx)]
        outs = _moe(y, g_norm_ffn[l][None], sh2, sc2, g2, group_fn, l, w_router, b_router,
                    w_exp_gate, w_exp_up, w_exp_down, w_sh_gate, w_sh_up, w_sh_down, splits)
        y = outs[0]

    yp = outs[0].reshape(bp, sp, d)
    ys = outs[1].reshape(dec_b, dec_s, d)
    even = tuple(jnp.concatenate([s[j] for s in states_even], axis=1) for j in range(4))
    odd = tuple(jnp.concatenate([s[j] for s in states_odd], axis=1) for j in range(4))
    return (yp, ys) + even + odd
```

```python
import functools
import math

import jax
import jax.numpy as jnp
from jax import lax
from jax.experimental import pallas as pl
from jax.experimental.pallas import tpu as pltpu

F32 = jnp.float32
BF16 = jnp.bfloat16

GRID_W = 64
ROPE_THETA = 10000.0
EPS = 1e-6
NEG_INF = -1e30
LOG2E = math.log2(math.e)
HEAD_DIM = 128
MLA_HEADS = 16
MLA_Q_RANK = 768
MLA_KV_RANK = 512
MLA_NOPE = 128
MLA_ROPE = 64
MLA_V = 128
MLA_QK = MLA_NOPE + MLA_ROPE
MLA_QK_PAD = 256
WIN_HEADS = 16
WIN_KV = 4
WINDOW = 128
DIFF_HEADS = 16
DIFF_KV = 4
DIFF_D = HEAD_DIM // 2
AX_HEADS = 16
AX_KV = 4
N_EXPERTS = 64
N_EXPERT_GROUPS = 8
TOPK_GROUPS = 4
TOP_K = 8
ROUTED_SCALE = 2.5

V7X_LANES = 128
V7X_VMEM_LIMIT_BYTES = 56 * 1024 * 1024
MM_WEIGHT_TILE_BYTES = 32 * 1024 * 1024
MOE_ROW_TILE = 256
MOE_CHUNKS = 8


def _cparams(sem):
    return pltpu.CompilerParams(dimension_semantics=sem, vmem_limit_bytes=V7X_VMEM_LIMIT_BYTES)


def _largest_tile(n, cap, step):
    if n <= cap:
        return n
    t = (cap // step) * step
    while t >= step:
        if n % t == 0:
            return t
        t -= step
    raise ValueError(f"no tile for {n} under {cap}")


def _group_index_fn(tm, n_ctx_rows, dec_rows):
    assert n_ctx_rows % tm == 0 and dec_rows % tm == 0
    ctx_tiles = n_ctx_rows // tm
    per_dec = dec_rows // tm

    def fn(i):
        return jnp.where(i < ctx_tiles, 0, 1 + (jnp.maximum(i - ctx_tiles, 0)) // per_dec)

    return fn


def _mm_body(*refs, has_bias, has_resid):
    x_ref, w_ref = refs[0], refs[1]
    idx = 2
    if has_bias:
        bias_ref = refs[idx]
        idx += 1
    if has_resid:
        resid_ref, gate_ref = refs[idx], refs[idx + 1]
        idx += 2
    o_ref = refs[idx]
    if w_ref.dtype == BF16:
        wb_ref = w_ref
    else:
        wb_ref = refs[idx + 1]

        @pl.when(pl.program_id(1) == 0)
        def _():
            wb_ref[...] = w_ref[...].astype(BF16)

    acc = jnp.dot(x_ref[...].astype(BF16), wb_ref[...], preferred_element_type=F32)
    if has_bias:
        acc = acc + bias_ref[...]
    if has_resid:
        acc = resid_ref[...] + gate_ref[...] * acc
    o_ref[...] = acc.astype(o_ref.dtype)


def _matmul(x, w, *, out_dtype, layer=None, bias=None, resid=None, gates=None, group_fn=None, name):
    m, k = x.shape
    n = w.shape[-1]
    tm = _largest_tile(m, 512, 8)
    precast = w.dtype == BF16
    col_bytes = k * (4 if precast else 10)
    budget = MM_WEIGHT_TILE_BYTES // 2 if precast else MM_WEIGHT_TILE_BYTES
    tn_cap = max(V7X_LANES, min(2048, budget // col_bytes))
    tn = _largest_tile(n, tn_cap, V7X_LANES)
    grid = (n // tn, m // tm)
    if layer is None:
        w_spec = pl.BlockSpec((k, tn), lambda j, i: (0, j))
    else:
        w_spec = pl.BlockSpec((None, k, tn), lambda j, i: (layer, 0, j))
    in_specs = [pl.BlockSpec((tm, k), lambda j, i: (i, 0)), w_spec]
    args = [x, w]
    if bias is not None:
        in_specs.append(pl.BlockSpec((None, 1, tn), lambda j, i: (layer, 0, j)))
        args.append(bias)
    if resid is not None:
        gfn = group_fn(tm)
        in_specs.append(pl.BlockSpec((tm, tn), lambda j, i: (i, j)))
        in_specs.append(pl.BlockSpec((None, 1, tn), lambda j, i: (gfn(i), 0, j)))
        args += [resid, gates]
    return pl.pallas_call(
        functools.partial(_mm_body, has_bias=bias is not None, has_resid=resid is not None),
        out_shape=jax.ShapeDtypeStruct((m, n), out_dtype),
        grid=grid,
        in_specs=in_specs,
        out_specs=pl.BlockSpec((tm, tn), lambda j, i: (i, j)),
        scratch_shapes=[] if precast else [pltpu.VMEM((k, tn), BF16)],
        compiler_params=_cparams(("arbitrary", "arbitrary")),
        name=name,
    )(*args)


def _first_max(cur, sub, limit):
    m = jnp.max(cur, axis=0, keepdims=True)
    first = jnp.min(jnp.where(cur == m, sub, limit), axis=0, keepdims=True)
    return m, first, sub == first


def _route_columns(logits, bias, tri, carry):
    e, t = logits.shape
    per = e // N_EXPERT_GROUPS
    scores = jax.nn.sigmoid(logits)
    biased = scores + bias
    neg = -jnp.inf
    gsub = lax.broadcasted_iota(jnp.int32, (per, t), 0)
    grp = []
    for g in range(N_EXPERT_GROUPS):
        xg = biased[g * per:(g + 1) * per]
        m1, _, hit = _first_max(xg, gsub, per)
        m2 = jnp.max(jnp.where(hit, neg, xg), axis=0, keepdims=True)
        grp.append(m1 + m2)
    grp = jnp.concatenate(grp, axis=0)
    nsub = lax.broadcasted_iota(jnp.int32, (N_EXPERT_GROUPS, t), 0)
    gsel = jnp.zeros((N_EXPERT_GROUPS, t), F32)
    for _ in range(TOPK_GROUPS):
        _, _, hit = _first_max(grp, nsub, N_EXPERT_GROUPS)
        gsel = jnp.where(hit, 1.0, gsel)
        grp = jnp.where(hit, neg, grp)
    emask = jnp.concatenate([jnp.broadcast_to(gsel[g:g + 1], (per, t)) for g in range(N_EXPERT_GROUPS)], axis=0)
    cur = jnp.where(emask > 0.5, biased, neg)
    esub = lax.broadcasted_iota(jnp.int32, (e, t), 0)
    sel = jnp.zeros((e, t), F32)
    ids, ws, hits = [], [], []
    for _ in range(TOP_K):
        _, first, hit = _first_max(cur, esub, e)
        ids.append(first)
        ws.append(jnp.sum(jnp.where(hit, scores, 0.0), axis=0, keepdims=True))
        hits.append(hit)
        sel = jnp.where(hit, 1.0, sel)
        cur = jnp.where(hit, neg, cur)
    w = jnp.concatenate(ws, axis=0)
    gates = w / jnp.sum(w, axis=0, keepdims=True) * ROUTED_SCALE
    rank_all = jnp.dot(sel.astype(BF16), tri, preferred_element_type=F32) + carry
    ranks = [jnp.sum(jnp.where(hit, rank_all, 0.0), axis=0, keepdims=True) for hit in hits]
    counts = jnp.sum(sel, axis=1, keepdims=True)
    return (jnp.concatenate(ids, axis=0), gates, jnp.concatenate(ranks, axis=0).astype(jnp.int32), counts,
            rank_all + sel)


def _adanorm_body(x_ref, g_ref, shift_ref, scale_ref, *rest, has_router):
    x = x_ref[...]
    y = x * lax.rsqrt(jnp.mean(x * x, axis=-1, keepdims=True) + EPS) * g_ref[...]
    t = y * (1.0 + scale_ref[...]) + shift_ref[...]
    if has_router:
        (whi_ref, wlo_ref, br_ref, tri_ref, o_ref, ids_ref, gate_ref, rank_ref, cnt_ref, cum_ref,
         carry_ref) = rest

        @pl.when(pl.program_id(0) == 0)
        def _():
            carry_ref[...] = jnp.zeros_like(carry_ref)

        t_hi = t.astype(BF16)
        t_lo = (t - t_hi.astype(F32)).astype(BF16)
        nt = lambda a, b: lax.dot_general(a, b, (((1,), (1,)), ((), ())), preferred_element_type=F32)
        logits = nt(whi_ref[...], t_hi) + (nt(wlo_ref[...], t_hi) + nt(whi_ref[...], t_lo))
        ids, gates, ranks, counts, cum = _route_columns(logits, br_ref[...], tri_ref[...], carry_ref[:, :1])
        cum_ref[...] = cum
        ids_ref[...] = ids
        gate_ref[...] = gates
        rank_ref[...] = ranks
        carry_ref[...] = carry_ref[...] + counts
        cnt_ref[...] = carry_ref[...]
    else:
        (o_ref,) = rest
    o_ref[...] = t.astype(o_ref.dtype)


def _ada_norm(x, g, shift, scale, group_fn, *, w_router=None, b_router=None, name):
    n, d = x.shape
    tm = 256
    gfn = group_fn(tm)
    in_specs = [pl.BlockSpec((tm, d), lambda i: (i, 0)),
                pl.BlockSpec((1, d), lambda i: (0, 0)),
                pl.BlockSpec((None, 1, d), lambda i: (gfn(i), 0, 0)),
                pl.BlockSpec((None, 1, d), lambda i: (gfn(i), 0, 0))]
    args = [x, g, shift, scale]
    out_shape = jax.ShapeDtypeStruct((n, d), BF16)
    out_specs = pl.BlockSpec((tm, d), lambda i: (i, 0))
    scratch = []
    if w_router is not None:
        e = w_router.shape[1]
        tri = (lax.broadcasted_iota(jnp.int32, (tm, tm), 0) < lax.broadcasted_iota(jnp.int32, (tm, tm), 1))
        in_specs += [pl.BlockSpec((e, d), lambda i: (0, 0)),
                     pl.BlockSpec((e, d), lambda i: (0, 0)),
                     pl.BlockSpec((e, tm), lambda i: (0, 0)),
                     pl.BlockSpec((tm, tm), lambda i: (0, 0))]
        w_t = w_router.T.astype(F32)
        w_hi = w_t.astype(BF16)
        w_lo = (w_t - w_hi.astype(F32)).astype(BF16)
        args += [w_hi, w_lo, jnp.broadcast_to(b_router.astype(F32)[:, None], (e, tm)), tri.astype(BF16)]
        col = lambda rows, dt: (jax.ShapeDtypeStruct((rows, n), dt), pl.BlockSpec((rows, tm), lambda i: (0, i)))
        extra = [col(TOP_K, jnp.int32), col(TOP_K, F32), col(TOP_K, jnp.int32),
                 (jax.ShapeDtypeStruct((e, V7X_LANES), F32), pl.BlockSpec((e, V7X_LANES), lambda i: (0, 0))),
                 col(e, F32)]
        out_shape = (out_shape,) + tuple(s for s, _ in extra)
        out_specs = (out_specs,) + tuple(b for _, b in extra)
        scratch = [pltpu.VMEM((e, V7X_LANES), F32)]
    return pl.pallas_call(
        functools.partial(_adanorm_body, has_router=w_router is not None),
        out_shape=out_shape,
        grid=(n // tm,),
        in_specs=in_specs,
        out_specs=out_specs,
        scratch_shapes=scratch,
        compiler_params=_cparams(("arbitrary",)),
        name=name,
    )(*args)


def _attn_body(*refs, hb, groups, dq, dv, tq, tk, scale, t_new, t_ctx, window, has_sink,
               diff_post_scale, has_out_buf):
    it = iter(refs)
    q_ref, kn_ref, vn_ref = next(it), next(it), next(it)
    kc_ref = vc_ref = sink_ref = lam_ref = gsub_ref = None
    if t_ctx:
        kc_ref, vc_ref = next(it), next(it)
    if has_sink:
        sink_ref = next(it)
    diff = diff_post_scale is not None
    if diff:
        lam_ref, gsub_ref = next(it), next(it)
    if has_out_buf:
        next(it)
    o_ref = next(it)

    h = pl.program_id(1)
    i = pl.program_id(2)
    nstack = 2 * groups if diff else groups
    rows = nstack * tq

    def stack_rows(parts):
        return parts[0] if len(parts) == 1 else jnp.concatenate(parts, axis=0)

    if window is not None:
        wk = min(t_new, tq + 2 * window)
        if wk == t_new:
            wstart = 0
        else:
            wstart = pl.multiple_of(jnp.clip(i * tq - window, 0, t_new - wk), V7X_LANES)
        qi = i * tq + lax.broadcasted_iota(jnp.int32, (tq, wk), 0)
        kj = wstart + lax.broadcasted_iota(jnp.int32, (tq, wk), 1)
        wmask = stack_rows([jnp.abs(qi - kj) <= window] * nstack)

    for j in range(hb):
        qj = q_ref[:, j * groups * dq:(j + 1) * groups * dq]
        parts = [qj[:, g * dq:(g + 1) * dq] for g in range(groups)]
        if diff:
            lo = lax.broadcasted_iota(jnp.int32, (tq, dq), 1) < dq // 2
            zero = jnp.zeros((tq, dq), qj.dtype)
            parts = [jnp.where(lo, p, zero) for p in parts] + [jnp.where(lo, zero, p) for p in parts]
        qs = stack_rows(parts)

        if has_sink:
            sinks = []
            for g in range(groups):
                hh = (h * hb + j) * groups + g
                sinks.append(jnp.broadcast_to(sink_ref[pl.ds(hh, 1), :][:, :1], (tq, 1)))
            m = stack_rows(sinks) * LOG2E
            l = jnp.ones((rows, 1), F32)
        else:
            m = jnp.full((rows, 1), NEG_INF, F32)
            l = jnp.zeros((rows, 1), F32)
        acc = jnp.zeros((rows, dv), F32)

        def step(carry, kc, vc, mask):
            m, l, acc = carry
            s = lax.dot_general(qs, kc, (((1,), (1,)), ((), ())), preferred_element_type=F32) * (scale * LOG2E)
            if mask is not None:
                s = jnp.where(mask, s, NEG_INF)
            m_new = jnp.maximum(m, jnp.max(s, axis=-1, keepdims=True))
            alpha = jnp.exp2(m - m_new)
            p = jnp.exp2(s - m_new)
            l = alpha * l + jnp.sum(p, axis=-1, keepdims=True)
            acc = alpha * acc + jnp.dot(p.astype(BF16), vc, preferred_element_type=F32)
            return m_new, l, acc

        carry = (m, l, acc)
        kcols = slice(j * dq, (j + 1) * dq)
        vcols = slice(j * dv, (j + 1) * dv)
        if window is not None:
            carry = step(carry, kn_ref[pl.ds(wstart, wk), kcols], vn_ref[pl.ds(wstart, wk), vcols], wmask)
        else:
            for c in range(t_new // tk):
                carry = step(carry, kn_ref[c * tk:(c + 1) * tk, kcols], vn_ref[c * tk:(c + 1) * tk, vcols], None)
        if t_ctx:
            tkc = min(tk, t_ctx)
            for c in range(t_ctx // tkc):
                carry = step(carry, kc_ref[c * tkc:(c + 1) * tkc, kcols], vc_ref[c * tkc:(c + 1) * tkc, vcols], None)
        m, l, acc = carry
        o = acc / l
        if diff:
            half = groups * tq
            d = o[:half] - lam_ref[...] * o[half:]
            o = (d * lax.rsqrt(jnp.mean(d * d, axis=-1, keepdims=True) + EPS) * gsub_ref[...]) * diff_post_scale
        outs = [o[g * tq:(g + 1) * tq] for g in range(groups)]
        oj = outs[0] if groups == 1 else jnp.concatenate(outs, axis=1)
        o_ref[:, j * groups * dv:(j + 1) * groups * dv] = oj.astype(o_ref.dtype)


def _attention(q, q_off, k_new, v_new, kn_off, *, batch, seq, nkv, groups, dq, dv, scale, hb, tq,
               k_ctx=None, v_ctx=None, kc_off=0, t_ctx=0, sink=None, window=None,
               lam=None, g_sub=None, diff_post_scale=None, out=None, out_shape=None, out_off=(0, 0), name):
    t_new = seq
    tq = min(tq, seq)
    tk = min(512, t_new)
    assert seq % tq == 0 and t_new % tk == 0 and nkv % hb == 0
    assert q_off % tq == 0 and kn_off % t_new == 0
    qb, nb = q_off // tq, kn_off // t_new
    spt = seq // tq
    in_specs = [pl.BlockSpec((tq, hb * groups * dq), lambda b, h, i: (qb + b * spt + i, h)),
                pl.BlockSpec((t_new, hb * dq), lambda b, h, i: (nb + b, h)),
                pl.BlockSpec((t_new, hb * dv), lambda b, h, i: (nb + b, h))]
    args = [q, k_new, v_new]
    if t_ctx:
        assert kc_off % t_ctx == 0
        cb = kc_off // t_ctx
        in_specs += [pl.BlockSpec((t_ctx, hb * dq), lambda b, h, i: (cb + b, h)),
                     pl.BlockSpec((t_ctx, hb * dv), lambda b, h, i: (cb + b, h))]
        args += [k_ctx, v_ctx]
    if sink is not None:
        in_specs.append(pl.BlockSpec(sink.shape, lambda b, h, i: (0, 0)))
        args.append(sink)
    if diff_post_scale is not None:
        in_specs += [pl.BlockSpec((1, dv), lambda b, h, i: (0, 0))] * 2
        args += [lam, g_sub]
    aliases = {}
    if out is not None:
        out_shape = out.shape
        in_specs.append(pl.BlockSpec(memory_space=pl.ANY))
        args.append(out)
        aliases = {len(args) - 1: 0}
    wblk = hb * groups * dv
    assert out_off[0] % tq == 0 and out_off[1] % wblk == 0
    ob, oc = out_off[0] // tq, out_off[1] // wblk
    body = functools.partial(
        _attn_body, hb=hb, groups=groups, dq=dq, dv=dv, tq=tq, tk=tk, scale=scale, t_new=t_new,
        t_ctx=t_ctx, window=window, has_sink=sink is not None, diff_post_scale=diff_post_scale,
        has_out_buf=out is not None)
    return pl.pallas_call(
        body,
        out_shape=jax.ShapeDtypeStruct(out_shape, BF16),
        grid=(batch, nkv // hb, spt),
        in_specs=in_specs,
        out_specs=pl.BlockSpec((tq, wblk), lambda b, h, i: (ob + b * spt + i, oc + h)),
        input_output_aliases=aliases,
        compiler_params=_cparams(("arbitrary", "arbitrary", "arbitrary")),
        name=name,
    )(*args)


def _experts_body(te_ref, nv_ref, x_ref, wg_ref, wu_ref, wd_ref, *rest, tile0):
    o_ref, wgb, wub, wdb = rest[-4:]
    t = pl.program_id(0)
    g = tile0 + t
    valid = g < nv_ref[0]
    prev = te_ref[jnp.maximum(g - 1, 0)]
    first = jnp.logical_or(t == 0, te_ref[g] != prev)

    @pl.when(jnp.logical_and(first, valid))
    def _():
        wgb[...] = wg_ref[...].astype(BF16)
        wub[...] = wu_ref[...].astype(BF16)
        wdb[...] = wd_ref[...].astype(BF16)

    @pl.when(valid)
    def _():
        x = x_ref[...]
        a = jnp.dot(x, wgb[...], preferred_element_type=F32)
        u = jnp.dot(x, wub[...], preferred_element_type=F32)
        hcur = a * jax.nn.sigmoid(a) * u
        o_ref[...] = jnp.dot(hcur.astype(BF16), wdb[...], preferred_element_type=F32).astype(o_ref.dtype)

    @pl.when(jnp.logical_not(valid))
    def _():
        o_ref[...] = jnp.zeros_like(o_ref)


def _routed_experts(xs, tile_expert, n_valid, w_gate, w_up, w_down, layer, tile0, n_tiles, rows_buf):
    r, d = xs.shape
    ff = w_gate.shape[-1]
    tm = MOE_ROW_TILE
    chunk_tiles = r // tm

    def x_blk(t, te, nv):
        last = jnp.clip(nv[0] - tile0 - 1, 0, chunk_tiles - 1)
        return (jnp.minimum(t, last), 0)

    def o_blk(t, te, nv):
        return (jnp.where(tile0 + t < nv[0], tile0 + t, n_tiles), 0)

    w_blk = lambda t, te, nv: (layer, te[tile0 + t], 0, 0)
    in_specs = [pl.BlockSpec((tm, d), x_blk),
                pl.BlockSpec((None, None, d, ff), w_blk),
                pl.BlockSpec((None, None, d, ff), w_blk),
                pl.BlockSpec((None, None, ff, d), w_blk)]
    args = [tile_expert, n_valid, xs, w_gate, w_up, w_down]
    aliases = {}
    if rows_buf is not None:
        in_specs.append(pl.BlockSpec(memory_space=pl.ANY))
        args.append(rows_buf)
        aliases = {len(args) - 1: 0}
    grid_spec = pltpu.PrefetchScalarGridSpec(
        num_scalar_prefetch=2,
        grid=(chunk_tiles,),
        in_specs=in_specs,
        out_specs=pl.BlockSpec((tm, d), o_blk),
        scratch_shapes=[pltpu.VMEM((d, ff), BF16), pltpu.VMEM((d, ff), BF16), pltpu.VMEM((ff, d), BF16)],
    )
    return pl.pallas_call(
        functools.partial(_experts_body, tile0=tile0),
        out_shape=jax.ShapeDtypeStruct(((n_tiles + 1) * tm, d), BF16),
        grid_spec=grid_spec,
        input_output_aliases=aliases,
        compiler_params=_cparams(("arbitrary",)),
        name="routed_experts",
    )(*args)


def _shared_body(t_ref, wg_ref, wu_ref, wd_ref, resid_ref, gate_ref, o_ref):
    x = t_ref[...]
    a = jnp.dot(x, wg_ref[...], preferred_element_type=F32)
    u = jnp.dot(x, wu_ref[...], preferred_element_type=F32)
    hcur = (a * jax.nn.sigmoid(a) * u).astype(BF16)
    shared = jnp.dot(hcur, wd_ref[...], preferred_element_type=F32)
    o_ref[...] = resid_ref[...] + gate_ref[...] * shared


def _shared_expert(t, ws_gate, ws_up, ws_down, resid, gates, group_fn):
    n, d = t.shape
    ff = ws_gate.shape[1]
    tm = 256
    gfn = group_fn(tm)
    return pl.pallas_call(
        _shared_body,
        out_shape=jax.ShapeDtypeStruct((n, d), F32),
        grid=(n // tm,),
        in_specs=[pl.BlockSpec((tm, d), lambda i: (i, 0)),
                  pl.BlockSpec((d, ff), lambda i: (0, 0)),
                  pl.BlockSpec((d, ff), lambda i: (0, 0)),
                  pl.BlockSpec((ff, d), lambda i: (0, 0)),
                  pl.BlockSpec((tm, d), lambda i: (i, 0)),
                  pl.BlockSpec((None, 1, d), lambda i: (gfn(i), 0, 0))],
        out_specs=pl.BlockSpec((tm, d), lambda i: (i, 0)),
        compiler_params=_cparams(("arbitrary",)),
        name="shared_expert",
    )(t, ws_gate, ws_up, ws_down, resid, gates)


def _combine_body(rows_ref, w_ref, base_ref, gate_ref, o_ref):
    w = w_ref[...]
    acc = w[:, 0:1] * rows_ref[0].astype(F32)
    for c in range(1, rows_ref.shape[0]):
        acc = acc + w[:, c:c + 1] * rows_ref[c].astype(F32)
    o_ref[...] = base_ref[...] + gate_ref[...] * acc


def _combine(picked, weights, base, gates, group_fn, row_start, row_count):
    d = base.shape[1]
    k = picked.shape[0]
    tm = 128
    assert row_start % tm == 0 and row_count % tm == 0
    r0 = row_start // tm
    gfn = group_fn(tm)
    return pl.pallas_call(
        _combine_body,
        out_shape=jax.ShapeDtypeStruct((row_count, d), F32),
        grid=(row_count // tm,),
        in_specs=[pl.BlockSpec((k, tm, d), lambda i: (0, r0 + i, 0)),
                  pl.BlockSpec((tm, k), lambda i: (r0 + i, 0)),
                  pl.BlockSpec((tm, d), lambda i: (r0 + i, 0)),
                  pl.BlockSpec((None, 1, d), lambda i: (gfn(r0 + i), 0, 0))],
        out_specs=pl.BlockSpec((tm, d), lambda i: (i, 0)),
        compiler_params=_cparams(("arbitrary",)),
        name="moe_combine",
    )(picked, weights, base, gates)


def _plan_body(rs_ref, ids_ref, rank_ref, pos_ref):
    ids = ids_ref[...]
    base = jnp.zeros_like(ids)
    for e in range(N_EXPERTS):
        base = jnp.where(ids == e, rs_ref[e], base)
    pos_ref[...] = base + rank_ref[...]


def _row_token_body(te_ref, j0_ref, blo_ref, bhi_ref, cum_ref, o_ref, *, tm, n_tok):
    t = pl.program_id(0)
    e = te_ref[t]
    sub = lax.broadcasted_iota(jnp.int32, (tm, 1), 0)
    j = (j0_ref[t] + sub).astype(F32)

    def body(b, acc):
        c = cum_ref[e, pl.ds(b, 1), :]
        return acc + jnp.where(c <= j, 1.0, 0.0)

    acc = lax.fori_loop(blo_ref[t], bhi_ref[t], body, jnp.zeros((tm, V7X_LANES), F32))
    cnt = (blo_ref[t] * V7X_LANES).astype(F32) + jnp.sum(acc, axis=1, keepdims=True)
    spare = lax.rem(t * tm, n_tok) + sub
    spare = jnp.where(spare >= n_tok, spare - n_tok, spare).astype(F32)
    tok = jnp.where(cnt >= n_tok, spare, cnt)
    o_ref[...] = jnp.broadcast_to(tok, (tm, V7X_LANES)).T[0:1, :].astype(jnp.int32)


def _dispatch_plan(ids, ranks, counts, cum, tm):
    k, n = ids.shape
    e = counts.shape[0]
    n_rows = n * k + e * tm
    n_tiles = n_rows // tm
    tiles_e = (counts + tm - 1) // tm
    tile_end = jnp.cumsum(tiles_e)
    row_start = ((tile_end - tiles_e) * tm).astype(jnp.int32)
    tile_ids = jnp.arange(n_tiles, dtype=jnp.int32)
    tile_expert = jnp.sum((tile_end[None, :] <= tile_ids[:, None]).astype(jnp.int32), axis=1)
    tile_expert = jnp.minimum(tile_expert, e - 1)

    assert n % V7X_LANES == 0 and tm <= n
    nb = n // V7X_LANES
    cum_b = cum.reshape(e, nb, V7X_LANES)
    j0 = tile_ids * tm - row_start[tile_expert]
    blk_last = cum_b[:, :, -1][tile_expert]
    blk_first = cum_b[:, :, 0][tile_expert]
    blo = jnp.sum((blk_last <= j0[:, None].astype(F32)).astype(jnp.int32), axis=1)
    bhi = jnp.sum((blk_first <= (j0[:, None] + (tm - 1)).astype(F32)).astype(jnp.int32), axis=1)
    row_token = pl.pallas_call(
        functools.partial(_row_token_body, tm=tm, n_tok=n),
        out_shape=jax.ShapeDtypeStruct((n_tiles, 1, tm), jnp.int32),
        grid_spec=pltpu.PrefetchScalarGridSpec(
            num_scalar_prefetch=4,
            grid=(n_tiles,),
            in_specs=[pl.BlockSpec((e, nb, V7X_LANES), lambda t, *_: (0, 0, 0))],
            out_specs=pl.BlockSpec((None, 1, tm), lambda t, *_: (t, 0, 0))),
        compiler_params=_cparams(("arbitrary",)),
        name="row_tokens",
    )(tile_expert, j0, blo, jnp.maximum(bhi, blo), cum_b).reshape(n_rows)

    tn = _largest_tile(n, 2048, V7X_LANES)
    pos = pl.pallas_call(
        _plan_body,
        out_shape=jax.ShapeDtypeStruct((k, n), jnp.int32),
        grid_spec=pltpu.PrefetchScalarGridSpec(
            num_scalar_prefetch=1,
            grid=(n // tn,),
            in_specs=[pl.BlockSpec((k, tn), lambda i, rs: (0, i))] * 2,
            out_specs=pl.BlockSpec((k, tn), lambda i, rs: (0, i))),
        compiler_params=_cparams(("arbitrary",)),
        name="dispatch_rows",
    )(row_start, ids, ranks)
    return pos.reshape(-1), row_token, tile_expert, tile_end[-1:].astype(jnp.int32)


def _moe(y, g_ffn, shift, scale, gates, group_fn, layer, w_router, b_router, w_gate, w_up, w_down,
         ws_gate, ws_up, ws_down, row_splits):
    n, d = y.shape
    t, ids, gate_w, ranks, counts, cum = _ada_norm(y, g_ffn, shift, scale, group_fn, w_router=w_router[layer],
                                                   b_router=b_router[layer], name="ada_norm_route")
    flat, row_token, tile_expert, n_valid = _dispatch_plan(
        ids, ranks, counts[:, 0].astype(jnp.int32), cum, MOE_ROW_TILE)
    base = _shared_expert(t, ws_gate[layer].astype(BF16), ws_up[layer].astype(BF16),
                          ws_down[layer].astype(BF16), y, gates, group_fn)
    n_tiles = tile_expert.shape[0]
    chunk_tiles = n_tiles // MOE_CHUNKS
    chunk_rows = chunk_tiles * MOE_ROW_TILE
    rows = None
    for c in range(MOE_CHUNKS):
        xs = t.at[row_token[c * chunk_rows:(c + 1) * chunk_rows]].get(mode="promise_in_bounds")
        rows = _routed_experts(xs, tile_expert, n_valid, w_gate, w_up, w_down, layer,
                               c * chunk_tiles, n_tiles, rows)
    picked = rows.at[flat].get(mode="promise_in_bounds").reshape(TOP_K, n, d)
    weights = gate_w.T
    return [_combine(picked, weights, base, gates, group_fn, r0, cnt) for (r0, cnt) in row_splits]


def _rope_2d(n_tok, rot_dim):
    rows = n_tok // GRID_W
    row = jnp.broadcast_to(jnp.arange(rows, dtype=F32)[:, None], (rows, GRID_W)).reshape(-1)
    col = jnp.broadcast_to(jnp.arange(GRID_W, dtype=F32)[None, :], (rows, GRID_W)).reshape(-1)
    n_freq = rot_dim // 4
    inv = ROPE_THETA ** (-jnp.arange(n_freq, dtype=F32) / n_freq)
    ang = jnp.concatenate([row[:, None] * inv, col[:, None] * inv], axis=-1)
    return jnp.cos(ang), jnp.sin(ang)


def _rope_tables(rot_dim, n_ctx, dec_b, dec_s, n_tail):
    c, s = _rope_2d(dec_s, rot_dim)
    reps = V7X_LANES // rot_dim
    cos = jnp.concatenate([c, c] * reps, axis=1)
    sin = jnp.concatenate([-s, s] * reps, axis=1)
    if n_tail:
        fill = V7X_LANES - rot_dim
        cos = jnp.concatenate([c, c, jnp.ones((dec_s, fill), F32)], axis=1)
        sin = jnp.concatenate([-s, s, jnp.zeros((dec_s, fill), F32)], axis=1)
    ones = lambda r: jnp.ones((r, V7X_LANES), F32)
    zeros = lambda r: jnp.zeros((r, V7X_LANES), F32)
    cos = jnp.concatenate([ones(n_ctx)] + [cos] * dec_b + [ones(n_tail)], axis=0)
    sin = jnp.concatenate([zeros(n_ctx)] + [sin] * dec_b + [zeros(n_tail)], axis=0)
    return cos, sin


def _rotate_half(y, seg):
    half = seg // 2
    if seg == V7X_LANES:
        return pltpu.roll(y, half, axis=1)
    lane = lax.broadcasted_iota(jnp.int32, y.shape, 1)
    return jnp.where(lane % seg < half, pltpu.roll(y, V7X_LANES - half, axis=1), pltpu.roll(y, half, axis=1))


def _rotate_half_tail(t):
    half = MLA_ROPE // 2
    lane = lax.broadcasted_iota(jnp.int32, t.shape, 1)
    return jnp.where(lane < half, pltpu.roll(t, V7X_LANES - half, axis=1), pltpu.roll(t, half, axis=1))


def _seg_rms(x, seg):
    sq = x * x
    if seg == V7X_LANES:
        return x * lax.rsqrt(jnp.mean(sq, axis=-1, keepdims=True) + EPS)
    assert seg * 2 == V7X_LANES
    lo = lax.broadcasted_iota(jnp.int32, x.shape, 1) < seg
    s_lo = jnp.sum(jnp.where(lo, sq, 0.0), axis=-1, keepdims=True)
    s_hi = jnp.sum(jnp.where(lo, 0.0, sq), axis=-1, keepdims=True)
    return x * lax.rsqrt(jnp.where(lo, s_lo, s_hi) / seg + EPS)


def _prep_body(proj_ref, *refs, plan):
    it = iter(refs)
    inputs = []
    for (_, _, kind, _) in plan:
        if kind in ("norm", "norm_f32"):
            inputs.append((next(it),))
        elif kind.startswith("heads"):
            inputs.append((next(it), next(it), next(it)))
        else:
            inputs.append(())
    outs = list(it)
    oi = 0
    for (col, width, kind, seg), ins in zip(plan, inputs):
        if kind in ("norm", "norm_f32"):
            (g_ref,) = ins
            x = proj_ref[:, col:col + width]
            y = x * lax.rsqrt(jnp.mean(x * x, axis=-1, keepdims=True) + EPS) * g_ref[...]
            outs[oi][...] = y.astype(outs[oi].dtype)
            oi += 1
        elif kind.startswith("heads"):
            g_ref, cos_ref, sin_ref = ins
            with_state = kind.endswith("+state")
            cos, sin, g = cos_ref[...], sin_ref[...], g_ref[...]
            for c0 in range(0, width, V7X_LANES):
                y = _seg_rms(proj_ref[:, col + c0:col + c0 + V7X_LANES], seg) * g
                if with_state:
                    outs[oi + 1][:, c0:c0 + V7X_LANES] = y
                y = y * cos + _rotate_half(y, seg) * sin
                outs[oi][:, c0:c0 + V7X_LANES] = y.astype(BF16)
            oi += 2 if with_state else 1
        else:
            outs[oi][...] = proj_ref[:, col:col + width].astype(BF16)
            oi += 1


def _prep(proj, plan, params, name):
    n, width_all = proj.shape
    tm = 256
    in_specs = [pl.BlockSpec((tm, width_all), lambda i: (i, 0))]
    args = [proj]
    out_shape, out_specs = [], []
    for (col, width, kind, seg), ps in zip(plan, params):
        row_spec = pl.BlockSpec((tm, width), lambda i: (i, 0))
        if kind in ("norm", "norm_f32"):
            in_specs.append(pl.BlockSpec((1, width), lambda i: (0, 0)))
            args.append(ps[0].astype(F32).reshape(1, width))
            out_shape.append(jax.ShapeDtypeStruct((n, width), F32 if kind == "norm_f32" else BF16))
            out_specs.append(row_spec)
        elif kind.startswith("heads"):
            g, cos, sin = ps
            in_specs += [pl.BlockSpec((1, V7X_LANES), lambda i: (0, 0)),
                         pl.BlockSpec((tm, V7X_LANES), lambda i: (i, 0)),
                         pl.BlockSpec((tm, V7X_LANES), lambda i: (i, 0))]
            args += [jnp.tile(g.astype(F32), V7X_LANES // seg).reshape(1, V7X_LANES), cos, sin]
            out_shape.append(jax.ShapeDtypeStruct((n, width), BF16))
            out_specs.append(row_spec)
            if kind.endswith("+state"):
                out_shape.append(jax.ShapeDtypeStruct((n, width), F32))
                out_specs.append(row_spec)
        else:
            out_shape.append(jax.ShapeDtypeStruct((n, width), BF16))
            out_specs.append(row_spec)
    return pl.pallas_call(
        functools.partial(_prep_body, plan=plan),
        out_shape=tuple(out_shape),
        grid=(n // tm,),
        in_specs=in_specs,
        out_specs=tuple(out_specs),
        compiler_params=_cparams(("arbitrary",)),
        name=name,
    )(*args)


def _mla_q_body(x_ref, w_ref, g_ref, cos_ref, sin_ref, o_ref):
    acc = jnp.dot(x_ref[...], w_ref[...], preferred_element_type=F32)
    cos, sin = cos_ref[...], sin_ref[...]
    for h in range(MLA_HEADS):
        c0 = h * MLA_QK_PAD
        nope = acc[:, c0:c0 + MLA_NOPE]
        tail = acc[:, c0 + MLA_NOPE:c0 + MLA_QK_PAD]
        ss = jnp.sum(nope * nope, axis=-1, keepdims=True) + jnp.sum(tail * tail, axis=-1, keepdims=True)
        r = lax.rsqrt(ss / MLA_QK + EPS)
        o_ref[:, c0:c0 + MLA_NOPE] = (nope * r * g_ref[:, :MLA_NOPE]).astype(BF16)
        t = tail * r * g_ref[:, MLA_NOPE:]
        t = t * cos + _rotate_half_tail(t) * sin
        o_ref[:, c0 + MLA_NOPE:c0 + MLA_QK_PAD] = t.astype(BF16)


def _mla_q_up(q_lat, w_qb_p, g_qn_p, cos, sin):
    n, k = q_lat.shape
    width = w_qb_p.shape[1]
    tm = 256
    return pl.pallas_call(
        _mla_q_body,
        out_shape=jax.ShapeDtypeStruct((n, width), BF16),
        grid=(n // tm,),
        in_specs=[pl.BlockSpec((tm, k), lambda i: (i, 0)),
                  pl.BlockSpec((k, width), lambda i: (0, 0)),
                  pl.BlockSpec((1, MLA_QK_PAD), lambda i: (0, 0)),
                  pl.BlockSpec((tm, V7X_LANES), lambda i: (i, 0)),
                  pl.BlockSpec((tm, V7X_LANES), lambda i: (i, 0))],
        out_specs=pl.BlockSpec((tm, width), lambda i: (i, 0)),
        compiler_params=_cparams(("arbitrary",)),
        name="mla_q_up",
    )(q_lat, w_qb_p, g_qn_p, cos, sin)


def _mla_kv_body(x_ref, w_ref, kr_ref, g_ref, cos_ref, sin_ref, k_ref, v_ref):
    acc = jnp.dot(x_ref[...].astype(BF16), w_ref[...], preferred_element_type=F32)
    cos, sin = cos_ref[...], sin_ref[...]
    kr = kr_ref[...]
    kr_ss = jnp.sum(kr * kr, axis=-1, keepdims=True)
    per = MLA_NOPE + MLA_V
    for h in range(MLA_HEADS):
        nope = acc[:, h * per:h * per + MLA_NOPE]
        r = lax.rsqrt((jnp.sum(nope * nope, axis=-1, keepdims=True) + kr_ss) / MLA_QK + EPS)
        c0 = h * MLA_QK_PAD
        k_ref[:, c0:c0 + MLA_NOPE] = (nope * r * g_ref[:, :MLA_NOPE]).astype(BF16)
        t = kr * r * g_ref[:, MLA_NOPE:]
        t = t * cos + _rotate_half_tail(t) * sin
        k_ref[:, c0 + MLA_NOPE:c0 + MLA_QK_PAD] = t.astype(BF16)
        v_ref[:, h * MLA_V:(h + 1) * MLA_V] = acc[:, h * per + MLA_NOPE:(h + 1) * per].astype(BF16)


def _mla_kv_up(c_kv, w_kvb, k_rope_p, g_kn_p, cos, sin):
    n, k = c_kv.shape
    tm = 256
    return pl.pallas_call(
        _mla_kv_body,
        out_shape=(jax.ShapeDtypeStruct((n, MLA_HEADS * MLA_QK_PAD), BF16),
                   jax.ShapeDtypeStruct((n, MLA_HEADS * MLA_V), BF16)),
        grid=(n // tm,),
        in_specs=[pl.BlockSpec((tm, k), lambda i: (i, 0)),
                  pl.BlockSpec(w_kvb.shape, lambda i: (0, 0)),
                  pl.BlockSpec((tm, V7X_LANES), lambda i: (i, 0)),
                  pl.BlockSpec((1, MLA_QK_PAD), lambda i: (0, 0)),
                  pl.BlockSpec((tm, V7X_LANES), lambda i: (i, 0)),
                  pl.BlockSpec((tm, V7X_LANES), lambda i: (i, 0))],
        out_specs=(pl.BlockSpec((tm, MLA_HEADS * MLA_QK_PAD), lambda i: (i, 0)),
                   pl.BlockSpec((tm, MLA_HEADS * MLA_V), lambda i: (i, 0))),
        compiler_params=_cparams(("arbitrary",)),
        name="mla_kv_up",
    )(c_kv, w_kvb, k_rope_p, g_kn_p, cos, sin)


def _even_mixer(h, dims, cache, p, tables):
    (bp, sp, dec_b, dec_s, past) = dims
    n_ctx = bp * sp
    n = h.shape[0]
    (w_in, g_q, w_qb, g_kv, w_kvb, g_qn, g_kn, g_wq, g_wk, sink) = p
    ckv_c, krope_c, wk_c, wv_c = cache
    (cos_h, sin_h), (cos_m, sin_m) = tables["head"], tables["mla"]
    o1 = MLA_Q_RANK
    o2 = o1 + MLA_KV_RANK
    o3 = o2 + MLA_ROPE
    pad = (-(w_in.shape[1])) % V7X_LANES
    w_in_p = jnp.concatenate([w_in[:, :o2], w_in[:, o3:], w_in[:, o2:o3],
                              jnp.zeros((w_in.shape[0], pad), w_in.dtype)], axis=1).astype(BF16)
    proj = _matmul(h, w_in_p, out_dtype=F32, name="even_in_proj")
    c1 = o2 + WIN_HEADS * HEAD_DIM
    c2 = c1 + WIN_KV * HEAD_DIM
    c3 = c2 + WIN_KV * HEAD_DIM
    plan = ((0, o1, "norm", 0), (o1, MLA_KV_RANK, "norm_f32", 0),
            (o2, WIN_HEADS * HEAD_DIM, "heads", HEAD_DIM),
            (c1, WIN_KV * HEAD_DIM, "heads+state", HEAD_DIM),
            (c2, WIN_KV * HEAD_DIM, "cast", 0))
    q_lat, c_kv, wq_r, wk_r, wk, wv_b = _prep(
        proj, plan, ((g_q,), (g_kv,), (g_wq, cos_h, sin_h), (g_wk, cos_h, sin_h), ()), "even_prep")
    wv = proj[:n_ctx, c2:c3]
    k_rope_p = proj[:, c3:c3 + V7X_LANES]

    head_pad = ((0, 0), (0, 0), (0, MLA_QK_PAD - MLA_QK))
    w_qb_p = jnp.pad(w_qb.reshape(MLA_Q_RANK, MLA_HEADS, MLA_QK), head_pad)
    w_qb_p = w_qb_p.reshape(MLA_Q_RANK, MLA_HEADS * MLA_QK_PAD).astype(BF16)
    g_qn_p = jnp.pad(g_qn.astype(F32), (0, MLA_QK_PAD - MLA_QK)).reshape(1, MLA_QK_PAD)
    g_kn_p = jnp.pad(g_kn.astype(F32), (0, MLA_QK_PAD - MLA_QK)).reshape(1, MLA_QK_PAD)
    q_mla = _mla_q_up(q_lat, w_qb_p, g_qn_p, cos_m, sin_m)

    ckv_all = jnp.concatenate([c_kv, ckv_c.reshape(dec_b * past, MLA_KV_RANK)], axis=0)
    krope_cache = jnp.pad(krope_c.reshape(dec_b * past, MLA_ROPE), ((0, 0), (0, V7X_LANES - MLA_ROPE)))
    krope_all = jnp.concatenate([k_rope_p, krope_cache], axis=0)
    mk, mv = _mla_kv_up(ckv_all, w_kvb.astype(BF16), krope_all, g_kn_p, cos_m, sin_m)

    mla_scale = MLA_QK ** -0.5
    mla_cols = MLA_HEADS * MLA_V
    out = _attention(q_mla, 0, mk, mv, 0, batch=bp, seq=sp, nkv=MLA_HEADS, groups=1,
                     dq=MLA_QK_PAD, dv=MLA_V, scale=mla_scale, hb=MLA_HEADS, tq=256,
                     out_shape=(n, mla_cols + WIN_HEADS * HEAD_DIM), name="mla_attn_ctx")
    out = _attention(q_mla, n_ctx, mk, mv, n_ctx, batch=dec_b, seq=dec_s, nkv=MLA_HEADS, groups=1,
                     dq=MLA_QK_PAD, dv=MLA_V, scale=mla_scale, hb=4, tq=512,
                     k_ctx=mk, v_ctx=mv, kc_off=n, t_ctx=past, out=out, out_off=(n_ctx, 0),
                     name="mla_attn_lat")

    grp = WIN_HEADS // WIN_KV
    sink_b = jnp.broadcast_to(sink.astype(F32)[:, None], (WIN_HEADS, V7X_LANES))
    win_scale = HEAD_DIM ** -0.5
    out = _attention(wq_r, 0, wk_r, wv_b, 0, batch=bp, seq=sp, nkv=WIN_KV, groups=grp,
                     dq=HEAD_DIM, dv=HEAD_DIM, scale=win_scale, hb=WIN_KV, tq=256, sink=sink_b,
                     out=out, out_off=(0, mla_cols), name="win_attn_ctx")
    out = _attention(wq_r, n_ctx, wk_r, wv_b, n_ctx, batch=dec_b, seq=dec_s, nkv=WIN_KV, groups=grp,
                     dq=HEAD_DIM, dv=HEAD_DIM, scale=win_scale, hb=WIN_KV, tq=256,
                     k_ctx=wk_c.reshape(dec_b * past, WIN_KV * HEAD_DIM).astype(BF16),
                     v_ctx=wv_c.reshape(dec_b * past, WIN_KV * HEAD_DIM).astype(BF16),
                     kc_off=0, t_ctx=past, sink=sink_b, window=WINDOW,
                     out=out, out_off=(n_ctx, mla_cols), name="win_attn_lat")
    state = (c_kv[:n_ctx].reshape(bp, 1, sp, MLA_KV_RANK),
             k_rope_p[:n_ctx, :MLA_ROPE].reshape(bp, 1, sp, MLA_ROPE),
             wk[:n_ctx].reshape(bp, 1, sp, WIN_KV, HEAD_DIM),
             wv.reshape(bp, 1, sp, WIN_KV, HEAD_DIM))
    return out, state


def _odd_mixer(h, dims, cache, p, lam_init, layer, tables):
    (bp, sp, dec_b, dec_s, past) = dims
    n_ctx = bp * sp
    (w_in, g_dq, g_dk, lq1, lk1, lq2, lk2, g_sub, g_aq, g_ak) = p
    dk_c, dv_c, ak_c, av_c = cache
    (cos_h, sin_h), (cos_d, sin_d) = tables["head"], tables["diff"]
    o1 = DIFF_HEADS * HEAD_DIM
    o2 = o1 + DIFF_KV * HEAD_DIM
    o3 = o2 + DIFF_KV * HEAD_DIM
    o4 = o3 + AX_HEADS * HEAD_DIM
    o5 = o4 + AX_KV * HEAD_DIM
    proj = _matmul(h, w_in[layer].astype(BF16), out_dtype=F32, name="odd_in_proj")
    plan = ((0, o1, "heads", DIFF_D), (o1, o2 - o1, "heads+state", DIFF_D), (o2, o3 - o2, "cast", 0),
            (o3, o4 - o3, "heads", HEAD_DIM), (o4, o5 - o4, "heads+state", HEAD_DIM),
            (o5, AX_KV * HEAD_DIM, "cast", 0))
    dq_r, dk_r, dk, dv_b, aq_r, ak_r, ak, av_b = _prep(
        proj, plan, ((g_dq, cos_d, sin_d), (g_dk, cos_d, sin_d), (), (g_aq, cos_h, sin_h),
                     (g_ak, cos_h, sin_h), ()), "odd_prep")
    dv = proj[:n_ctx, o2:o3]
    av = proj[:n_ctx, o5:]

    lam = (jnp.exp(jnp.sum(lq1.astype(F32) * lk1.astype(F32)))
           - jnp.exp(jnp.sum(lq2.astype(F32) * lk2.astype(F32))) + lam_init)
    lam_b = jnp.broadcast_to(lam.astype(F32), (1, HEAD_DIM))
    g_sub_b = g_sub.astype(F32).reshape(1, HEAD_DIM)
    grp = DIFF_HEADS // DIFF_KV
    diff_kw = dict(nkv=DIFF_KV, groups=grp, dq=HEAD_DIM, dv=HEAD_DIM, scale=DIFF_D ** -0.5, hb=DIFF_KV,
                   lam=lam_b, g_sub=g_sub_b, diff_post_scale=1.0 - lam_init)
    n = n_ctx + dec_b * dec_s
    diff_cols = DIFF_HEADS * HEAD_DIM
    out = _attention(dq_r, 0, dk_r, dv_b, 0, batch=bp, seq=sp, tq=256,
                     out_shape=(n, diff_cols + AX_HEADS * HEAD_DIM), name="diff_attn_ctx", **diff_kw)
    out = _attention(dq_r, n_ctx, dk_r, dv_b, n_ctx, batch=dec_b, seq=dec_s, tq=128,
                     k_ctx=dk_c.reshape(dec_b * past, DIFF_KV * HEAD_DIM).astype(BF16),
                     v_ctx=dv_c.reshape(dec_b * past, DIFF_KV * HEAD_DIM).astype(BF16),
                     kc_off=0, t_ctx=past, out=out, out_off=(n_ctx, 0), name="diff_attn_lat", **diff_kw)
    agrp = AX_HEADS // AX_KV
    ax_kw = dict(nkv=AX_KV, groups=agrp, dq=HEAD_DIM, dv=HEAD_DIM, scale=HEAD_DIM ** -0.5, hb=AX_KV)
    out = _attention(aq_r, 0, ak_r, av_b, 0, batch=bp, seq=sp, tq=256,
                     out=out, out_off=(0, diff_cols), name="ax_attn_ctx", **ax_kw)
    out = _attention(aq_r, n_ctx, ak_r, av_b, n_ctx, batch=dec_b, seq=dec_s, tq=256,
                     k_ctx=ak_c.reshape(dec_b * past, AX_KV * HEAD_DIM).astype(BF16),
                     v_ctx=av_c.reshape(dec_b * past, AX_KV * HEAD_DIM).astype(BF16),
                     kc_off=0, t_ctx=past, out=out, out_off=(n_ctx, diff_cols), name="ax_attn_lat", **ax_kw)
    state = (dk[:n_ctx].reshape(bp, 1, sp, DIFF_KV, 2, DIFF_D),
             dv.reshape(bp, 1, sp, DIFF_KV, HEAD_DIM),
             ak[:n_ctx].reshape(bp, 1, sp, AX_KV, HEAD_DIM),
             av.reshape(bp, 1, sp, AX_KV, HEAD_DIM))
    return out, state


def kernel(x_prompt, x_sample, cache_mla_ckv, cache_mla_krope, cache_win_k, cache_win_v, cache_diff_k, cache_diff_v, cache_ax_k, cache_ax_v, c, c_ctx, w_mod, b_mod, g_norm_mix, g_norm_ffn, w_in_even, g_mla_q, w_mla_qb, g_mla_kv, w_mla_kvb, g_mla_qn, g_mla_kn, g_win_qn, g_win_kn, win_sink, w_out_even, w_in_odd, g_diff_qn, g_diff_kn, diff_lq1, diff_lk1, diff_lq2, diff_lk2, g_diff_sub, g_ax_qn, g_ax_kn, w_out_odd, w_router, b_router, w_exp_gate, w_exp_up, w_exp_down, w_sh_gate, w_sh_up, w_sh_down):
    bp, sp, d = x_prompt.shape
    dec_b, dec_s, _ = x_sample.shape
    depth = w_mod.shape[0]
    n_ctx = bp * sp
    n = n_ctx + dec_b * dec_s
    dims = (bp, sp, dec_b, dec_s, cache_mla_ckv.shape[2])
    group_fn = lambda tm: _group_index_fn(tm, n_ctx, dec_s)
    n_groups = 1 + dec_b

    y = jnp.concatenate([x_prompt.reshape(n_ctx, d), x_sample.reshape(dec_b * dec_s, d)], axis=0)

    cond = jnp.concatenate([c_ctx[None], c, jnp.zeros((8 - n_groups % 8, d), F32)], axis=0)
    cond = jax.nn.silu(cond)

    past = cache_mla_ckv.shape[2]
    tables = {"head": _rope_tables(HEAD_DIM, n_ctx, dec_b, dec_s, 0),
              "diff": _rope_tables(DIFF_D, n_ctx, dec_b, dec_s, 0),
              "mla": _rope_tables(MLA_ROPE, n_ctx, dec_b, dec_s, dec_b * past)}

    states_even, states_odd = [], []
    for l in range(depth):
        i = l // 2
        mod = _matmul(cond, w_mod, layer=l, out_dtype=F32, bias=b_mod.reshape(depth, 1, 6 * d),
                      name="modulation")
        mod = mod[:n_groups].reshape(n_groups, 6, 1, d)
        sh1, sc1, g1, sh2, sc2, g2 = (mod[:, j] for j in range(6))
        h = _ada_norm(y, g_norm_mix[l][None], sh1, sc1, group_fn, name="ada_norm_mix")
        if l % 2 == 0:
            pe = (w_in_even[i], g_mla_q[i], w_mla_qb[i], g_mla_kv[i], w_mla_kvb[i], g_mla_qn[i],
                  g_mla_kn[i], g_win_qn[i], g_win_kn[i], win_sink[i])
            cache = (cache_mla_ckv[:, i], cache_mla_krope[:, i], cache_win_k[:, i], cache_win_v[:, i])
            out, state = _even_mixer(h, dims, cache, pe, tables)
            states_even.append(state)
            w_out = w_out_even
        else:
            po = (w_in_odd, g_diff_qn[i], g_diff_kn[i], diff_lq1[i], diff_lk1[i], diff_lq2[i],
                  diff_lk2[i], g_diff_sub[i], g_ax_qn[i], g_ax_kn[i])
            cache = (cache_diff_k[:, i], cache_diff_v[:, i], cache_ax_k[:, i], cache_ax_v[:, i])
            lam_init = 0.8 - 0.6 * math.exp(-0.3 * l)
            out, state = _odd_mixer(h, dims, cache, po, lam_init, i, tables)
            states_odd.append(state)
            w_out = w_out_odd
        y = _matmul(out, w_out[i].astype(BF16), out_dtype=F32, resid=y, gates=g1, group_fn=group_fn,
                    name="mixer_out_proj")
        splits = [(0, n)] if l + 1 < depth else [(0, n_ctx), (n_ctx, n - n_ctx)]
        outs = _moe(y, g_norm_ffn[l][None], sh2, sc2, g2, group_fn, l, w_router, b_router,
                    w_exp_gate, w_exp_up, w_exp_down, w_sh_gate, w_sh_up, w_sh_down, splits)
        y = outs[0]

    yp = outs[0].reshape(bp, sp, d)
    ys = outs[1].reshape(dec_b, dec_s, d)
    even = tuple(jnp.concatenate([s[j] for s in states_even], axis=1) for j in range(4))
    odd = tuple(jnp.concatenate([s[j] for s in states_odd], axis=1) for j in range(4))
    return (yp, ys) + even + odd
```

```python
import functools
import math

import jax
import jax.numpy as jnp
from jax import lax
from jax.experimental import pallas as pl
from jax.experimental.pallas import tpu as pltpu

F32 = jnp.float32
BF16 = jnp.bfloat16

GRID_W = 64
ROPE_THETA = 10000.0
EPS = 1e-6
NEG_INF = -1e30
LOG2E = math.log2(math.e)
HEAD_DIM = 128
MLA_HEADS = 16
MLA_Q_RANK = 768
MLA_KV_RANK = 512
MLA_NOPE = 128
MLA_ROPE = 64
MLA_V = 128
MLA_QK = MLA_NOPE + MLA_ROPE
MLA_QK_PAD = 256
WIN_HEADS = 16
WIN_KV = 4
WINDOW = 128
DIFF_HEADS = 16
DIFF_KV = 4
DIFF_D = HEAD_DIM // 2
AX_HEADS = 16
AX_KV = 4
N_EXPERTS = 64
N_EXPERT_GROUPS = 8
TOPK_GROUPS = 4
TOP_K = 8
ROUTED_SCALE = 2.5

V7X_LANES = 128
V7X_VMEM_LIMIT_BYTES = 56 * 1024 * 1024
MM_WEIGHT_TILE_BYTES = 32 * 1024 * 1024
MOE_ROW_TILE = 256
MOE_CHUNKS = 8
COMBINE_CHUNK_ROWS = 2048


def _cparams(sem):
    return pltpu.CompilerParams(dimension_semantics=sem, vmem_limit_bytes=V7X_VMEM_LIMIT_BYTES)


def _largest_tile(n, cap, step):
    if n <= cap:
        return n
    t = (cap // step) * step
    while t >= step:
        if n % t == 0:
            return t
        t -= step
    raise ValueError(f"no tile for {n} under {cap}")


def _group_index_fn(tm, n_ctx_rows, dec_rows):
    assert n_ctx_rows % tm == 0 and dec_rows % tm == 0
    ctx_tiles = n_ctx_rows // tm
    per_dec = dec_rows // tm

    def fn(i):
        return jnp.where(i < ctx_tiles, 0, 1 + (jnp.maximum(i - ctx_tiles, 0)) // per_dec)

    return fn


def _mm_body(*refs, has_bias, has_resid):
    x_ref, w_ref = refs[0], refs[1]
    idx = 2
    if has_bias:
        bias_ref = refs[idx]
        idx += 1
    if has_resid:
        resid_ref, gate_ref = refs[idx], refs[idx + 1]
        idx += 2
    o_ref = refs[idx]
    if w_ref.dtype == BF16:
        wb_ref = w_ref
    else:
        wb_ref = refs[idx + 1]

        @pl.when(pl.program_id(1) == 0)
        def _():
            wb_ref[...] = w_ref[...].astype(BF16)

    acc = jnp.dot(x_ref[...].astype(BF16), wb_ref[...], preferred_element_type=F32)
    if has_bias:
        acc = acc + bias_ref[...]
    if has_resid:
        acc = resid_ref[...] + gate_ref[...] * acc
    o_ref[...] = acc.astype(o_ref.dtype)


def _matmul(x, w, *, out_dtype, layer=None, bias=None, resid=None, gates=None, group_fn=None, name):
    m, k = x.shape
    n = w.shape[-1]
    tm = _largest_tile(m, 512, 8)
    precast = w.dtype == BF16
    col_bytes = k * (4 if precast else 10)
    budget = MM_WEIGHT_TILE_BYTES // 2 if precast else MM_WEIGHT_TILE_BYTES
    tn_cap = max(V7X_LANES, min(2048, budget // col_bytes))
    tn = _largest_tile(n, tn_cap, V7X_LANES)
    grid = (n // tn, m // tm)
    if layer is None:
        w_spec = pl.BlockSpec((k, tn), lambda j, i: (0, j))
    else:
        w_spec = pl.BlockSpec((None, k, tn), lambda j, i: (layer, 0, j))
    in_specs = [pl.BlockSpec((tm, k), lambda j, i: (i, 0)), w_spec]
    args = [x, w]
    if bias is not None:
        in_specs.append(pl.BlockSpec((None, 1, tn), lambda j, i: (layer, 0, j)))
        args.append(bias)
    if resid is not None:
        gfn = group_fn(tm)
        in_specs.append(pl.BlockSpec((tm, tn), lambda j, i: (i, j)))
        in_specs.append(pl.BlockSpec((None, 1, tn), lambda j, i: (gfn(i), 0, j)))
        args += [resid, gates]
    return pl.pallas_call(
        functools.partial(_mm_body, has_bias=bias is not None, has_resid=resid is not None),
        out_shape=jax.ShapeDtypeStruct((m, n), out_dtype),
        grid=grid,
        in_specs=in_specs,
        out_specs=pl.BlockSpec((tm, tn), lambda j, i: (i, j)),
        scratch_shapes=[] if precast else [pltpu.VMEM((k, tn), BF16)],
        compiler_params=_cparams(("arbitrary", "arbitrary")),
        name=name,
    )(*args)


def _first_max(cur, sub, limit):
    m = jnp.max(cur, axis=0, keepdims=True)
    first = jnp.min(jnp.where(cur == m, sub, limit), axis=0, keepdims=True)
    return m, first, sub == first


def _route_columns(logits, bias, tri, carry):
    e, t = logits.shape
    per = e // N_EXPERT_GROUPS
    scores = jax.nn.sigmoid(logits)
    biased = scores + bias
    neg = -jnp.inf
    gsub = lax.broadcasted_iota(jnp.int32, (per, t), 0)
    grp = []
    for g in range(N_EXPERT_GROUPS):
        xg = biased[g * per:(g + 1) * per]
        m1, _, hit = _first_max(xg, gsub, per)
        m2 = jnp.max(jnp.where(hit, neg, xg), axis=0, keepdims=True)
        grp.append(m1 + m2)
    grp = jnp.concatenate(grp, axis=0)
    nsub = lax.broadcasted_iota(jnp.int32, (N_EXPERT_GROUPS, t), 0)
    gsel = jnp.zeros((N_EXPERT_GROUPS, t), F32)
    for _ in range(TOPK_GROUPS):
        _, _, hit = _first_max(grp, nsub, N_EXPERT_GROUPS)
        gsel = jnp.where(hit, 1.0, gsel)
        grp = jnp.where(hit, neg, grp)
    emask = jnp.concatenate([jnp.broadcast_to(gsel[g:g + 1], (per, t)) for g in range(N_EXPERT_GROUPS)], axis=0)
    cur = jnp.where(emask > 0.5, biased, neg)
    esub = lax.broadcasted_iota(jnp.int32, (e, t), 0)
    sel = jnp.zeros((e, t), F32)
    ids, ws, hits = [], [], []
    for _ in range(TOP_K):
        _, first, hit = _first_max(cur, esub, e)
        ids.append(first)
        ws.append(jnp.sum(jnp.where(hit, scores, 0.0), axis=0, keepdims=True))
        hits.append(hit)
        sel = jnp.where(hit, 1.0, sel)
        cur = jnp.where(hit, neg, cur)
    w = jnp.concatenate(ws, axis=0)
    gates = w / jnp.sum(w, axis=0, keepdims=True) * ROUTED_SCALE
    rank_all = jnp.dot(sel.astype(BF16), tri, preferred_element_type=F32) + carry
    ranks = [jnp.sum(jnp.where(hit, rank_all, 0.0), axis=0, keepdims=True) for hit in hits]
    counts = jnp.sum(sel, axis=1, keepdims=True)
    return (jnp.concatenate(ids, axis=0), gates, jnp.concatenate(ranks, axis=0).astype(jnp.int32), counts,
            rank_all + sel)


def _adanorm_body(x_ref, g_ref, shift_ref, scale_ref, *rest, has_router):
    x = x_ref[...]
    y = x * lax.rsqrt(jnp.mean(x * x, axis=-1, keepdims=True) + EPS) * g_ref[...]
    t = y * (1.0 + scale_ref[...]) + shift_ref[...]
    if has_router:
        (whi_ref, wlo_ref, br_ref, tri_ref, o_ref, ids_ref, gate_ref, rank_ref, cnt_ref, cum_ref,
         carry_ref) = rest

        @pl.when(pl.program_id(0) == 0)
        def _():
            carry_ref[...] = jnp.zeros_like(carry_ref)

        t_hi = t.astype(BF16)
        t_lo = (t - t_hi.astype(F32)).astype(BF16)
        nt = lambda a, b: lax.dot_general(a, b, (((1,), (1,)), ((), ())), preferred_element_type=F32)
        logits = nt(whi_ref[...], t_hi) + (nt(wlo_ref[...], t_hi) + nt(whi_ref[...], t_lo))
        ids, gates, ranks, counts, cum = _route_columns(logits, br_ref[...], tri_ref[...], carry_ref[:, :1])
        cum_ref[...] = cum
        ids_ref[...] = ids
        gate_ref[...] = gates
        rank_ref[...] = ranks
        carry_ref[...] = carry_ref[...] + counts
        cnt_ref[...] = carry_ref[...]
    else:
        (o_ref,) = rest
    o_ref[...] = t.astype(o_ref.dtype)


def _ada_norm(x, g, shift, scale, group_fn, *, w_router=None, b_router=None, name):
    n, d = x.shape
    tm = 256
    gfn = group_fn(tm)
    in_specs = [pl.BlockSpec((tm, d), lambda i: (i, 0)),
                pl.BlockSpec((1, d), lambda i: (0, 0)),
                pl.BlockSpec((None, 1, d), lambda i: (gfn(i), 0, 0)),
                pl.BlockSpec((None, 1, d), lambda i: (gfn(i), 0, 0))]
    args = [x, g, shift, scale]
    out_shape = jax.ShapeDtypeStruct((n, d), BF16)
    out_specs = pl.BlockSpec((tm, d), lambda i: (i, 0))
    scratch = []
    if w_router is not None:
        e = w_router.shape[1]
        tri = (lax.broadcasted_iota(jnp.int32, (tm, tm), 0) < lax.broadcasted_iota(jnp.int32, (tm, tm), 1))
        in_specs += [pl.BlockSpec((e, d), lambda i: (0, 0)),
                     pl.BlockSpec((e, d), lambda i: (0, 0)),
                     pl.BlockSpec((e, tm), lambda i: (0, 0)),
                     pl.BlockSpec((tm, tm), lambda i: (0, 0))]
        w_t = w_router.T.astype(F32)
        w_hi = w_t.astype(BF16)
        w_lo = (w_t - w_hi.astype(F32)).astype(BF16)
        args += [w_hi, w_lo, jnp.broadcast_to(b_router.astype(F32)[:, None], (e, tm)), tri.astype(BF16)]
        col = lambda rows, dt: (jax.ShapeDtypeStruct((rows, n), dt), pl.BlockSpec((rows, tm), lambda i: (0, i)))
        extra = [col(TOP_K, jnp.int32), col(TOP_K, F32), col(TOP_K, jnp.int32),
                 (jax.ShapeDtypeStruct((e, V7X_LANES), F32), pl.BlockSpec((e, V7X_LANES), lambda i: (0, 0))),
                 col(e, F32)]
        out_shape = (out_shape,) + tuple(s for s, _ in extra)
        out_specs = (out_specs,) + tuple(b for _, b in extra)
        scratch = [pltpu.VMEM((e, V7X_LANES), F32)]
    return pl.pallas_call(
        functools.partial(_adanorm_body, has_router=w_router is not None),
        out_shape=out_shape,
        grid=(n // tm,),
        in_specs=in_specs,
        out_specs=out_specs,
        scratch_shapes=scratch,
        compiler_params=_cparams(("arbitrary",)),
        name=name,
    )(*args)


def _attn_body(*refs, hb, groups, dq, dv, tq, tk, scale, t_new, t_ctx, window, has_sink,
               diff_post_scale, has_out_buf):
    it = iter(refs)
    q_ref, kn_ref, vn_ref = next(it), next(it), next(it)
    kc_ref = vc_ref = sink_ref = lam_ref = gsub_ref = None
    if t_ctx:
        kc_ref, vc_ref = next(it), next(it)
    if has_sink:
        sink_ref = next(it)
    diff = diff_post_scale is not None
    if diff:
        lam_ref, gsub_ref = next(it), next(it)
    if has_out_buf:
        next(it)
    o_ref = next(it)

    h = pl.program_id(1)
    i = pl.program_id(2)
    nstack = 2 * groups if diff else groups
    rows = nstack * tq

    def stack_rows(parts):
        return parts[0] if len(parts) == 1 else jnp.concatenate(parts, axis=0)

    if window is not None:
        wk = min(t_new, tq + 2 * window)
        if wk == t_new:
            wstart = 0
        else:
            wstart = pl.multiple_of(jnp.clip(i * tq - window, 0, t_new - wk), V7X_LANES)
        qi = i * tq + lax.broadcasted_iota(jnp.int32, (tq, wk), 0)
        kj = wstart + lax.broadcasted_iota(jnp.int32, (tq, wk), 1)
        wmask = stack_rows([jnp.abs(qi - kj) <= window] * nstack)

    for j in range(hb):
        qj = q_ref[:, j * groups * dq:(j + 1) * groups * dq]
        parts = [qj[:, g * dq:(g + 1) * dq] for g in range(groups)]
        if diff:
            lo = lax.broadcasted_iota(jnp.int32, (tq, dq), 1) < dq // 2
            zero = jnp.zeros((tq, dq), qj.dtype)
            parts = [jnp.where(lo, p, zero) for p in parts] + [jnp.where(lo, zero, p) for p in parts]
        qs = stack_rows(parts)

        if has_sink:
            sinks = []
            for g in range(groups):
                hh = (h * hb + j) * groups + g
                sinks.append(jnp.broadcast_to(sink_ref[pl.ds(hh, 1), :][:, :1], (tq, 1)))
            m = stack_rows(sinks) * LOG2E
            l = jnp.ones((rows, 1), F32)
        else:
            m = jnp.full((rows, 1), NEG_INF, F32)
            l = jnp.zeros((rows, 1), F32)
        acc = jnp.zeros((rows, dv), F32)

        def step(carry, kc, vc, mask):
            m, l, acc = carry
            s = lax.dot_general(qs, kc, (((1,), (1,)), ((), ())), preferred_element_type=F32) * (scale * LOG2E)
            if mask is not None:
                s = jnp.where(mask, s, NEG_INF)
            m_new = jnp.maximum(m, jnp.max(s, axis=-1, keepdims=True))
            alpha = jnp.exp2(m - m_new)
            p = jnp.exp2(s - m_new)
            l = alpha * l + jnp.sum(p, axis=-1, keepdims=True)
            acc = alpha * acc + jnp.dot(p.astype(BF16), vc, preferred_element_type=F32)
            return m_new, l, acc

        carry = (m, l, acc)
        kcols = slice(j * dq, (j + 1) * dq)
        vcols = slice(j * dv, (j + 1) * dv)
        if window is not None:
            carry = step(carry, kn_ref[pl.ds(wstart, wk), kcols], vn_ref[pl.ds(wstart, wk), vcols], wmask)
        else:
            for c in range(t_new // tk):
                carry = step(carry, kn_ref[c * tk:(c + 1) * tk, kcols], vn_ref[c * tk:(c + 1) * tk, vcols], None)
        if t_ctx:
            tkc = min(tk, t_ctx)
            for c in range(t_ctx // tkc):
                carry = step(carry, kc_ref[c * tkc:(c + 1) * tkc, kcols], vc_ref[c * tkc:(c + 1) * tkc, vcols], None)
        m, l, acc = carry
        o = acc / l
        if diff:
            half = groups * tq
            d = o[:half] - lam_ref[...] * o[half:]
            o = (d * lax.rsqrt(jnp.mean(d * d, axis=-1, keepdims=True) + EPS) * gsub_ref[...]) * diff_post_scale
        outs = [o[g * tq:(g + 1) * tq] for g in range(groups)]
        oj = outs[0] if groups == 1 else jnp.concatenate(outs, axis=1)
        o_ref[:, j * groups * dv:(j + 1) * groups * dv] = oj.astype(o_ref.dtype)


def _attention(q, q_off, k_new, v_new, kn_off, *, batch, seq, nkv, groups, dq, dv, scale, hb, tq,
               k_ctx=None, v_ctx=None, kc_off=0, t_ctx=0, sink=None, window=None,
               lam=None, g_sub=None, diff_post_scale=None, out=None, out_shape=None, out_off=(0, 0), name):
    t_new = seq
    tq = min(tq, seq)
    tk = min(512, t_new)
    assert seq % tq == 0 and t_new % tk == 0 and nkv % hb == 0
    assert q_off % tq == 0 and kn_off % t_new == 0
    qb, nb = q_off // tq, kn_off // t_new
    spt = seq // tq
    in_specs = [pl.BlockSpec((tq, hb * groups * dq), lambda b, h, i: (qb + b * spt + i, h)),
                pl.BlockSpec((t_new, hb * dq), lambda b, h, i: (nb + b, h)),
                pl.BlockSpec((t_new, hb * dv), lambda b, h, i: (nb + b, h))]
    args = [q, k_new, v_new]
    if t_ctx:
        assert kc_off % t_ctx == 0
        cb = kc_off // t_ctx
        in_specs += [pl.BlockSpec((t_ctx, hb * dq), lambda b, h, i: (cb + b, h)),
                     pl.BlockSpec((t_ctx, hb * dv), lambda b, h, i: (cb + b, h))]
        args += [k_ctx, v_ctx]
    if sink is not None:
        in_specs.append(pl.BlockSpec(sink.shape, lambda b, h, i: (0, 0)))
        args.append(sink)
    if diff_post_scale is not None:
        in_specs += [pl.BlockSpec((1, dv), lambda b, h, i: (0, 0))] * 2
        args += [lam, g_sub]
    aliases = {}
    if out is not None:
        out_shape = out.shape
        in_specs.append(pl.BlockSpec(memory_space=pl.ANY))
        args.append(out)
        aliases = {len(args) - 1: 0}
    wblk = hb * groups * dv
    assert out_off[0] % tq == 0 and out_off[1] % wblk == 0
    ob, oc = out_off[0] // tq, out_off[1] // wblk
    body = functools.partial(
        _attn_body, hb=hb, groups=groups, dq=dq, dv=dv, tq=tq, tk=tk, scale=scale, t_new=t_new,
        t_ctx=t_ctx, window=window, has_sink=sink is not None, diff_post_scale=diff_post_scale,
        has_out_buf=out is not None)
    return pl.pallas_call(
        body,
        out_shape=jax.ShapeDtypeStruct(out_shape, BF16),
        grid=(batch, nkv // hb, spt),
        in_specs=in_specs,
        out_specs=pl.BlockSpec((tq, wblk), lambda b, h, i: (ob + b * spt + i, oc + h)),
        input_output_aliases=aliases,
        compiler_params=_cparams(("arbitrary", "arbitrary", "arbitrary")),
        name=name,
    )(*args)


def _experts_body(te_ref, nv_ref, x_ref, wg_ref, wu_ref, wd_ref, *rest, tile0):
    o_ref, wgb, wub, wdb = rest[-4:]
    t = pl.program_id(0)
    g = tile0 + t
    valid = g < nv_ref[0]
    prev = te_ref[jnp.maximum(g - 1, 0)]
    first = jnp.logical_or(t == 0, te_ref[g] != prev)

    @pl.when(jnp.logical_and(first, valid))
    def _():
        wgb[...] = wg_ref[...].astype(BF16)
        wub[...] = wu_ref[...].astype(BF16)
        wdb[...] = wd_ref[...].astype(BF16)

    @pl.when(valid)
    def _():
        x = x_ref[...]
        a = jnp.dot(x, wgb[...], preferred_element_type=F32)
        u = jnp.dot(x, wub[...], preferred_element_type=F32)
        hcur = a * jax.nn.sigmoid(a) * u
        o_ref[...] = jnp.dot(hcur.astype(BF16), wdb[...], preferred_element_type=F32).astype(o_ref.dtype)

    @pl.when(jnp.logical_not(valid))
    def _():
        o_ref[...] = jnp.zeros_like(o_ref)


def _routed_experts(xs, tile_expert, n_valid, w_gate, w_up, w_down, layer, tile0, n_tiles, rows_buf):
    r, d = xs.shape
    ff = w_gate.shape[-1]
    tm = MOE_ROW_TILE
    chunk_tiles = r // tm

    def x_blk(t, te, nv):
        last = jnp.clip(nv[0] - tile0 - 1, 0, chunk_tiles - 1)
        return (jnp.minimum(t, last), 0)

    def o_blk(t, te, nv):
        return (jnp.where(tile0 + t < nv[0], tile0 + t, n_tiles), 0)

    w_blk = lambda t, te, nv: (layer, te[tile0 + t], 0, 0)
    in_specs = [pl.BlockSpec((tm, d), x_blk),
                pl.BlockSpec((None, None, d, ff), w_blk),
                pl.BlockSpec((None, None, d, ff), w_blk),
                pl.BlockSpec((None, None, ff, d), w_blk)]
    args = [tile_expert, n_valid, xs, w_gate, w_up, w_down]
    aliases = {}
    if rows_buf is not None:
        in_specs.append(pl.BlockSpec(memory_space=pl.ANY))
        args.append(rows_buf)
        aliases = {len(args) - 1: 0}
    grid_spec = pltpu.PrefetchScalarGridSpec(
        num_scalar_prefetch=2,
        grid=(chunk_tiles,),
        in_specs=in_specs,
        out_specs=pl.BlockSpec((tm, d), o_blk),
        scratch_shapes=[pltpu.VMEM((d, ff), BF16), pltpu.VMEM((d, ff), BF16), pltpu.VMEM((ff, d), BF16)],
    )
    return pl.pallas_call(
        functools.partial(_experts_body, tile0=tile0),
        out_shape=jax.ShapeDtypeStruct(((n_tiles + 1) * tm, d), BF16),
        grid_spec=grid_spec,
        input_output_aliases=aliases,
        compiler_params=_cparams(("arbitrary",)),
        name="routed_experts",
    )(*args)


def _shared_body(t_ref, wg_ref, wu_ref, wd_ref, resid_ref, gate_ref, o_ref):
    x = t_ref[...]
    a = jnp.dot(x, wg_ref[...], preferred_element_type=F32)
    u = jnp.dot(x, wu_ref[...], preferred_element_type=F32)
    hcur = (a * jax.nn.sigmoid(a) * u).astype(BF16)
    shared = jnp.dot(hcur, wd_ref[...], preferred_element_type=F32)
    o_ref[...] = resid_ref[...] + gate_ref[...] * shared


def _shared_expert(t, ws_gate, ws_up, ws_down, resid, gates, group_fn):
    n, d = t.shape
    ff = ws_gate.shape[1]
    tm = 256
    gfn = group_fn(tm)
    return pl.pallas_call(
        _shared_body,
        out_shape=jax.ShapeDtypeStruct((n, d), F32),
        grid=(n // tm,),
        in_specs=[pl.BlockSpec((tm, d), lambda i: (i, 0)),
                  pl.BlockSpec((d, ff), lambda i: (0, 0)),
                  pl.BlockSpec((d, ff), lambda i: (0, 0)),
                  pl.BlockSpec((ff, d), lambda i: (0, 0)),
                  pl.BlockSpec((tm, d), lambda i: (i, 0)),
                  pl.BlockSpec((None, 1, d), lambda i: (gfn(i), 0, 0))],
        out_specs=pl.BlockSpec((tm, d), lambda i: (i, 0)),
        compiler_params=_cparams(("arbitrary",)),
        name="shared_expert",
    )(t, ws_gate, ws_up, ws_down, resid, gates)


def _combine_body(rows_ref, w_ref, base_ref, gate_ref, *rest):
    o_ref = rest[-1]
    w = w_ref[...]
    acc = w[:, 0:1] * rows_ref[0].astype(F32)
    for c in range(1, rows_ref.shape[0]):
        acc = acc + w[:, c:c + 1] * rows_ref[c].astype(F32)
    o_ref[...] = base_ref[...] + gate_ref[...] * acc


def _combine(picked, weights, base, gates, group_fn, row_start, out, out_rows, out_off):
    d = base.shape[1]
    k, count, _ = picked.shape
    tm = 128
    assert row_start % tm == 0 and count % tm == 0 and out_off % tm == 0
    r0, o0 = row_start // tm, out_off // tm
    gfn = group_fn(tm)
    in_specs = [pl.BlockSpec((k, tm, d), lambda i: (0, i, 0)),
                pl.BlockSpec((tm, k), lambda i: (r0 + i, 0)),
                pl.BlockSpec((tm, d), lambda i: (r0 + i, 0)),
                pl.BlockSpec((None, 1, d), lambda i: (gfn(r0 + i), 0, 0))]
    args = [picked, weights, base, gates]
    aliases = {}
    if out is not None:
        in_specs.append(pl.BlockSpec(memory_space=pl.ANY))
        args.append(out)
        aliases = {len(args) - 1: 0}
    return pl.pallas_call(
        _combine_body,
        out_shape=jax.ShapeDtypeStruct((out_rows, d), F32),
        grid=(count // tm,),
        in_specs=in_specs,
        out_specs=pl.BlockSpec((tm, d), lambda i: (o0 + i, 0)),
        input_output_aliases=aliases,
        compiler_params=_cparams(("arbitrary",)),
        name="moe_combine",
    )(*args)


def _plan_body(rs_ref, ids_ref, rank_ref, pos_ref):
    ids = ids_ref[...]
    base = jnp.zeros_like(ids)
    for e in range(N_EXPERTS):
        base = jnp.where(ids == e, rs_ref[e], base)
    pos_ref[...] = base + rank_ref[...]


def _row_token_body(te_ref, j0_ref, blo_ref, bhi_ref, cum_ref, o_ref, *, tm, n_tok):
    sub = lax.broadcasted_iota(jnp.int32, (tm, 1), 0)
    for s in range(o_ref.shape[0]):
        t = pl.program_id(0) * o_ref.shape[0] + s
        e = te_ref[t]
        j = (j0_ref[t] + sub).astype(F32)

        def body(b, acc):
            c = cum_ref[e, pl.ds(b, 1), :]
            return acc + jnp.where(c <= j, 1.0, 0.0)

        acc = lax.fori_loop(blo_ref[t], bhi_ref[t], body, jnp.zeros((tm, V7X_LANES), F32))
        cnt = (blo_ref[t] * V7X_LANES).astype(F32) + jnp.sum(acc, axis=1, keepdims=True)
        spare = lax.rem(t * tm, n_tok) + sub
        spare = jnp.where(spare >= n_tok, spare - n_tok, spare).astype(F32)
        tok = jnp.where(cnt >= n_tok, spare, cnt)
        o_ref[s] = jnp.broadcast_to(tok, (tm, V7X_LANES)).T[0:1, :].astype(jnp.int32)


def _dispatch_plan(ids, ranks, counts, cum, tm):
    k, n = ids.shape
    e = counts.shape[0]
    n_rows = n * k + e * tm
    n_tiles = n_rows // tm
    tiles_e = (counts + tm - 1) // tm
    tile_end = jnp.cumsum(tiles_e)
    row_start = ((tile_end - tiles_e) * tm).astype(jnp.int32)
    tile_ids = jnp.arange(n_tiles, dtype=jnp.int32)
    tile_expert = jnp.sum((tile_end[None, :] <= tile_ids[:, None]).astype(jnp.int32), axis=1)
    tile_expert = jnp.minimum(tile_expert, e - 1)

    assert n % V7X_LANES == 0 and tm <= n
    nb = n // V7X_LANES
    cum_b = cum.reshape(e, nb, V7X_LANES)
    j0 = tile_ids * tm - row_start[tile_expert]
    blk_last = cum_b[:, :, -1][tile_expert]
    blk_first = cum_b[:, :, 0][tile_expert]
    blo = jnp.sum((blk_last <= j0[:, None].astype(F32)).astype(jnp.int32), axis=1)
    bhi = jnp.sum((blk_first <= (j0[:, None] + (tm - 1)).astype(F32)).astype(jnp.int32), axis=1)
    row_token = pl.pallas_call(
        functools.partial(_row_token_body, tm=tm, n_tok=n),
        out_shape=jax.ShapeDtypeStruct((n_tiles, 1, tm), jnp.int32),
        grid_spec=pltpu.PrefetchScalarGridSpec(
            num_scalar_prefetch=4,
            grid=(n_tiles // MOE_CHUNKS,),
            in_specs=[pl.BlockSpec((e, nb, V7X_LANES), lambda t, *_: (0, 0, 0))],
            out_specs=pl.BlockSpec((MOE_CHUNKS, 1, tm), lambda t, *_: (t, 0, 0))),
        compiler_params=_cparams(("arbitrary",)),
        name="row_tokens",
    )(tile_expert, j0, blo, jnp.maximum(bhi, blo), cum_b).reshape(n_rows)

    tn = _largest_tile(n, 2048, V7X_LANES)
    pos = pl.pallas_call(
        _plan_body,
        out_shape=jax.ShapeDtypeStruct((k, n), jnp.int32),
        grid_spec=pltpu.PrefetchScalarGridSpec(
            num_scalar_prefetch=1,
            grid=(n // tn,),
            in_specs=[pl.BlockSpec((k, tn), lambda i, rs: (0, i))] * 2,
            out_specs=pl.BlockSpec((k, tn), lambda i, rs: (0, i))),
        compiler_params=_cparams(("arbitrary",)),
        name="dispatch_rows",
    )(row_start, ids, ranks)
    return pos, row_token, tile_expert, tile_end[-1:].astype(jnp.int32)


def _moe(y, g_ffn, shift, scale, gates, group_fn, layer, w_router, b_router, w_gate, w_up, w_down,
         ws_gate, ws_up, ws_down, row_splits):
    n, d = y.shape
    t, ids, gate_w, ranks, counts, cum = _ada_norm(y, g_ffn, shift, scale, group_fn, w_router=w_router[layer],
                                                   b_router=b_router[layer], name="ada_norm_route")
    pos, row_token, tile_expert, n_valid = _dispatch_plan(
        ids, ranks, counts[:, 0].astype(jnp.int32), cum, MOE_ROW_TILE)
    base = _shared_expert(t, ws_gate[layer].astype(BF16), ws_up[layer].astype(BF16),
                          ws_down[layer].astype(BF16), y, gates, group_fn)
    n_tiles = tile_expert.shape[0]
    chunk_tiles = n_tiles // MOE_CHUNKS
    chunk_rows = chunk_tiles * MOE_ROW_TILE
    rows = None
    for c in range(MOE_CHUNKS):
        xs = t.at[row_token[c * chunk_rows:(c + 1) * chunk_rows]].get(mode="promise_in_bounds")
        rows = _routed_experts(xs, tile_expert, n_valid, w_gate, w_up, w_down, layer,
                               c * chunk_tiles, n_tiles, rows)
    weights = gate_w.T
    outs = []
    for (r0, cnt) in row_splits:
        step = min(COMBINE_CHUNK_ROWS, cnt)
        assert cnt % step == 0
        buf = None
        for a in range(0, cnt, step):
            picks = pos[:, r0 + a:r0 + a + step].reshape(-1)
            picked = rows.at[picks].get(mode="promise_in_bounds").reshape(TOP_K, step, d)
            buf = _combine(picked, weights, base, gates, group_fn, r0 + a, buf, cnt, a)
        outs.append(buf)
    return outs


def _rope_2d(n_tok, rot_dim):
    rows = n_tok // GRID_W
    row = jnp.broadcast_to(jnp.arange(rows, dtype=F32)[:, None], (rows, GRID_W)).reshape(-1)
    col = jnp.broadcast_to(jnp.arange(GRID_W, dtype=F32)[None, :], (rows, GRID_W)).reshape(-1)
    n_freq = rot_dim // 4
    inv = ROPE_THETA ** (-jnp.arange(n_freq, dtype=F32) / n_freq)
    ang = jnp.concatenate([row[:, None] * inv, col[:, None] * inv], axis=-1)
    return jnp.cos(ang), jnp.sin(ang)


def _rope_tables(rot_dim, n_ctx, dec_b, dec_s, n_tail):
    c, s = _rope_2d(dec_s, rot_dim)
    reps = V7X_LANES // rot_dim
    cos = jnp.concatenate([c, c] * reps, axis=1)
    sin = jnp.concatenate([-s, s] * reps, axis=1)
    if n_tail:
        fill = V7X_LANES - rot_dim
        cos = jnp.concatenate([c, c, jnp.ones((dec_s, fill), F32)], axis=1)
        sin = jnp.concatenate([-s, s, jnp.zeros((dec_s, fill), F32)], axis=1)
    ones = lambda r: jnp.ones((r, V7X_LANES), F32)
    zeros = lambda r: jnp.zeros((r, V7X_LANES), F32)
    cos = jnp.concatenate([ones(n_ctx)] + [cos] * dec_b + [ones(n_tail)], axis=0)
    sin = jnp.concatenate([zeros(n_ctx)] + [sin] * dec_b + [zeros(n_tail)], axis=0)
    return cos, sin


def _rotate_half(y, seg):
    half = seg // 2
    if seg == V7X_LANES:
        return pltpu.roll(y, half, axis=1)
    lane = lax.broadcasted_iota(jnp.int32, y.shape, 1)
    return jnp.where(lane % seg < half, pltpu.roll(y, V7X_LANES - half, axis=1), pltpu.roll(y, half, axis=1))


def _rotate_half_tail(t):
    half = MLA_ROPE // 2
    lane = lax.broadcasted_iota(jnp.int32, t.shape, 1)
    return jnp.where(lane < half, pltpu.roll(t, V7X_LANES - half, axis=1), pltpu.roll(t, half, axis=1))


def _segment_ones(seg):
    shift = seg.bit_length() - 1
    assert 1 << shift == seg
    r = lax.shift_right_logical(lax.broadcasted_iota(jnp.int32, (V7X_LANES, V7X_LANES), 0), shift)
    c = lax.shift_right_logical(lax.broadcasted_iota(jnp.int32, (V7X_LANES, V7X_LANES), 1), shift)
    return jnp.where(r == c, 1.0, 0.0).astype(BF16)


def _segment_sums(v, ones):
    hi = v.astype(BF16)
    lo = (v - hi.astype(F32)).astype(BF16)
    return jnp.dot(hi, ones, preferred_element_type=F32) + jnp.dot(lo, ones, preferred_element_type=F32)


def _seg_rms(x, seg, ones):
    return x * lax.rsqrt(_segment_sums(x * x, ones) / seg + EPS)


def _prep_body(proj_ref, *refs, plan):
    it = iter(refs)
    inputs = []
    for (_, _, kind, _) in plan:
        if kind in ("norm", "norm_f32"):
            inputs.append((next(it),))
        elif kind.startswith("heads"):
            inputs.append((next(it), next(it), next(it)))
        else:
            inputs.append(())
    outs = list(it)
    oi = 0
    for (col, width, kind, seg), ins in zip(plan, inputs):
        if kind in ("norm", "norm_f32"):
            (g_ref,) = ins
            x = proj_ref[:, col:col + width]
            y = x * lax.rsqrt(jnp.mean(x * x, axis=-1, keepdims=True) + EPS) * g_ref[...]
            outs[oi][...] = y.astype(outs[oi].dtype)
            oi += 1
        elif kind.startswith("heads"):
            g_ref, cos_ref, sin_ref = ins
            with_state = kind.endswith("+state")
            cos, sin, g = cos_ref[...], sin_ref[...], g_ref[...]
            ones = _segment_ones(seg)
            for c0 in range(0, width, V7X_LANES):
                y = _seg_rms(proj_ref[:, col + c0:col + c0 + V7X_LANES], seg, ones) * g
                if with_state:
                    outs[oi + 1][:, c0:c0 + V7X_LANES] = y
                y = y * cos + _rotate_half(y, seg) * sin
                outs[oi][:, c0:c0 + V7X_LANES] = y.astype(BF16)
            oi += 2 if with_state else 1
        else:
            outs[oi][...] = proj_ref[:, col:col + width].astype(BF16)
            oi += 1


def _prep(proj, plan, params, name):
    n, width_all = proj.shape
    tm = 256
    in_specs = [pl.BlockSpec((tm, width_all), lambda i: (i, 0))]
    args = [proj]
    out_shape, out_specs = [], []
    for (col, width, kind, seg), ps in zip(plan, params):
        row_spec = pl.BlockSpec((tm, width), lambda i: (i, 0))
        if kind in ("norm", "norm_f32"):
            in_specs.append(pl.BlockSpec((1, width), lambda i: (0, 0)))
            args.append(ps[0].astype(F32).reshape(1, width))
            out_shape.append(jax.ShapeDtypeStruct((n, width), F32 if kind == "norm_f32" else BF16))
            out_specs.append(row_spec)
        elif kind.startswith("heads"):
            g, cos, sin = ps
            in_specs += [pl.BlockSpec((1, V7X_LANES), lambda i: (0, 0)),
                         pl.BlockSpec((tm, V7X_LANES), lambda i: (i, 0)),
                         pl.BlockSpec((tm, V7X_LANES), lambda i: (i, 0))]
            args += [jnp.tile(g.astype(F32), V7X_LANES // seg).reshape(1, V7X_LANES), cos, sin]
            out_shape.append(jax.ShapeDtypeStruct((n, width), BF16))
            out_specs.append(row_spec)
            if kind.endswith("+state"):
                out_shape.append(jax.ShapeDtypeStruct((n, width), F32))
                out_specs.append(row_spec)
        else:
            out_shape.append(jax.ShapeDtypeStruct((n, width), BF16))
            out_specs.append(row_spec)
    return pl.pallas_call(
        functools.partial(_prep_body, plan=plan),
        out_shape=tuple(out_shape),
        grid=(n // tm,),
        in_specs=in_specs,
        out_specs=tuple(out_specs),
        compiler_params=_cparams(("arbitrary",)),
        name=name,
    )(*args)


def _mla_q_body(x_ref, w_ref, g_ref, cos_ref, sin_ref, o_ref):
    acc = jnp.dot(x_ref[...], w_ref[...], preferred_element_type=F32)
    cos, sin = cos_ref[...], sin_ref[...]
    ones = _segment_ones(V7X_LANES)
    for h in range(MLA_HEADS):
        c0 = h * MLA_QK_PAD
        nope = acc[:, c0:c0 + MLA_NOPE]
        tail = acc[:, c0 + MLA_NOPE:c0 + MLA_QK_PAD]
        ss = _segment_sums(nope * nope, ones) + _segment_sums(tail * tail, ones)
        r = lax.rsqrt(ss / MLA_QK + EPS)
        o_ref[:, c0:c0 + MLA_NOPE] = (nope * r * g_ref[:, :MLA_NOPE]).astype(BF16)
        t = tail * r * g_ref[:, MLA_NOPE:]
        t = t * cos + _rotate_half_tail(t) * sin
        o_ref[:, c0 + MLA_NOPE:c0 + MLA_QK_PAD] = t.astype(BF16)


def _mla_q_up(q_lat, w_qb_p, g_qn_p, cos, sin):
    n, k = q_lat.shape
    width = w_qb_p.shape[1]
    tm = 256
    return pl.pallas_call(
        _mla_q_body,
        out_shape=jax.ShapeDtypeStruct((n, width), BF16),
        grid=(n // tm,),
        in_specs=[pl.BlockSpec((tm, k), lambda i: (i, 0)),
                  pl.BlockSpec((k, width), lambda i: (0, 0)),
                  pl.BlockSpec((1, MLA_QK_PAD), lambda i: (0, 0)),
                  pl.BlockSpec((tm, V7X_LANES), lambda i: (i, 0)),
                  pl.BlockSpec((tm, V7X_LANES), lambda i: (i, 0))],
        out_specs=pl.BlockSpec((tm, width), lambda i: (i, 0)),
        compiler_params=_cparams(("arbitrary",)),
        name="mla_q_up",
    )(q_lat, w_qb_p, g_qn_p, cos, sin)


def _mla_kv_body(x_ref, w_ref, kr_ref, g_ref, cos_ref, sin_ref, k_ref, v_ref):
    acc = jnp.dot(x_ref[...].astype(BF16), w_ref[...], preferred_element_type=F32)
    cos, sin = cos_ref[...], sin_ref[...]
    kr = kr_ref[...]
    ones = _segment_ones(V7X_LANES)
    kr_ss = _segment_sums(kr * kr, ones)
    per = MLA_NOPE + MLA_V
    for h in range(MLA_HEADS):
        nope = acc[:, h * per:h * per + MLA_NOPE]
        r = lax.rsqrt((_segment_sums(nope * nope, ones) + kr_ss) / MLA_QK + EPS)
        c0 = h * MLA_QK_PAD
        k_ref[:, c0:c0 + MLA_NOPE] = (nope * r * g_ref[:, :MLA_NOPE]).astype(BF16)
        t = kr * r * g_ref[:, MLA_NOPE:]
        t = t * cos + _rotate_half_tail(t) * sin
        k_ref[:, c0 + MLA_NOPE:c0 + MLA_QK_PAD] = t.astype(BF16)
        v_ref[:, h * MLA_V:(h + 1) * MLA_V] = acc[:, h * per + MLA_NOPE:(h + 1) * per].astype(BF16)


def _mla_kv_up(c_kv, w_kvb, k_rope_p, g_kn_p, cos, sin):
    n, k = c_kv.shape
    tm = 256
    return pl.pallas_call(
        _mla_kv_body,
        out_shape=(jax.ShapeDtypeStruct((n, MLA_HEADS * MLA_QK_PAD), BF16),
                   jax.ShapeDtypeStruct((n, MLA_HEADS * MLA_V), BF16)),
        grid=(n // tm,),
        in_specs=[pl.BlockSpec((tm, k), lambda i: (i, 0)),
                  pl.BlockSpec(w_kvb.shape, lambda i: (0, 0)),
                  pl.BlockSpec((tm, V7X_LANES), lambda i: (i, 0)),
                  pl.BlockSpec((1, MLA_QK_PAD), lambda i: (0, 0)),
                  pl.BlockSpec((tm, V7X_LANES), lambda i: (i, 0)),
                  pl.BlockSpec((tm, V7X_LANES), lambda i: (i, 0))],
        out_specs=(pl.BlockSpec((tm, MLA_HEADS * MLA_QK_PAD), lambda i: (i, 0)),
                   pl.BlockSpec((tm, MLA_HEADS * MLA_V), lambda i: (i, 0))),
        compiler_params=_cparams(("arbitrary",)),
        name="mla_kv_up",
    )(c_kv, w_kvb, k_rope_p, g_kn_p, cos, sin)


def _even_mixer(h, dims, cache, p, tables):
    (bp, sp, dec_b, dec_s, past) = dims
    n_ctx = bp * sp
    n = h.shape[0]
    (w_in, g_q, w_qb, g_kv, w_kvb, g_qn, g_kn, g_wq, g_wk, sink) = p
    ckv_c, krope_c, wk_c, wv_c = cache
    (cos_h, sin_h), (cos_m, sin_m) = tables["head"], tables["mla"]
    o1 = MLA_Q_RANK
    o2 = o1 + MLA_KV_RANK
    o3 = o2 + MLA_ROPE
    pad = (-(w_in.shape[1])) % V7X_LANES
    w_in_p = jnp.concatenate([w_in[:, :o2], w_in[:, o3:], w_in[:, o2:o3],
                              jnp.zeros((w_in.shape[0], pad), w_in.dtype)], axis=1).astype(BF16)
    proj = _matmul(h, w_in_p, out_dtype=F32, name="even_in_proj")
    c1 = o2 + WIN_HEADS * HEAD_DIM
    c2 = c1 + WIN_KV * HEAD_DIM
    c3 = c2 + WIN_KV * HEAD_DIM
    plan = ((0, o1, "norm", 0), (o1, MLA_KV_RANK, "norm_f32", 0),
            (o2, WIN_HEADS * HEAD_DIM, "heads", HEAD_DIM),
            (c1, WIN_KV * HEAD_DIM, "heads+state", HEAD_DIM),
            (c2, WIN_KV * HEAD_DIM, "cast", 0))
    q_lat, c_kv, wq_r, wk_r, wk, wv_b = _prep(
        proj, plan, ((g_q,), (g_kv,), (g_wq, cos_h, sin_h), (g_wk, cos_h, sin_h), ()), "even_prep")
    wv = proj[:n_ctx, c2:c3]
    k_rope_p = proj[:, c3:c3 + V7X_LANES]

    head_pad = ((0, 0), (0, 0), (0, MLA_QK_PAD - MLA_QK))
    w_qb_p = jnp.pad(w_qb.reshape(MLA_Q_RANK, MLA_HEADS, MLA_QK), head_pad)
    w_qb_p = w_qb_p.reshape(MLA_Q_RANK, MLA_HEADS * MLA_QK_PAD).astype(BF16)
    g_qn_p = jnp.pad(g_qn.astype(F32), (0, MLA_QK_PAD - MLA_QK)).reshape(1, MLA_QK_PAD)
    g_kn_p = jnp.pad(g_kn.astype(F32), (0, MLA_QK_PAD - MLA_QK)).reshape(1, MLA_QK_PAD)
    q_mla = _mla_q_up(q_lat, w_qb_p, g_qn_p, cos_m, sin_m)

    ckv_all = jnp.concatenate([c_kv, ckv_c.reshape(dec_b * past, MLA_KV_RANK)], axis=0)
    krope_cache = jnp.pad(krope_c.reshape(dec_b * past, MLA_ROPE), ((0, 0), (0, V7X_LANES - MLA_ROPE)))
    krope_all = jnp.concatenate([k_rope_p, krope_cache], axis=0)
    mk, mv = _mla_kv_up(ckv_all, w_kvb.astype(BF16), krope_all, g_kn_p, cos_m, sin_m)

    mla_scale = MLA_QK ** -0.5
    mla_cols = MLA_HEADS * MLA_V
    out = _attention(q_mla, 0, mk, mv, 0, batch=bp, seq=sp, nkv=MLA_HEADS, groups=1,
                     dq=MLA_QK_PAD, dv=MLA_V, scale=mla_scale, hb=MLA_HEADS, tq=256,
                     out_shape=(n, mla_cols + WIN_HEADS * HEAD_DIM), name="mla_attn_ctx")
    out = _attention(q_mla, n_ctx, mk, mv, n_ctx, batch=dec_b, seq=dec_s, nkv=MLA_HEADS, groups=1,
                     dq=MLA_QK_PAD, dv=MLA_V, scale=mla_scale, hb=4, tq=512,
                     k_ctx=mk, v_ctx=mv, kc_off=n, t_ctx=past, out=out, out_off=(n_ctx, 0),
                     name="mla_attn_lat")

    grp = WIN_HEADS // WIN_KV
    sink_b = jnp.broadcast_to(sink.astype(F32)[:, None], (WIN_HEADS, V7X_LANES))
    win_scale = HEAD_DIM ** -0.5
    out = _attention(wq_r, 0, wk_r, wv_b, 0, batch=bp, seq=sp, nkv=WIN_KV, groups=grp,
                     dq=HEAD_DIM, dv=HEAD_DIM, scale=win_scale, hb=WIN_KV, tq=256, sink=sink_b,
                     out=out, out_off=(0, mla_cols), name="win_attn_ctx")
    out = _attention(wq_r, n_ctx, wk_r, wv_b, n_ctx, batch=dec_b, seq=dec_s, nkv=WIN_KV, groups=grp,
                     dq=HEAD_DIM, dv=HEAD_DIM, scale=win_scale, hb=WIN_KV, tq=256,
                     k_ctx=wk_c.reshape(dec_b * past, WIN_KV * HEAD_DIM).astype(BF16),
                     v_ctx=wv_c.reshape(dec_b * past, WIN_KV * HEAD_DIM).astype(BF16),
                     kc_off=0, t_ctx=past, sink=sink_b, window=WINDOW,
                     out=out, out_off=(n_ctx, mla_cols), name="win_attn_lat")
    state = (c_kv[:n_ctx].reshape(bp, 1, sp, MLA_KV_RANK),
             k_rope_p[:n_ctx, :MLA_ROPE].reshape(bp, 1, sp, MLA_ROPE),
             wk[:n_ctx].reshape(bp, 1, sp, WIN_KV, HEAD_DIM),
             wv.reshape(bp, 1, sp, WIN_KV, HEAD_DIM))
    return out, state


def _odd_mixer(h, dims, cache, p, lam_init, layer, tables):
    (bp, sp, dec_b, dec_s, past) = dims
    n_ctx = bp * sp
    (w_in, g_dq, g_dk, lq1, lk1, lq2, lk2, g_sub, g_aq, g_ak) = p
    dk_c, dv_c, ak_c, av_c = cache
    (cos_h, sin_h), (cos_d, sin_d) = tables["head"], tables["diff"]
    o1 = DIFF_HEADS * HEAD_DIM
    o2 = o1 + DIFF_KV * HEAD_DIM
    o3 = o2 + DIFF_KV * HEAD_DIM
    o4 = o3 + AX_HEADS * HEAD_DIM
    o5 = o4 + AX_KV * HEAD_DIM
    proj = _matmul(h, w_in, layer=layer, out_dtype=F32, name="odd_in_proj")
    plan = ((0, o1, "heads", DIFF_D), (o1, o2 - o1, "heads+state", DIFF_D), (o2, o3 - o2, "cast", 0),
            (o3, o4 - o3, "heads", HEAD_DIM), (o4, o5 - o4, "heads+state", HEAD_DIM),
            (o5, AX_KV * HEAD_DIM, "cast", 0))
    dq_r, dk_r, dk, dv_b, aq_r, ak_r, ak, av_b = _prep(
        proj, plan, ((g_dq, cos_d, sin_d), (g_dk, cos_d, sin_d), (), (g_aq, cos_h, sin_h),
                     (g_ak, cos_h, sin_h), ()), "odd_prep")
    dv = proj[:n_ctx, o2:o3]
    av = proj[:n_ctx, o5:]

    lam = (jnp.exp(jnp.sum(lq1.astype(F32) * lk1.astype(F32)))
           - jnp.exp(jnp.sum(lq2.astype(F32) * lk2.astype(F32))) + lam_init)
    lam_b = jnp.broadcast_to(lam.astype(F32), (1, HEAD_DIM))
    g_sub_b = g_sub.astype(F32).reshape(1, HEAD_DIM)
    grp = DIFF_HEADS // DIFF_KV
    diff_kw = dict(nkv=DIFF_KV, groups=grp, dq=HEAD_DIM, dv=HEAD_DIM, scale=DIFF_D ** -0.5, hb=DIFF_KV,
                   lam=lam_b, g_sub=g_sub_b, diff_post_scale=1.0 - lam_init)
    n = n_ctx + dec_b * dec_s
    diff_cols = DIFF_HEADS * HEAD_DIM
    out = _attention(dq_r, 0, dk_r, dv_b, 0, batch=bp, seq=sp, tq=256,
                     out_shape=(n, diff_cols + AX_HEADS * HEAD_DIM), name="diff_attn_ctx", **diff_kw)
    out = _attention(dq_r, n_ctx, dk_r, dv_b, n_ctx, batch=dec_b, seq=dec_s, tq=128,
                     k_ctx=dk_c.reshape(dec_b * past, DIFF_KV * HEAD_DIM).astype(BF16),
                     v_ctx=dv_c.reshape(dec_b * past, DIFF_KV * HEAD_DIM).astype(BF16),
                     kc_off=0, t_ctx=past, out=out, out_off=(n_ctx, 0), name="diff_attn_lat", **diff_kw)
    agrp = AX_HEADS // AX_KV
    ax_kw = dict(nkv=AX_KV, groups=agrp, dq=HEAD_DIM, dv=HEAD_DIM, scale=HEAD_DIM ** -0.5, hb=AX_KV)
    out = _attention(aq_r, 0, ak_r, av_b, 0, batch=bp, seq=sp, tq=256,
                     out=out, out_off=(0, diff_cols), name="ax_attn_ctx", **ax_kw)
    out = _attention(aq_r, n_ctx, ak_r, av_b, n_ctx, batch=dec_b, seq=dec_s, tq=256,
                     k_ctx=ak_c.reshape(dec_b * past, AX_KV * HEAD_DIM).astype(BF16),
                     v_ctx=av_c.reshape(dec_b * past, AX_KV * HEAD_DIM).astype(BF16),
                     kc_off=0, t_ctx=past, out=out, out_off=(n_ctx, diff_cols), name="ax_attn_lat", **ax_kw)
    state = (dk[:n_ctx].reshape(bp, 1, sp, DIFF_KV, 2, DIFF_D),
             dv.reshape(bp, 1, sp, DIFF_KV, HEAD_DIM),
             ak[:n_ctx].reshape(bp, 1, sp, AX_KV, HEAD_DIM),
             av.reshape(bp, 1, sp, AX_KV, HEAD_DIM))
    return out, state


def kernel(x_prompt, x_sample, cache_mla_ckv, cache_mla_krope, cache_win_k, cache_win_v, cache_diff_k, cache_diff_v, cache_ax_k, cache_ax_v, c, c_ctx, w_mod, b_mod, g_norm_mix, g_norm_ffn, w_in_even, g_mla_q, w_mla_qb, g_mla_kv, w_mla_kvb, g_mla_qn, g_mla_kn, g_win_qn, g_win_kn, win_sink, w_out_even, w_in_odd, g_diff_qn, g_diff_kn, diff_lq1, diff_lk1, diff_lq2, diff_lk2, g_diff_sub, g_ax_qn, g_ax_kn, w_out_odd, w_router, b_router, w_exp_gate, w_exp_up, w_exp_down, w_sh_gate, w_sh_up, w_sh_down):
    bp, sp, d = x_prompt.shape
    dec_b, dec_s, _ = x_sample.shape
    depth = w_mod.shape[0]
    n_ctx = bp * sp
    n = n_ctx + dec_b * dec_s
    dims = (bp, sp, dec_b, dec_s, cache_mla_ckv.shape[2])
    group_fn = lambda tm: _group_index_fn(tm, n_ctx, dec_s)
    n_groups = 1 + dec_b

    y = jnp.concatenate([x_prompt.reshape(n_ctx, d), x_sample.reshape(dec_b * dec_s, d)], axis=0)

    cond = jnp.concatenate([c_ctx[None], c, jnp.zeros((8 - n_groups % 8, d), F32)], axis=0)
    cond = jax.nn.silu(cond)

    past = cache_mla_ckv.shape[2]
    tables = {"head": _rope_tables(HEAD_DIM, n_ctx, dec_b, dec_s, 0),
              "diff": _rope_tables(DIFF_D, n_ctx, dec_b, dec_s, 0),
              "mla": _rope_tables(MLA_ROPE, n_ctx, dec_b, dec_s, dec_b * past)}

    states_even, states_odd = [], []
    for l in range(depth):
        i = l // 2
        mod = _matmul(cond, w_mod, layer=l, out_dtype=F32, bias=b_mod.reshape(depth, 1, 6 * d),
                      name="modulation")
        mod = mod[:n_groups].reshape(n_groups, 6, 1, d)
        sh1, sc1, g1, sh2, sc2, g2 = (mod[:, j] for j in range(6))
        h = _ada_norm(y, g_norm_mix[l][None], sh1, sc1, group_fn, name="ada_norm_mix")
        if l % 2 == 0:
            pe = (w_in_even[i], g_mla_q[i], w_mla_qb[i], g_mla_kv[i], w_mla_kvb[i], g_mla_qn[i],
                  g_mla_kn[i], g_win_qn[i], g_win_kn[i], win_sink[i])
            cache = (cache_mla_ckv[:, i], cache_mla_krope[:, i], cache_win_k[:, i], cache_win_v[:, i])
            out, state = _even_mixer(h, dims, cache, pe, tables)
            states_even.append(state)
            w_out = w_out_even
        else:
            po = (w_in_odd, g_diff_qn[i], g_diff_kn[i], diff_lq1[i], diff_lk1[i], diff_lq2[i],
                  diff_lk2[i], g_diff_sub[i], g_ax_qn[i], g_ax_kn[i])
            cache = (cache_diff_k[:, i], cache_diff_v[:, i], cache_ax_k[:, i], cache_ax_v[:, i])
            lam_init = 0.8 - 0.6 * math.exp(-0.3 * l)
            out, state = _odd_mixer(h, dims, cache, po, lam_init, i, tables)
            states_odd.append(state)
            w_out = w_out_odd
        y = _matmul(out, w_out[i].astype(BF16), out_dtype=F32, resid=y, gates=g1, group_fn=group_fn,
                    name="mixer_out_proj")
        splits = [(0, n)] if l + 1 < depth else [(0, n_ctx), (n_ctx, n - n_ctx)]
        outs = _moe(y, g_norm_ffn[l][None], sh2, sc2, g2, group_fn, l, w_router, b_router,
                    w_exp_gate, w_exp_up, w_exp_down, w_sh_gate, w_sh_up, w_sh_down, splits)
        y = outs[0]

    yp = outs[0].reshape(bp, sp, d)
    ys = outs[1].reshape(dec_b, dec_s, d)
    even = tuple(jnp.concatenate([s[j] for s in states_even], axis=1) for j in range(4))
    odd = tuple(jnp.concatenate([s[j] for s in states_odd], axis=1) for j in range(4))
    return (yp, ys) + even + odd
```

```python
import functools
import math

import jax
import jax.numpy as jnp
from jax import lax
from jax.experimental import pallas as pl
from jax.experimental.pallas import tpu as pltpu

F32 = jnp.float32
BF16 = jnp.bfloat16

GRID_W = 64
ROPE_THETA = 10000.0
EPS = 1e-6
NEG_INF = -1e30
LOG2E = math.log2(math.e)
HEAD_DIM = 128
MLA_HEADS = 16
MLA_Q_RANK = 768
MLA_KV_RANK = 512
MLA_NOPE = 128
MLA_ROPE = 64
MLA_V = 128
MLA_QK = MLA_NOPE + MLA_ROPE
MLA_QK_PAD = 256
WIN_HEADS = 16
WIN_KV = 4
WINDOW = 128
DIFF_HEADS = 16
DIFF_KV = 4
DIFF_D = HEAD_DIM // 2
AX_HEADS = 16
AX_KV = 4
N_EXPERTS = 64
N_EXPERT_GROUPS = 8
TOPK_GROUPS = 4
TOP_K = 8
ROUTED_SCALE = 2.5

V7X_LANES = 128
V7X_SUBLANES = 8
V7X_VMEM_LIMIT_BYTES = 56 * 1024 * 1024

ROW_TILE = 256
MM_ROW_TILE = 512
MM_MAX_COL_TILE = 2048
MM_WEIGHT_TILE_BYTES = 32 * 1024 * 1024
COMBINE_ROW_TILE = 128
PLAN_COL_TILE = 2048
ATTN_KEY_CHUNK = 512
Q_TILE_CTX = 256
Q_TILE_MLA_LAT = 512
Q_TILE_GQA_LAT = 256
Q_TILE_DIFF_LAT = 128
MLA_LAT_HEADS_PER_STEP = 4
MAX_ROW_TOKEN_TILES_PER_STEP = 8
COUNT_SPLIT = 64
MOE_ROW_TILE = 256
MOE_CHUNKS = 8
COMBINE_CHUNK_ROWS = 1024


def _cparams(sem):
    return pltpu.CompilerParams(dimension_semantics=sem, vmem_limit_bytes=V7X_VMEM_LIMIT_BYTES)


def _largest_tile(n, cap, step):
    if n <= cap:
        return n
    t = (cap // step) * step
    while t >= step:
        if n % t == 0:
            return t
        t -= step
    raise ValueError(f"no tile for {n} under {cap}")


def _group_index_fn(tm, n_ctx_rows, dec_rows):
    assert n_ctx_rows % tm == 0 and dec_rows % tm == 0
    ctx_tiles = n_ctx_rows // tm
    per_dec = dec_rows // tm

    def fn(i):
        return jnp.where(i < ctx_tiles, 0, 1 + (jnp.maximum(i - ctx_tiles, 0)) // per_dec)

    return fn


def _mm_body(*refs, has_bias, has_resid):
    x_ref, w_ref = refs[0], refs[1]
    idx = 2
    if has_bias:
        bias_ref = refs[idx]
        idx += 1
    if has_resid:
        resid_ref, gate_ref = refs[idx], refs[idx + 1]
        idx += 2
    o_ref = refs[idx]
    if w_ref.dtype == BF16:
        wb_ref = w_ref
    else:
        wb_ref = refs[idx + 1]

        @pl.when(pl.program_id(1) == 0)
        def _():
            wb_ref[...] = w_ref[...].astype(BF16)

    acc = jnp.dot(x_ref[...].astype(BF16), wb_ref[...], preferred_element_type=F32)
    if has_bias:
        acc = acc + bias_ref[...]
    if has_resid:
        acc = resid_ref[...] + gate_ref[...] * acc
    o_ref[...] = acc.astype(o_ref.dtype)


def _matmul(x, w, *, out_dtype, layer=None, bias=None, resid=None, gates=None, group_fn=None, name):
    m, k = x.shape
    n = w.shape[-1]
    tm = _largest_tile(m, MM_ROW_TILE, V7X_SUBLANES)
    precast = w.dtype == BF16
    col_bytes = k * (4 if precast else 10)
    budget = MM_WEIGHT_TILE_BYTES // 2 if precast else MM_WEIGHT_TILE_BYTES
    tn_cap = max(V7X_LANES, min(MM_MAX_COL_TILE, budget // col_bytes))
    tn = _largest_tile(n, tn_cap, V7X_LANES)
    grid = (n // tn, m // tm)
    if layer is None:
        w_spec = pl.BlockSpec((k, tn), lambda j, i: (0, j))
    else:
        w_spec = pl.BlockSpec((None, k, tn), lambda j, i: (layer, 0, j))
    in_specs = [pl.BlockSpec((tm, k), lambda j, i: (i, 0)), w_spec]
    args = [x, w]
    if bias is not None:
        in_specs.append(pl.BlockSpec((None, 1, tn), lambda j, i: (layer, 0, j)))
        args.append(bias)
    if resid is not None:
        gfn = group_fn(tm)
        in_specs.append(pl.BlockSpec((tm, tn), lambda j, i: (i, j)))
        in_specs.append(pl.BlockSpec((None, 1, tn), lambda j, i: (gfn(i), 0, j)))
        args += [resid, gates]
    return pl.pallas_call(
        functools.partial(_mm_body, has_bias=bias is not None, has_resid=resid is not None),
        out_shape=jax.ShapeDtypeStruct((m, n), out_dtype),
        grid=grid,
        in_specs=in_specs,
        out_specs=pl.BlockSpec((tm, tn), lambda j, i: (i, j)),
        scratch_shapes=[] if precast else [pltpu.VMEM((k, tn), BF16)],
        compiler_params=_cparams(("arbitrary", "arbitrary")),
        name=name,
    )(*args)


def _first_max(cur, sub, limit):
    m = jnp.max(cur, axis=0, keepdims=True)
    first = jnp.min(jnp.where(cur == m, sub, limit), axis=0, keepdims=True)
    return m, first, sub == first


def _route_columns(logits, bias, tri, carry):
    e, t = logits.shape
    per = e // N_EXPERT_GROUPS
    scores = jax.nn.sigmoid(logits)
    biased = scores + bias
    neg = -jnp.inf
    gsub = lax.broadcasted_iota(jnp.int32, (per, t), 0)
    grp = []
    for g in range(N_EXPERT_GROUPS):
        xg = biased[g * per:(g + 1) * per]
        m1, _, hit = _first_max(xg, gsub, per)
        m2 = jnp.max(jnp.where(hit, neg, xg), axis=0, keepdims=True)
        grp.append(m1 + m2)
    grp = jnp.concatenate(grp, axis=0)
    nsub = lax.broadcasted_iota(jnp.int32, (N_EXPERT_GROUPS, t), 0)
    gsel = jnp.zeros((N_EXPERT_GROUPS, t), F32)
    for _ in range(TOPK_GROUPS):
        _, _, hit = _first_max(grp, nsub, N_EXPERT_GROUPS)
        gsel = jnp.where(hit, 1.0, gsel)
        grp = jnp.where(hit, neg, grp)
    emask = jnp.concatenate([jnp.broadcast_to(gsel[g:g + 1], (per, t)) for g in range(N_EXPERT_GROUPS)], axis=0)
    cur = jnp.where(emask > 0.5, biased, neg)
    esub = lax.broadcasted_iota(jnp.int32, (e, t), 0)
    sel = jnp.zeros((e, t), F32)
    ids, ws, hits = [], [], []
    for _ in range(TOP_K):
        _, first, hit = _first_max(cur, esub, e)
        ids.append(first)
        ws.append(jnp.sum(jnp.where(hit, scores, 0.0), axis=0, keepdims=True))
        hits.append(hit)
        sel = jnp.where(hit, 1.0, sel)
        cur = jnp.where(hit, neg, cur)
    w = jnp.concatenate(ws, axis=0)
    gates = w / jnp.sum(w, axis=0, keepdims=True) * ROUTED_SCALE
    rank_all = jnp.dot(sel.astype(BF16), tri, preferred_element_type=F32) + carry
    ranks = [jnp.sum(jnp.where(hit, rank_all, 0.0), axis=0, keepdims=True) for hit in hits]
    counts = jnp.sum(sel, axis=1, keepdims=True)
    return (jnp.concatenate(ids, axis=0), gates, jnp.concatenate(ranks, axis=0).astype(jnp.int32), counts,
            rank_all + sel)


def _adanorm_body(x_ref, g_ref, shift_ref, scale_ref, *rest, has_router):
    x = x_ref[...]
    y = x * lax.rsqrt(jnp.mean(x * x, axis=-1, keepdims=True) + EPS) * g_ref[...]
    t = y * (1.0 + scale_ref[...]) + shift_ref[...]
    if has_router:
        (whi_ref, wlo_ref, br_ref, tri_ref, o_ref, ids_ref, gate_ref, rank_ref, cnt_ref, cum_ref,
         carry_ref) = rest

        @pl.when(pl.program_id(0) == 0)
        def _():
            carry_ref[...] = jnp.zeros_like(carry_ref)

        t_hi = t.astype(BF16)
        t_lo = (t - t_hi.astype(F32)).astype(BF16)
        nt = lambda a, b: lax.dot_general(a, b, (((1,), (1,)), ((), ())), preferred_element_type=F32)
        logits = nt(whi_ref[...], t_hi) + (nt(wlo_ref[...], t_hi) + nt(whi_ref[...], t_lo))
        ids, gates, ranks, counts, cum = _route_columns(logits, br_ref[...], tri_ref[...], carry_ref[:, :1])
        cum_ref[...] = cum
        ids_ref[...] = ids
        gate_ref[...] = gates
        rank_ref[...] = ranks
        carry_ref[...] = carry_ref[...] + counts
        cnt_ref[...] = carry_ref[...]
    else:
        (o_ref,) = rest
    o_ref[...] = t.astype(o_ref.dtype)


def _ada_norm(x, g, shift, scale, group_fn, *, w_router=None, b_router=None, name):
    n, d = x.shape
    tm = ROW_TILE
    gfn = group_fn(tm)
    in_specs = [pl.BlockSpec((tm, d), lambda i: (i, 0)),
                pl.BlockSpec((1, d), lambda i: (0, 0)),
                pl.BlockSpec((None, 1, d), lambda i: (gfn(i), 0, 0)),
                pl.BlockSpec((None, 1, d), lambda i: (gfn(i), 0, 0))]
    args = [x, g, shift, scale]
    out_shape = jax.ShapeDtypeStruct((n, d), BF16)
    out_specs = pl.BlockSpec((tm, d), lambda i: (i, 0))
    scratch = []
    if w_router is not None:
        e = w_router.shape[1]
        tri = (lax.broadcasted_iota(jnp.int32, (tm, tm), 0) < lax.broadcasted_iota(jnp.int32, (tm, tm), 1))
        in_specs += [pl.BlockSpec((e, d), lambda i: (0, 0)),
                     pl.BlockSpec((e, d), lambda i: (0, 0)),
                     pl.BlockSpec((e, tm), lambda i: (0, 0)),
                     pl.BlockSpec((tm, tm), lambda i: (0, 0))]
        w_t = w_router.T.astype(F32)
        w_hi = w_t.astype(BF16)
        w_lo = (w_t - w_hi.astype(F32)).astype(BF16)
        args += [w_hi, w_lo, jnp.broadcast_to(b_router.astype(F32)[:, None], (e, tm)), tri.astype(BF16)]
        col = lambda rows, dt: (jax.ShapeDtypeStruct((rows, n), dt), pl.BlockSpec((rows, tm), lambda i: (0, i)))
        extra = [col(TOP_K, jnp.int32), col(TOP_K, F32), col(TOP_K, jnp.int32),
                 (jax.ShapeDtypeStruct((e, V7X_LANES), F32), pl.BlockSpec((e, V7X_LANES), lambda i: (0, 0))),
                 col(e, F32)]
        out_shape = (out_shape,) + tuple(s for s, _ in extra)
        out_specs = (out_specs,) + tuple(b for _, b in extra)
        scratch = [pltpu.VMEM((e, V7X_LANES), F32)]
    return pl.pallas_call(
        functools.partial(_adanorm_body, has_router=w_router is not None),
        out_shape=out_shape,
        grid=(n // tm,),
        in_specs=in_specs,
        out_specs=out_specs,
        scratch_shapes=scratch,
        compiler_params=_cparams(("arbitrary",)),
        name=name,
    )(*args)


def _attn_body(*refs, hb, groups, dq, dv, tq, tk, scale, t_new, t_ctx, window, has_sink,
               diff_post_scale, has_out_buf):
    it = iter(refs)
    q_ref, kn_ref, vn_ref = next(it), next(it), next(it)
    kc_ref = vc_ref = sink_ref = lam_ref = gsub_ref = None
    if t_ctx:
        kc_ref, vc_ref = next(it), next(it)
    if has_sink:
        sink_ref = next(it)
    diff = diff_post_scale is not None
    if diff:
        lam_ref, gsub_ref = next(it), next(it)
    if has_out_buf:
        next(it)
    o_ref = next(it)

    h = pl.program_id(1)
    i = pl.program_id(2)
    nstack = 2 * groups if diff else groups
    rows = nstack * tq

    def stack_rows(parts):
        return parts[0] if len(parts) == 1 else jnp.concatenate(parts, axis=0)

    if window is not None:
        wk = min(t_new, tq + 2 * window)
        if wk == t_new:
            wstart = 0
        else:
            wstart = pl.multiple_of(jnp.clip(i * tq - window, 0, t_new - wk), V7X_LANES)
        qi = i * tq + lax.broadcasted_iota(jnp.int32, (tq, wk), 0)
        kj = wstart + lax.broadcasted_iota(jnp.int32, (tq, wk), 1)
        wmask = stack_rows([jnp.abs(qi - kj) <= window] * nstack)

    for j in range(hb):
        qj = q_ref[:, j * groups * dq:(j + 1) * groups * dq]
        parts = [qj[:, g * dq:(g + 1) * dq] for g in range(groups)]
        if diff:
            lo = lax.broadcasted_iota(jnp.int32, (tq, dq), 1) < dq // 2
            zero = jnp.zeros((tq, dq), qj.dtype)
            parts = [jnp.where(lo, p, zero) for p in parts] + [jnp.where(lo, zero, p) for p in parts]
        qs = stack_rows(parts)

        if has_sink:
            sinks = []
            for g in range(groups):
                hh = (h * hb + j) * groups + g
                sinks.append(jnp.broadcast_to(sink_ref[pl.ds(hh, 1), :][:, :1], (tq, 1)))
            m = stack_rows(sinks) * LOG2E
            l = jnp.ones((rows, 1), F32)
        else:
            m = jnp.full((rows, 1), NEG_INF, F32)
            l = jnp.zeros((rows, 1), F32)
        acc = jnp.zeros((rows, dv), F32)

        def step(carry, kc, vc, mask):
            m, l, acc = carry
            s = lax.dot_general(qs, kc, (((1,), (1,)), ((), ())), preferred_element_type=F32) * (scale * LOG2E)
            if mask is not None:
                s = jnp.where(mask, s, NEG_INF)
            m_new = jnp.maximum(m, jnp.max(s, axis=-1, keepdims=True))
            alpha = jnp.exp2(m - m_new)
            p = jnp.exp2(s - m_new)
            l = alpha * l + jnp.sum(p, axis=-1, keepdims=True)
            acc = alpha * acc + jnp.dot(p.astype(BF16), vc, preferred_element_type=F32)
            return m_new, l, acc

        carry = (m, l, acc)
        kcols = slice(j * dq, (j + 1) * dq)
        vcols = slice(j * dv, (j + 1) * dv)
        if window is not None:
            carry = step(carry, kn_ref[pl.ds(wstart, wk), kcols], vn_ref[pl.ds(wstart, wk), vcols], wmask)
        else:
            for c in range(t_new // tk):
                carry = step(carry, kn_ref[c * tk:(c + 1) * tk, kcols], vn_ref[c * tk:(c + 1) * tk, vcols], None)
        if t_ctx:
            tkc = min(tk, t_ctx)
            for c in range(t_ctx // tkc):
                carry = step(carry, kc_ref[c * tkc:(c + 1) * tkc, kcols], vc_ref[c * tkc:(c + 1) * tkc, vcols], None)
        m, l, acc = carry
        o = acc / l
        if diff:
            half = groups * tq
            d = o[:half] - lam_ref[...] * o[half:]
            o = (d * lax.rsqrt(jnp.mean(d * d, axis=-1, keepdims=True) + EPS) * gsub_ref[...]) * diff_post_scale
        outs = [o[g * tq:(g + 1) * tq] for g in range(groups)]
        oj = outs[0] if groups == 1 else jnp.concatenate(outs, axis=1)
        o_ref[:, j * groups * dv:(j + 1) * groups * dv] = oj.astype(o_ref.dtype)


def _attention(q, q_off, k_new, v_new, kn_off, *, batch, seq, nkv, groups, dq, dv, scale, hb, tq,
               k_ctx=None, v_ctx=None, kc_off=0, t_ctx=0, sink=None, window=None,
               lam=None, g_sub=None, diff_post_scale=None, out=None, out_shape=None, out_off=(0, 0), name):
    t_new = seq
    tq = min(tq, seq)
    tk = min(ATTN_KEY_CHUNK, t_new)
    assert seq % tq == 0 and t_new % tk == 0 and nkv % hb == 0
    assert q_off % tq == 0 and kn_off % t_new == 0
    qb, nb = q_off // tq, kn_off // t_new
    spt = seq // tq
    in_specs = [pl.BlockSpec((tq, hb * groups * dq), lambda b, h, i: (qb + b * spt + i, h)),
                pl.BlockSpec((t_new, hb * dq), lambda b, h, i: (nb + b, h)),
                pl.BlockSpec((t_new, hb * dv), lambda b, h, i: (nb + b, h))]
    args = [q, k_new, v_new]
    if t_ctx:
        assert kc_off % t_ctx == 0
        cb = kc_off // t_ctx
        in_specs += [pl.BlockSpec((t_ctx, hb * dq), lambda b, h, i: (cb + b, h)),
                     pl.BlockSpec((t_ctx, hb * dv), lambda b, h, i: (cb + b, h))]
        args += [k_ctx, v_ctx]
    if sink is not None:
        in_specs.append(pl.BlockSpec(sink.shape, lambda b, h, i: (0, 0)))
        args.append(sink)
    if diff_post_scale is not None:
        in_specs += [pl.BlockSpec((1, dv), lambda b, h, i: (0, 0))] * 2
        args += [lam, g_sub]
    aliases = {}
    if out is not None:
        out_shape = out.shape
        in_specs.append(pl.BlockSpec(memory_space=pl.ANY))
        args.append(out)
        aliases = {len(args) - 1: 0}
    wblk = hb * groups * dv
    assert out_off[0] % tq == 0 and out_off[1] % wblk == 0
    ob, oc = out_off[0] // tq, out_off[1] // wblk
    body = functools.partial(
        _attn_body, hb=hb, groups=groups, dq=dq, dv=dv, tq=tq, tk=tk, scale=scale, t_new=t_new,
        t_ctx=t_ctx, window=window, has_sink=sink is not None, diff_post_scale=diff_post_scale,
        has_out_buf=out is not None)
    return pl.pallas_call(
        body,
        out_shape=jax.ShapeDtypeStruct(out_shape, BF16),
        grid=(batch, nkv // hb, spt),
        in_specs=in_specs,
        out_specs=pl.BlockSpec((tq, wblk), lambda b, h, i: (ob + b * spt + i, oc + h)),
        input_output_aliases=aliases,
        compiler_params=_cparams(("arbitrary", "arbitrary", "arbitrary")),
        name=name,
    )(*args)


def _experts_body(te_ref, nv_ref, x_ref, wg_ref, wu_ref, wd_ref, *rest, tile0):
    o_ref, wgb, wub, wdb = rest[-4:]
    t = pl.program_id(0)
    g = tile0 + t
    valid = g < nv_ref[0]
    prev = te_ref[jnp.maximum(g - 1, 0)]
    first = jnp.logical_or(t == 0, te_ref[g] != prev)

    @pl.when(jnp.logical_and(first, valid))
    def _():
        wgb[...] = wg_ref[...].astype(BF16)
        wub[...] = wu_ref[...].astype(BF16)
        wdb[...] = wd_ref[...].astype(BF16)

    @pl.when(valid)
    def _():
        x = x_ref[...]
        a = jnp.dot(x, wgb[...], preferred_element_type=F32)
        u = jnp.dot(x, wub[...], preferred_element_type=F32)
        hcur = a * jax.nn.sigmoid(a) * u
        o_ref[...] = jnp.dot(hcur.astype(BF16), wdb[...], preferred_element_type=F32).astype(o_ref.dtype)

    @pl.when(jnp.logical_not(valid))
    def _():
        o_ref[...] = jnp.zeros_like(o_ref)


def _routed_experts(xs, tile_expert, n_valid, w_gate, w_up, w_down, layer, tile0, n_tiles, rows_buf):
    r, d = xs.shape
    ff = w_gate.shape[-1]
    tm = MOE_ROW_TILE
    chunk_tiles = r // tm

    def x_blk(t, te, nv):
        last = jnp.clip(nv[0] - tile0 - 1, 0, chunk_tiles - 1)
        return (jnp.minimum(t, last), 0)

    def o_blk(t, te, nv):
        return (jnp.where(tile0 + t < nv[0], tile0 + t, n_tiles), 0)

    w_blk = lambda t, te, nv: (layer, te[tile0 + t], 0, 0)
    in_specs = [pl.BlockSpec((tm, d), x_blk),
                pl.BlockSpec((None, None, d, ff), w_blk),
                pl.BlockSpec((None, None, d, ff), w_blk),
                pl.BlockSpec((None, None, ff, d), w_blk)]
    args = [tile_expert, n_valid, xs, w_gate, w_up, w_down]
    aliases = {}
    if rows_buf is not None:
        in_specs.append(pl.BlockSpec(memory_space=pl.ANY))
        args.append(rows_buf)
        aliases = {len(args) - 1: 0}
    grid_spec = pltpu.PrefetchScalarGridSpec(
        num_scalar_prefetch=2,
        grid=(chunk_tiles,),
        in_specs=in_specs,
        out_specs=pl.BlockSpec((tm, d), o_blk),
        scratch_shapes=[pltpu.VMEM((d, ff), BF16), pltpu.VMEM((d, ff), BF16), pltpu.VMEM((ff, d), BF16)],
    )
    return pl.pallas_call(
        functools.partial(_experts_body, tile0=tile0),
        out_shape=jax.ShapeDtypeStruct(((n_tiles + 1) * tm, d), BF16),
        grid_spec=grid_spec,
        input_output_aliases=aliases,
        compiler_params=_cparams(("arbitrary",)),
        name="routed_experts",
    )(*args)


def _shared_body(t_ref, wg_ref, wu_ref, wd_ref, resid_ref, gate_ref, o_ref):
    x = t_ref[...]
    a = jnp.dot(x, wg_ref[...], preferred_element_type=F32)
    u = jnp.dot(x, wu_ref[...], preferred_element_type=F32)
    hcur = (a * jax.nn.sigmoid(a) * u).astype(BF16)
    shared = jnp.dot(hcur, wd_ref[...], preferred_element_type=F32)
    o_ref[...] = resid_ref[...] + gate_ref[...] * shared


def _shared_expert(t, ws_gate, ws_up, ws_down, resid, gates, group_fn):
    n, d = t.shape
    ff = ws_gate.shape[1]
    tm = ROW_TILE
    gfn = group_fn(tm)
    return pl.pallas_call(
        _shared_body,
        out_shape=jax.ShapeDtypeStruct((n, d), F32),
        grid=(n // tm,),
        in_specs=[pl.BlockSpec((tm, d), lambda i: (i, 0)),
                  pl.BlockSpec((d, ff), lambda i: (0, 0)),
                  pl.BlockSpec((d, ff), lambda i: (0, 0)),
                  pl.BlockSpec((ff, d), lambda i: (0, 0)),
                  pl.BlockSpec((tm, d), lambda i: (i, 0)),
                  pl.BlockSpec((None, 1, d), lambda i: (gfn(i), 0, 0))],
        out_specs=pl.BlockSpec((tm, d), lambda i: (i, 0)),
        compiler_params=_cparams(("arbitrary",)),
        name="shared_expert",
    )(t, ws_gate, ws_up, ws_down, resid, gates)


def _combine_body(rows_ref, w_ref, base_ref, gate_ref, *rest):
    o_ref = rest[-1]
    w = w_ref[...]
    acc = w[:, 0:1] * rows_ref[0].astype(F32)
    for c in range(1, rows_ref.shape[0]):
        acc = acc + w[:, c:c + 1] * rows_ref[c].astype(F32)
    o_ref[...] = base_ref[...] + gate_ref[...] * acc


def _combine(picked, weights, base, gates, group_fn, row_start, out, out_rows, out_off):
    d = base.shape[1]
    k, count, _ = picked.shape
    tm = COMBINE_ROW_TILE
    assert row_start % tm == 0 and count % tm == 0 and out_off % tm == 0
    r0, o0 = row_start // tm, out_off // tm
    gfn = group_fn(tm)
    in_specs = [pl.BlockSpec((k, tm, d), lambda i: (0, i, 0)),
                pl.BlockSpec((tm, k), lambda i: (r0 + i, 0)),
                pl.BlockSpec((tm, d), lambda i: (r0 + i, 0)),
                pl.BlockSpec((None, 1, d), lambda i: (gfn(r0 + i), 0, 0))]
    args = [picked, weights, base, gates]
    aliases = {}
    if out is not None:
        in_specs.append(pl.BlockSpec(memory_space=pl.ANY))
        args.append(out)
        aliases = {len(args) - 1: 0}
    return pl.pallas_call(
        _combine_body,
        out_shape=jax.ShapeDtypeStruct((out_rows, d), F32),
        grid=(count // tm,),
        in_specs=in_specs,
        out_specs=pl.BlockSpec((tm, d), lambda i: (o0 + i, 0)),
        input_output_aliases=aliases,
        compiler_params=_cparams(("arbitrary",)),
        name="moe_combine",
    )(*args)


def _plan_body(rs_ref, ids_ref, rank_ref, pos_ref):
    ids = ids_ref[...]
    base = jnp.zeros_like(ids)
    for e in range(N_EXPERTS):
        base = jnp.where(ids == e, rs_ref[e], base)
    pos_ref[...] = base + rank_ref[...]


def _row_token_body(te_ref, j0_ref, end_ref, hi_ref, lo_ref, o_ref, *, tm, n_tok, tile0):
    nb = end_ref.shape[-1]
    sub = lax.broadcasted_iota(jnp.int32, (tm, 1), 0)
    blk = lax.broadcasted_iota(jnp.int32, (tm, nb), 1).astype(F32)
    for s in range(o_ref.shape[0]):
        t = tile0 + pl.program_id(0) * o_ref.shape[0] + s
        e = te_ref[t]
        j = (j0_ref[t] + sub).astype(F32)
        full = jnp.sum(jnp.where(end_ref[e] <= j, 1.0, 0.0), axis=1, keepdims=True)
        pick = jnp.where(blk == full, 1.0, 0.0).astype(BF16)
        edge = (COUNT_SPLIT * jnp.dot(pick, hi_ref[e], preferred_element_type=F32)
                + jnp.dot(pick, lo_ref[e], preferred_element_type=F32))
        inside = jnp.sum(jnp.where(edge <= j, 1.0, 0.0), axis=1, keepdims=True)
        cnt = jnp.where(full >= nb, float(n_tok), full * V7X_LANES + inside)
        spare = lax.rem(t * tm, n_tok) + sub
        spare = jnp.where(spare >= n_tok, spare - n_tok, spare).astype(F32)
        tok = jnp.where(cnt >= n_tok, spare, cnt)
        o_ref[s] = jnp.broadcast_to(tok, (tm, V7X_LANES)).T[0:1, :].astype(jnp.int32)


def _dispatch_plan(ids, ranks, counts, cum, tm):
    k, n = ids.shape
    e = counts.shape[0]
    n_rows = n * k + e * tm
    n_tiles = n_rows // tm
    tiles_e = (counts + tm - 1) // tm
    tile_end = jnp.cumsum(tiles_e)
    row_start = ((tile_end - tiles_e) * tm).astype(jnp.int32)
    tile_ids = jnp.arange(n_tiles, dtype=jnp.int32)
    tile_expert = jnp.sum((tile_end[None, :] <= tile_ids[:, None]).astype(jnp.int32), axis=1)
    tile_expert = jnp.minimum(tile_expert, e - 1)

    assert n % V7X_LANES == 0 and tm <= n <= COUNT_SPLIT * 256
    nb = n // V7X_LANES
    cum_b = cum.reshape(e, nb, V7X_LANES)
    j0 = tile_ids * tm - row_start[tile_expert]
    block_end = cum_b[:, :, -1].reshape(e, 1, nb)
    cum_hi = jnp.floor(cum_b / COUNT_SPLIT)
    cum_lo = cum_b - COUNT_SPLIT * cum_hi
    chunk_tiles = n_tiles // MOE_CHUNKS
    per_step = max(c for c in range(1, MAX_ROW_TOKEN_TILES_PER_STEP + 1) if chunk_tiles % c == 0)

    def row_tokens(tile0, count):
        steps = count // per_step
        whole = lambda shape: pl.BlockSpec(shape, lambda t, *_: (0,) * len(shape))
        return pl.pallas_call(
            functools.partial(_row_token_body, tm=tm, n_tok=n, tile0=tile0),
            out_shape=jax.ShapeDtypeStruct((count, 1, tm), jnp.int32),
            grid_spec=pltpu.PrefetchScalarGridSpec(
                num_scalar_prefetch=2,
                grid=(steps,),
                in_specs=[whole((e, 1, nb)), whole((e, nb, V7X_LANES)), whole((e, nb, V7X_LANES))],
                out_specs=pl.BlockSpec((per_step, 1, tm), lambda t, *_: (t, 0, 0))),
            compiler_params=_cparams(("arbitrary",)),
            name="row_tokens",
        )(tile_expert, j0, block_end, cum_hi.astype(BF16), cum_lo.astype(BF16)).reshape(count * tm)

    first = row_tokens(0, chunk_tiles)
    rest = row_tokens(chunk_tiles, n_tiles - chunk_tiles)
    chunk_rows = chunk_tiles * tm
    row_token = [first] + [rest[c * chunk_rows:(c + 1) * chunk_rows] for c in range(MOE_CHUNKS - 1)]

    tn = _largest_tile(n, PLAN_COL_TILE, V7X_LANES)
    pos = pl.pallas_call(
        _plan_body,
        out_shape=jax.ShapeDtypeStruct((k, n), jnp.int32),
        grid_spec=pltpu.PrefetchScalarGridSpec(
            num_scalar_prefetch=1,
            grid=(n // tn,),
            in_specs=[pl.BlockSpec((k, tn), lambda i, rs: (0, i))] * 2,
            out_specs=pl.BlockSpec((k, tn), lambda i, rs: (0, i))),
        compiler_params=_cparams(("arbitrary",)),
        name="dispatch_rows",
    )(row_start, ids, ranks)
    return pos, row_token, tile_expert, tile_end[-1:].astype(jnp.int32)


def _moe(y, g_ffn, shift, scale, gates, group_fn, layer, w_router, b_router, w_gate, w_up, w_down,
         ws_gate, ws_up, ws_down, row_splits):
    n, d = y.shape
    t, ids, gate_w, ranks, counts, cum = _ada_norm(y, g_ffn, shift, scale, group_fn, w_router=w_router[layer],
                                                   b_router=b_router[layer], name="ada_norm_route")
    pos, row_token, tile_expert, n_valid = _dispatch_plan(
        ids, ranks, counts[:, 0].astype(jnp.int32), cum, MOE_ROW_TILE)
    base = _shared_expert(t, ws_gate[layer].astype(BF16), ws_up[layer].astype(BF16),
                          ws_down[layer].astype(BF16), y, gates, group_fn)
    n_tiles = tile_expert.shape[0]
    chunk_tiles = n_tiles // MOE_CHUNKS
    rows = None
    for c in range(MOE_CHUNKS):
        xs = t.at[row_token[c]].get(mode="promise_in_bounds")
        rows = _routed_experts(xs, tile_expert, n_valid, w_gate, w_up, w_down, layer,
                               c * chunk_tiles, n_tiles, rows)
    weights = gate_w.T
    outs = []
    for (r0, cnt) in row_splits:
        step = _largest_tile(cnt, COMBINE_CHUNK_ROWS, V7X_LANES)
        buf = None
        for a in range(0, cnt, step):
            picks = pos[:, r0 + a:r0 + a + step].reshape(-1)
            picked = rows.at[picks].get(mode="promise_in_bounds").reshape(TOP_K, step, d)
            buf = _combine(picked, weights, base, gates, group_fn, r0 + a, buf, cnt, a)
        outs.append(buf)
    return outs


def _rope_2d(n_tok, rot_dim):
    rows = n_tok // GRID_W
    row = jnp.broadcast_to(jnp.arange(rows, dtype=F32)[:, None], (rows, GRID_W)).reshape(-1)
    col = jnp.broadcast_to(jnp.arange(GRID_W, dtype=F32)[None, :], (rows, GRID_W)).reshape(-1)
    n_freq = rot_dim // 4
    inv = ROPE_THETA ** (-jnp.arange(n_freq, dtype=F32) / n_freq)
    ang = jnp.concatenate([row[:, None] * inv, col[:, None] * inv], axis=-1)
    return jnp.cos(ang), jnp.sin(ang)


def _rope_tables(rot_dim, n_ctx, dec_b, dec_s, n_tail):
    c, s = _rope_2d(dec_s, rot_dim)
    reps = V7X_LANES // rot_dim
    cos = jnp.concatenate([c, c] * reps, axis=1)
    sin = jnp.concatenate([-s, s] * reps, axis=1)
    if n_tail:
        fill = V7X_LANES - rot_dim
        cos = jnp.concatenate([c, c, jnp.ones((dec_s, fill), F32)], axis=1)
        sin = jnp.concatenate([-s, s, jnp.zeros((dec_s, fill), F32)], axis=1)
    ones = lambda r: jnp.ones((r, V7X_LANES), F32)
    zeros = lambda r: jnp.zeros((r, V7X_LANES), F32)
    cos = jnp.concatenate([ones(n_ctx)] + [cos] * dec_b + [ones(n_tail)], axis=0)
    sin = jnp.concatenate([zeros(n_ctx)] + [sin] * dec_b + [zeros(n_tail)], axis=0)
    return cos, sin


def _rotate_half(y, seg):
    half = seg // 2
    if seg == V7X_LANES:
        return pltpu.roll(y, half, axis=1)
    lane = lax.broadcasted_iota(jnp.int32, y.shape, 1)
    return jnp.where(lane % seg < half, pltpu.roll(y, V7X_LANES - half, axis=1), pltpu.roll(y, half, axis=1))


def _rotate_half_tail(t):
    half = MLA_ROPE // 2
    lane = lax.broadcasted_iota(jnp.int32, t.shape, 1)
    return jnp.where(lane < half, pltpu.roll(t, V7X_LANES - half, axis=1), pltpu.roll(t, half, axis=1))


def _segment_ones(seg):
    shift = seg.bit_length() - 1
    assert 1 << shift == seg
    r = lax.shift_right_logical(lax.broadcasted_iota(jnp.int32, (V7X_LANES, V7X_LANES), 0), shift)
    c = lax.shift_right_logical(lax.broadcasted_iota(jnp.int32, (V7X_LANES, V7X_LANES), 1), shift)
    return jnp.where(r == c, 1.0, 0.0).astype(BF16)


def _segment_sums(v, ones):
    hi = v.astype(BF16)
    lo = (v - hi.astype(F32)).astype(BF16)
    return jnp.dot(hi, ones, preferred_element_type=F32) + jnp.dot(lo, ones, preferred_element_type=F32)


def _seg_rms(x, seg, ones):
    return x * lax.rsqrt(_segment_sums(x * x, ones) / seg + EPS)


def _prep_body(proj_ref, *refs, plan):
    it = iter(refs)
    inputs = []
    for (_, _, kind, _) in plan:
        if kind in ("norm", "norm_f32"):
            inputs.append((next(it),))
        elif kind.startswith("heads"):
            inputs.append((next(it), next(it), next(it)))
        else:
            inputs.append(())
    outs = list(it)
    oi = 0
    for (col, width, kind, seg), ins in zip(plan, inputs):
        if kind in ("norm", "norm_f32"):
            (g_ref,) = ins
            x = proj_ref[:, col:col + width]
            y = x * lax.rsqrt(jnp.mean(x * x, axis=-1, keepdims=True) + EPS) * g_ref[...]
            outs[oi][...] = y.astype(outs[oi].dtype)
            oi += 1
        elif kind.startswith("heads"):
            g_ref, cos_ref, sin_ref = ins
            with_state = kind.endswith("+state")
            cos, sin, g = cos_ref[...], sin_ref[...], g_ref[...]
            ones = _segment_ones(seg)
            for c0 in range(0, width, V7X_LANES):
                y = _seg_rms(proj_ref[:, col + c0:col + c0 + V7X_LANES], seg, ones) * g
                if with_state:
                    outs[oi + 1][:, c0:c0 + V7X_LANES] = y
                y = y * cos + _rotate_half(y, seg) * sin
                outs[oi][:, c0:c0 + V7X_LANES] = y.astype(BF16)
            oi += 2 if with_state else 1
        else:
            outs[oi][...] = proj_ref[:, col:col + width].astype(BF16)
            oi += 1


def _prep(proj, plan, params, name):
    n, width_all = proj.shape
    tm = ROW_TILE
    in_specs = [pl.BlockSpec((tm, width_all), lambda i: (i, 0))]
    args = [proj]
    out_shape, out_specs = [], []
    for (col, width, kind, seg), ps in zip(plan, params):
        row_spec = pl.BlockSpec((tm, width), lambda i: (i, 0))
        if kind in ("norm", "norm_f32"):
            in_specs.append(pl.BlockSpec((1, width), lambda i: (0, 0)))
            args.append(ps[0].astype(F32).reshape(1, width))
            out_shape.append(jax.ShapeDtypeStruct((n, width), F32 if kind == "norm_f32" else BF16))
            out_specs.append(row_spec)
        elif kind.startswith("heads"):
            g, cos, sin = ps
            in_specs += [pl.BlockSpec((1, V7X_LANES), lambda i: (0, 0)),
                         pl.BlockSpec((tm, V7X_LANES), lambda i: (i, 0)),
                         pl.BlockSpec((tm, V7X_LANES), lambda i: (i, 0))]
            args += [jnp.tile(g.astype(F32), V7X_LANES // seg).reshape(1, V7X_LANES), cos, sin]
            out_shape.append(jax.ShapeDtypeStruct((n, width), BF16))
            out_specs.append(row_spec)
            if kind.endswith("+state"):
                out_shape.append(jax.ShapeDtypeStruct((n, width), F32))
                out_specs.append(row_spec)
        else:
            out_shape.append(jax.ShapeDtypeStruct((n, width), BF16))
            out_specs.append(row_spec)
    return pl.pallas_call(
        functools.partial(_prep_body, plan=plan),
        out_shape=tuple(out_shape),
        grid=(n // tm,),
        in_specs=in_specs,
        out_specs=tuple(out_specs),
        compiler_params=_cparams(("arbitrary",)),
        name=name,
    )(*args)


def _mla_q_body(x_ref, w_ref, g_ref, cos_ref, sin_ref, o_ref):
    acc = jnp.dot(x_ref[...], w_ref[...], preferred_element_type=F32)
    cos, sin = cos_ref[...], sin_ref[...]
    ones = _segment_ones(V7X_LANES)
    for h in range(MLA_HEADS):
        c0 = h * MLA_QK_PAD
        nope = acc[:, c0:c0 + MLA_NOPE]
        tail = acc[:, c0 + MLA_NOPE:c0 + MLA_QK_PAD]
        ss = _segment_sums(nope * nope, ones) + _segment_sums(tail * tail, ones)
        r = lax.rsqrt(ss / MLA_QK + EPS)
        o_ref[:, c0:c0 + MLA_NOPE] = (nope * r * g_ref[:, :MLA_NOPE]).astype(BF16)
        t = tail * r * g_ref[:, MLA_NOPE:]
        t = t * cos + _rotate_half_tail(t) * sin
        o_ref[:, c0 + MLA_NOPE:c0 + MLA_QK_PAD] = t.astype(BF16)


def _mla_q_up(q_lat, w_qb_p, g_qn_p, cos, sin):
    n, k = q_lat.shape
    width = w_qb_p.shape[1]
    tm = ROW_TILE
    return pl.pallas_call(
        _mla_q_body,
        out_shape=jax.ShapeDtypeStruct((n, width), BF16),
        grid=(n // tm,),
        in_specs=[pl.BlockSpec((tm, k), lambda i: (i, 0)),
                  pl.BlockSpec((k, width), lambda i: (0, 0)),
                  pl.BlockSpec((1, MLA_QK_PAD), lambda i: (0, 0)),
                  pl.BlockSpec((tm, V7X_LANES), lambda i: (i, 0)),
                  pl.BlockSpec((tm, V7X_LANES), lambda i: (i, 0))],
        out_specs=pl.BlockSpec((tm, width), lambda i: (i, 0)),
        compiler_params=_cparams(("arbitrary",)),
        name="mla_q_up",
    )(q_lat, w_qb_p, g_qn_p, cos, sin)


def _mla_kv_body(x_ref, w_ref, kr_ref, g_ref, cos_ref, sin_ref, k_ref, v_ref):
    acc = jnp.dot(x_ref[...].astype(BF16), w_ref[...], preferred_element_type=F32)
    cos, sin = cos_ref[...], sin_ref[...]
    kr = kr_ref[...]
    ones = _segment_ones(V7X_LANES)
    kr_ss = _segment_sums(kr * kr, ones)
    per = MLA_NOPE + MLA_V
    for h in range(MLA_HEADS):
        nope = acc[:, h * per:h * per + MLA_NOPE]
        r = lax.rsqrt((_segment_sums(nope * nope, ones) + kr_ss) / MLA_QK + EPS)
        c0 = h * MLA_QK_PAD
        k_ref[:, c0:c0 + MLA_NOPE] = (nope * r * g_ref[:, :MLA_NOPE]).astype(BF16)
        t = kr * r * g_ref[:, MLA_NOPE:]
        t = t * cos + _rotate_half_tail(t) * sin
        k_ref[:, c0 + MLA_NOPE:c0 + MLA_QK_PAD] = t.astype(BF16)
        v_ref[:, h * MLA_V:(h + 1) * MLA_V] = acc[:, h * per + MLA_NOPE:(h + 1) * per].astype(BF16)


def _mla_kv_up(c_kv, w_kvb, k_rope_p, g_kn_p, cos, sin):
    n, k = c_kv.shape
    tm = ROW_TILE
    return pl.pallas_call(
        _mla_kv_body,
        out_shape=(jax.ShapeDtypeStruct((n, MLA_HEADS * MLA_QK_PAD), BF16),
                   jax.ShapeDtypeStruct((n, MLA_HEADS * MLA_V), BF16)),
        grid=(n // tm,),
        in_specs=[pl.BlockSpec((tm, k), lambda i: (i, 0)),
                  pl.BlockSpec(w_kvb.shape, lambda i: (0, 0)),
                  pl.BlockSpec((tm, V7X_LANES), lambda i: (i, 0)),
                  pl.BlockSpec((1, MLA_QK_PAD), lambda i: (0, 0)),
                  pl.BlockSpec((tm, V7X_LANES), lambda i: (i, 0)),
                  pl.BlockSpec((tm, V7X_LANES), lambda i: (i, 0))],
        out_specs=(pl.BlockSpec((tm, MLA_HEADS * MLA_QK_PAD), lambda i: (i, 0)),
                   pl.BlockSpec((tm, MLA_HEADS * MLA_V), lambda i: (i, 0))),
        compiler_params=_cparams(("arbitrary",)),
        name="mla_kv_up",
    )(c_kv, w_kvb, k_rope_p, g_kn_p, cos, sin)


def _even_mixer(h, dims, cache, p, tables):
    (bp, sp, dec_b, dec_s, past) = dims
    n_ctx = bp * sp
    n = h.shape[0]
    (w_in, g_q, w_qb, g_kv, w_kvb, g_qn, g_kn, g_wq, g_wk, sink) = p
    ckv_c, krope_c, wk_c, wv_c = cache
    (cos_h, sin_h), (cos_m, sin_m) = tables["head"], tables["mla"]
    o1 = MLA_Q_RANK
    o2 = o1 + MLA_KV_RANK
    o3 = o2 + MLA_ROPE
    pad = (-(w_in.shape[1])) % V7X_LANES
    w_in_p = jnp.concatenate([w_in[:, :o2], w_in[:, o3:], w_in[:, o2:o3],
                              jnp.zeros((w_in.shape[0], pad), w_in.dtype)], axis=1).astype(BF16)
    proj = _matmul(h, w_in_p, out_dtype=F32, name="even_in_proj")
    c1 = o2 + WIN_HEADS * HEAD_DIM
    c2 = c1 + WIN_KV * HEAD_DIM
    c3 = c2 + WIN_KV * HEAD_DIM
    plan = ((0, o1, "norm", 0), (o1, MLA_KV_RANK, "norm_f32", 0),
            (o2, WIN_HEADS * HEAD_DIM, "heads", HEAD_DIM),
            (c1, WIN_KV * HEAD_DIM, "heads+state", HEAD_DIM),
            (c2, WIN_KV * HEAD_DIM, "cast", 0))
    q_lat, c_kv, wq_r, wk_r, wk, wv_b = _prep(
        proj, plan, ((g_q,), (g_kv,), (g_wq, cos_h, sin_h), (g_wk, cos_h, sin_h), ()), "even_prep")
    wv = proj[:n_ctx, c2:c3]
    k_rope_p = proj[:, c3:c3 + V7X_LANES]

    head_pad = ((0, 0), (0, 0), (0, MLA_QK_PAD - MLA_QK))
    w_qb_p = jnp.pad(w_qb.reshape(MLA_Q_RANK, MLA_HEADS, MLA_QK), head_pad)
    w_qb_p = w_qb_p.reshape(MLA_Q_RANK, MLA_HEADS * MLA_QK_PAD).astype(BF16)
    g_qn_p = jnp.pad(g_qn.astype(F32), (0, MLA_QK_PAD - MLA_QK)).reshape(1, MLA_QK_PAD)
    g_kn_p = jnp.pad(g_kn.astype(F32), (0, MLA_QK_PAD - MLA_QK)).reshape(1, MLA_QK_PAD)
    q_mla = _mla_q_up(q_lat, w_qb_p, g_qn_p, cos_m, sin_m)

    ckv_all = jnp.concatenate([c_kv, ckv_c.reshape(dec_b * past, MLA_KV_RANK)], axis=0)
    krope_cache = jnp.pad(krope_c.reshape(dec_b * past, MLA_ROPE), ((0, 0), (0, V7X_LANES - MLA_ROPE)))
    krope_all = jnp.concatenate([k_rope_p, krope_cache], axis=0)
    mk, mv = _mla_kv_up(ckv_all, w_kvb.astype(BF16), krope_all, g_kn_p, cos_m, sin_m)

    mla_scale = MLA_QK ** -0.5
    mla_cols = MLA_HEADS * MLA_V
    out = _attention(q_mla, 0, mk, mv, 0, batch=bp, seq=sp, nkv=MLA_HEADS, groups=1,
                     dq=MLA_QK_PAD, dv=MLA_V, scale=mla_scale, hb=MLA_HEADS, tq=Q_TILE_CTX,
                     out_shape=(n, mla_cols + WIN_HEADS * HEAD_DIM), name="mla_attn_ctx")
    out = _attention(q_mla, n_ctx, mk, mv, n_ctx, batch=dec_b, seq=dec_s, nkv=MLA_HEADS, groups=1,
                     dq=MLA_QK_PAD, dv=MLA_V, scale=mla_scale, hb=MLA_LAT_HEADS_PER_STEP, tq=Q_TILE_MLA_LAT,
                     k_ctx=mk, v_ctx=mv, kc_off=n, t_ctx=past, out=out, out_off=(n_ctx, 0),
                     name="mla_attn_lat")

    grp = WIN_HEADS // WIN_KV
    sink_b = jnp.broadcast_to(sink.astype(F32)[:, None], (WIN_HEADS, V7X_LANES))
    win_scale = HEAD_DIM ** -0.5
    out = _attention(wq_r, 0, wk_r, wv_b, 0, batch=bp, seq=sp, nkv=WIN_KV, groups=grp,
                     dq=HEAD_DIM, dv=HEAD_DIM, scale=win_scale, hb=WIN_KV, tq=Q_TILE_CTX, sink=sink_b,
                     out=out, out_off=(0, mla_cols), name="win_attn_ctx")
    out = _attention(wq_r, n_ctx, wk_r, wv_b, n_ctx, batch=dec_b, seq=dec_s, nkv=WIN_KV, groups=grp,
                     dq=HEAD_DIM, dv=HEAD_DIM, scale=win_scale, hb=WIN_KV, tq=Q_TILE_GQA_LAT,
                     k_ctx=wk_c.reshape(dec_b * past, WIN_KV * HEAD_DIM).astype(BF16),
                     v_ctx=wv_c.reshape(dec_b * past, WIN_KV * HEAD_DIM).astype(BF16),
                     kc_off=0, t_ctx=past, sink=sink_b, window=WINDOW,
                     out=out, out_off=(n_ctx, mla_cols), name="win_attn_lat")
    state = (c_kv[:n_ctx].reshape(bp, 1, sp, MLA_KV_RANK),
             k_rope_p[:n_ctx, :MLA_ROPE].reshape(bp, 1, sp, MLA_ROPE),
             wk[:n_ctx].reshape(bp, 1, sp, WIN_KV, HEAD_DIM),
             wv.reshape(bp, 1, sp, WIN_KV, HEAD_DIM))
    return out, state


def _odd_mixer(h, dims, cache, p, lam_init, layer, tables):
    (bp, sp, dec_b, dec_s, past) = dims
    n_ctx = bp * sp
    (w_in, g_dq, g_dk, lq1, lk1, lq2, lk2, g_sub, g_aq, g_ak) = p
    dk_c, dv_c, ak_c, av_c = cache
    (cos_h, sin_h), (cos_d, sin_d) = tables["head"], tables["diff"]
    o1 = DIFF_HEADS * HEAD_DIM
    o2 = o1 + DIFF_KV * HEAD_DIM
    o3 = o2 + DIFF_KV * HEAD_DIM
    o4 = o3 + AX_HEADS * HEAD_DIM
    o5 = o4 + AX_KV * HEAD_DIM
    proj = _matmul(h, w_in, layer=layer, out_dtype=F32, name="odd_in_proj")
    plan = ((0, o1, "heads", DIFF_D), (o1, o2 - o1, "heads+state", DIFF_D), (o2, o3 - o2, "cast", 0),
            (o3, o4 - o3, "heads", HEAD_DIM), (o4, o5 - o4, "heads+state", HEAD_DIM),
            (o5, AX_KV * HEAD_DIM, "cast", 0))
    dq_r, dk_r, dk, dv_b, aq_r, ak_r, ak, av_b = _prep(
        proj, plan, ((g_dq, cos_d, sin_d), (g_dk, cos_d, sin_d), (), (g_aq, cos_h, sin_h),
                     (g_ak, cos_h, sin_h), ()), "odd_prep")
    dv = proj[:n_ctx, o2:o3]
    av = proj[:n_ctx, o5:]

    lam = (jnp.exp(jnp.sum(lq1.astype(F32) * lk1.astype(F32)))
           - jnp.exp(jnp.sum(lq2.astype(F32) * lk2.astype(F32))) + lam_init)
    lam_b = jnp.broadcast_to(lam.astype(F32), (1, HEAD_DIM))
    g_sub_b = g_sub.astype(F32).reshape(1, HEAD_DIM)
    grp = DIFF_HEADS // DIFF_KV
    diff_kw = dict(nkv=DIFF_KV, groups=grp, dq=HEAD_DIM, dv=HEAD_DIM, scale=DIFF_D ** -0.5, hb=DIFF_KV,
                   lam=lam_b, g_sub=g_sub_b, diff_post_scale=1.0 - lam_init)
    n = n_ctx + dec_b * dec_s
    diff_cols = DIFF_HEADS * HEAD_DIM
    out = _attention(dq_r, 0, dk_r, dv_b, 0, batch=bp, seq=sp, tq=Q_TILE_CTX,
                     out_shape=(n, diff_cols + AX_HEADS * HEAD_DIM), name="diff_attn_ctx", **diff_kw)
    out = _attention(dq_r, n_ctx, dk_r, dv_b, n_ctx, batch=dec_b, seq=dec_s, tq=Q_TILE_DIFF_LAT,
                     k_ctx=dk_c.reshape(dec_b * past, DIFF_KV * HEAD_DIM).astype(BF16),
                     v_ctx=dv_c.reshape(dec_b * past, DIFF_KV * HEAD_DIM).astype(BF16),
                     kc_off=0, t_ctx=past, out=out, out_off=(n_ctx, 0), name="diff_attn_lat", **diff_kw)
    agrp = AX_HEADS // AX_KV
    ax_kw = dict(nkv=AX_KV, groups=agrp, dq=HEAD_DIM, dv=HEAD_DIM, scale=HEAD_DIM ** -0.5, hb=AX_KV)
    out = _attention(aq_r, 0, ak_r, av_b, 0, batch=bp, seq=sp, tq=Q_TILE_CTX,
                     out=out, out_off=(0, diff_cols), name="ax_attn_ctx", **ax_kw)
    out = _attention(aq_r, n_ctx, ak_r, av_b, n_ctx, batch=dec_b, seq=dec_s, tq=Q_TILE_GQA_LAT,
                     k_ctx=ak_c.reshape(dec_b * past, AX_KV * HEAD_DIM).astype(BF16),
                     v_ctx=av_c.reshape(dec_b * past, AX_KV * HEAD_DIM).astype(BF16),
                     kc_off=0, t_ctx=past, out=out, out_off=(n_ctx, diff_cols), name="ax_attn_lat", **ax_kw)
    state = (dk[:n_ctx].reshape(bp, 1, sp, DIFF_KV, 2, DIFF_D),
             dv.reshape(bp, 1, sp, DIFF_KV, HEAD_DIM),
             ak[:n_ctx].reshape(bp, 1, sp, AX_KV, HEAD_DIM),
             av.reshape(bp, 1, sp, AX_KV, HEAD_DIM))
    return out, state


def kernel(x_prompt, x_sample, cache_mla_ckv, cache_mla_krope, cache_win_k, cache_win_v, cache_diff_k, cache_diff_v, cache_ax_k, cache_ax_v, c, c_ctx, w_mod, b_mod, g_norm_mix, g_norm_ffn, w_in_even, g_mla_q, w_mla_qb, g_mla_kv, w_mla_kvb, g_mla_qn, g_mla_kn, g_win_qn, g_win_kn, win_sink, w_out_even, w_in_odd, g_diff_qn, g_diff_kn, diff_lq1, diff_lk1, diff_lq2, diff_lk2, g_diff_sub, g_ax_qn, g_ax_kn, w_out_odd, w_router, b_router, w_exp_gate, w_exp_up, w_exp_down, w_sh_gate, w_sh_up, w_sh_down):
    bp, sp, d = x_prompt.shape
    dec_b, dec_s, _ = x_sample.shape
    depth = w_mod.shape[0]
    n_ctx = bp * sp
    n = n_ctx + dec_b * dec_s
    dims = (bp, sp, dec_b, dec_s, cache_mla_ckv.shape[2])
    group_fn = lambda tm: _group_index_fn(tm, n_ctx, dec_s)
    n_groups = 1 + dec_b

    y = jnp.concatenate([x_prompt.reshape(n_ctx, d), x_sample.reshape(dec_b * dec_s, d)], axis=0)

    cond = jnp.concatenate([c_ctx[None], c, jnp.zeros((V7X_SUBLANES - n_groups % V7X_SUBLANES, d), F32)], axis=0)
    cond = jax.nn.silu(cond)

    past = cache_mla_ckv.shape[2]
    tables = {"head": _rope_tables(HEAD_DIM, n_ctx, dec_b, dec_s, 0),
              "diff": _rope_tables(DIFF_D, n_ctx, dec_b, dec_s, 0),
              "mla": _rope_tables(MLA_ROPE, n_ctx, dec_b, dec_s, dec_b * past)}

    states_even, states_odd = [], []
    for l in range(depth):
        i = l // 2
        mod = _matmul(cond, w_mod, layer=l, out_dtype=F32, bias=b_mod.reshape(depth, 1, 6 * d),
                      name="modulation")
        mod = mod[:n_groups].reshape(n_groups, 6, 1, d)
        sh1, sc1, g1, sh2, sc2, g2 = (mod[:, j] for j in range(6))
        h = _ada_norm(y, g_norm_mix[l][None], sh1, sc1, group_fn, name="ada_norm_mix")
        if l % 2 == 0:
            pe = (w_in_even[i], g_mla_q[i], w_mla_qb[i], g_mla_kv[i], w_mla_kvb[i], g_mla_qn[i],
                  g_mla_kn[i], g_win_qn[i], g_win_kn[i], win_sink[i])
            cache = (cache_mla_ckv[:, i], cache_mla_krope[:, i], cache_win_k[:, i], cache_win_v[:, i])
            out, state = _even_mixer(h, dims, cache, pe, tables)
            states_even.append(state)
            w_out = w_out_even
        else:
            po = (w_in_odd, g_diff_qn[i], g_diff_kn[i], diff_lq1[i], diff_lk1[i], diff_lq2[i],
                  diff_lk2[i], g_diff_sub[i], g_ax_qn[i], g_ax_kn[i])
            cache = (cache_diff_k[:, i], cache_diff_v[:, i], cache_ax_k[:, i], cache_ax_v[:, i])
            lam_init = 0.8 - 0.6 * math.exp(-0.3 * l)
            out, state = _odd_mixer(h, dims, cache, po, lam_init, i, tables)
            states_odd.append(state)
            w_out = w_out_odd
        y = _matmul(out, w_out[i].astype(BF16), out_dtype=F32, resid=y, gates=g1, group_fn=group_fn,
                    name="mixer_out_proj")
        splits = [(0, n)] if l + 1 < depth else [(0, n_ctx), (n_ctx, n - n_ctx)]
        outs = _moe(y, g_norm_ffn[l][None], sh2, sc2, g2, group_fn, l, w_router, b_router,
                    w_exp_gate, w_exp_up, w_exp_down, w_sh_gate, w_sh_up, w_sh_down, splits)
        y = outs[0]

    yp = outs[0].reshape(bp, sp, d)
    ys = outs[1].reshape(dec_b, dec_s, d)
    even = tuple(jnp.concatenate([s[j] for s in states_even], axis=1) for j in range(4))
    odd = tuple(jnp.concatenate([s[j] for s in states_odd], axis=1) for j in range(4))
    return (yp, ys) + even + odd
```

```python
import functools
import math

import jax
import jax.numpy as jnp
from jax import lax
from jax.experimental import pallas as pl
from jax.experimental.pallas import tpu as pltpu

F32 = jnp.float32
BF16 = jnp.bfloat16

GRID_W = 64
ROPE_THETA = 10000.0
EPS = 1e-6
NEG_INF = -1e30
LOG2E = math.log2(math.e)
HEAD_DIM = 128
MLA_HEADS = 16
MLA_Q_RANK = 768
MLA_KV_RANK = 512
MLA_NOPE = 128
MLA_ROPE = 64
MLA_V = 128
MLA_QK = MLA_NOPE + MLA_ROPE
MLA_QK_PAD = 256
WIN_HEADS = 16
WIN_KV = 4
WINDOW = 128
DIFF_HEADS = 16
DIFF_KV = 4
DIFF_D = HEAD_DIM // 2
AX_HEADS = 16
AX_KV = 4
N_EXPERTS = 64
N_EXPERT_GROUPS = 8
TOPK_GROUPS = 4
TOP_K = 8
ROUTED_SCALE = 2.5

V7X_LANES = 128
V7X_SUBLANES = 8
V7X_VMEM_LIMIT_BYTES = 56 * 1024 * 1024

ROW_TILE = 256
MM_ROW_TILE = 512
MM_MAX_COL_TILE = 2048
MM_WEIGHT_TILE_BYTES = 32 * 1024 * 1024
COMBINE_ROW_TILE = 128
PLAN_COL_TILE = 2048
ATTN_KEY_CHUNK = 1024
Q_TILE_CTX = 256
Q_TILE_MLA_LAT = 512
Q_TILE_GQA_LAT = 256
Q_TILE_DIFF_LAT = 128
MLA_LAT_HEADS_PER_STEP = 4
MAX_ROW_TOKEN_TILES_PER_STEP = 8
COUNT_SPLIT = 64
MOE_ROW_TILE = 256
MOE_CHUNKS = 8
COMBINE_CHUNK_ROWS = 1024


def _cparams(sem):
    return pltpu.CompilerParams(dimension_semantics=sem, vmem_limit_bytes=V7X_VMEM_LIMIT_BYTES)


def _largest_tile(n, cap, step):
    if n <= cap:
        return n
    t = (cap // step) * step
    while t >= step:
        if n % t == 0:
            return t
        t -= step
    raise ValueError(f"no tile for {n} under {cap}")


def _group_index_fn(tm, n_ctx_rows, dec_rows):
    assert n_ctx_rows % tm == 0 and dec_rows % tm == 0
    ctx_tiles = n_ctx_rows // tm
    per_dec = dec_rows // tm

    def fn(i):
        return jnp.where(i < ctx_tiles, 0, 1 + (jnp.maximum(i - ctx_tiles, 0)) // per_dec)

    return fn


def _mm_body(*refs, has_bias, has_resid):
    x_ref, w_ref = refs[0], refs[1]
    idx = 2
    if has_bias:
        bias_ref = refs[idx]
        idx += 1
    if has_resid:
        resid_ref, gate_ref = refs[idx], refs[idx + 1]
        idx += 2
    o_ref = refs[idx]
    if w_ref.dtype == BF16:
        wb_ref = w_ref
    else:
        wb_ref = refs[idx + 1]

        @pl.when(pl.program_id(1) == 0)
        def _():
            wb_ref[...] = w_ref[...].astype(BF16)

    acc = jnp.dot(x_ref[...].astype(BF16), wb_ref[...], preferred_element_type=F32)
    if has_bias:
        acc = acc + bias_ref[...]
    if has_resid:
        acc = resid_ref[...] + gate_ref[...] * acc
    o_ref[...] = acc.astype(o_ref.dtype)


def _matmul(x, w, *, out_dtype, layer=None, bias=None, resid=None, gates=None, group_fn=None, name):
    m, k = x.shape
    n = w.shape[-1]
    tm = _largest_tile(m, MM_ROW_TILE, V7X_SUBLANES)
    precast = w.dtype == BF16
    col_bytes = k * (4 if precast else 10)
    budget = MM_WEIGHT_TILE_BYTES // 2 if precast else MM_WEIGHT_TILE_BYTES
    tn_cap = max(V7X_LANES, min(MM_MAX_COL_TILE, budget // col_bytes))
    tn = _largest_tile(n, tn_cap, V7X_LANES)
    grid = (n // tn, m // tm)
    if layer is None:
        w_spec = pl.BlockSpec((k, tn), lambda j, i: (0, j))
    else:
        w_spec = pl.BlockSpec((None, k, tn), lambda j, i: (layer, 0, j))
    in_specs = [pl.BlockSpec((tm, k), lambda j, i: (i, 0)), w_spec]
    args = [x, w]
    if bias is not None:
        in_specs.append(pl.BlockSpec((None, 1, tn), lambda j, i: (layer, 0, j)))
        args.append(bias)
    if resid is not None:
        gfn = group_fn(tm)
        in_specs.append(pl.BlockSpec((tm, tn), lambda j, i: (i, j)))
        in_specs.append(pl.BlockSpec((None, 1, tn), lambda j, i: (gfn(i), 0, j)))
        args += [resid, gates]
    return pl.pallas_call(
        functools.partial(_mm_body, has_bias=bias is not None, has_resid=resid is not None),
        out_shape=jax.ShapeDtypeStruct((m, n), out_dtype),
        grid=grid,
        in_specs=in_specs,
        out_specs=pl.BlockSpec((tm, tn), lambda j, i: (i, j)),
        scratch_shapes=[] if precast else [pltpu.VMEM((k, tn), BF16)],
        compiler_params=_cparams(("arbitrary", "arbitrary")),
        name=name,
    )(*args)


def _first_max(cur, sub, limit):
    m = jnp.max(cur, axis=0, keepdims=True)
    first = jnp.min(jnp.where(cur == m, sub, limit), axis=0, keepdims=True)
    return m, first, sub == first


def _route_columns(logits, bias, tri, carry):
    e, t = logits.shape
    per = e // N_EXPERT_GROUPS
    scores = jax.nn.sigmoid(logits)
    biased = scores + bias
    neg = -jnp.inf
    gsub = lax.broadcasted_iota(jnp.int32, (per, t), 0)
    grp = []
    for g in range(N_EXPERT_GROUPS):
        xg = biased[g * per:(g + 1) * per]
        m1, _, hit = _first_max(xg, gsub, per)
        m2 = jnp.max(jnp.where(hit, neg, xg), axis=0, keepdims=True)
        grp.append(m1 + m2)
    grp = jnp.concatenate(grp, axis=0)
    nsub = lax.broadcasted_iota(jnp.int32, (N_EXPERT_GROUPS, t), 0)
    gsel = jnp.zeros((N_EXPERT_GROUPS, t), F32)
    for _ in range(TOPK_GROUPS):
        _, _, hit = _first_max(grp, nsub, N_EXPERT_GROUPS)
        gsel = jnp.where(hit, 1.0, gsel)
        grp = jnp.where(hit, neg, grp)
    emask = jnp.concatenate([jnp.broadcast_to(gsel[g:g + 1], (per, t)) for g in range(N_EXPERT_GROUPS)], axis=0)
    cur = jnp.where(emask > 0.5, biased, neg)
    esub = lax.broadcasted_iota(jnp.int32, (e, t), 0)
    sel = jnp.zeros((e, t), F32)
    ids, ws, hits = [], [], []
    for _ in range(TOP_K):
        _, first, hit = _first_max(cur, esub, e)
        ids.append(first)
        ws.append(jnp.sum(jnp.where(hit, scores, 0.0), axis=0, keepdims=True))
        hits.append(hit)
        sel = jnp.where(hit, 1.0, sel)
        cur = jnp.where(hit, neg, cur)
    w = jnp.concatenate(ws, axis=0)
    gates = w / jnp.sum(w, axis=0, keepdims=True) * ROUTED_SCALE
    rank_all = jnp.dot(sel.astype(BF16), tri, preferred_element_type=F32) + carry
    ranks = [jnp.sum(jnp.where(hit, rank_all, 0.0), axis=0, keepdims=True) for hit in hits]
    counts = jnp.sum(sel, axis=1, keepdims=True)
    return (jnp.concatenate(ids, axis=0), gates, jnp.concatenate(ranks, axis=0).astype(jnp.int32), counts,
            rank_all + sel)


def _adanorm_body(x_ref, g_ref, shift_ref, scale_ref, *rest, has_router):
    x = x_ref[...]
    y = x * lax.rsqrt(jnp.mean(x * x, axis=-1, keepdims=True) + EPS) * g_ref[...]
    t = y * (1.0 + scale_ref[...]) + shift_ref[...]
    if has_router:
        (whi_ref, wlo_ref, br_ref, tri_ref, o_ref, ids_ref, gate_ref, rank_ref, cnt_ref, cum_ref,
         carry_ref) = rest

        @pl.when(pl.program_id(0) == 0)
        def _():
            carry_ref[...] = jnp.zeros_like(carry_ref)

        t_hi = t.astype(BF16)
        t_lo = (t - t_hi.astype(F32)).astype(BF16)
        nt = lambda a, b: lax.dot_general(a, b, (((1,), (1,)), ((), ())), preferred_element_type=F32)
        logits = nt(whi_ref[...], t_hi) + (nt(wlo_ref[...], t_hi) + nt(whi_ref[...], t_lo))
        ids, gates, ranks, counts, cum = _route_columns(logits, br_ref[...], tri_ref[...], carry_ref[:, :1])
        cum_ref[...] = cum
        ids_ref[...] = ids
        gate_ref[...] = gates
        rank_ref[...] = ranks
        carry_ref[...] = carry_ref[...] + counts
        cnt_ref[...] = carry_ref[...]
    else:
        (o_ref,) = rest
    o_ref[...] = t.astype(o_ref.dtype)


def _ada_norm(x, g, shift, scale, group_fn, *, w_router=None, b_router=None, name):
    n, d = x.shape
    tm = ROW_TILE
    gfn = group_fn(tm)
    in_specs = [pl.BlockSpec((tm, d), lambda i: (i, 0)),
                pl.BlockSpec((1, d), lambda i: (0, 0)),
                pl.BlockSpec((None, 1, d), lambda i: (gfn(i), 0, 0)),
                pl.BlockSpec((None, 1, d), lambda i: (gfn(i), 0, 0))]
    args = [x, g, shift, scale]
    out_shape = jax.ShapeDtypeStruct((n, d), BF16)
    out_specs = pl.BlockSpec((tm, d), lambda i: (i, 0))
    scratch = []
    if w_router is not None:
        e = w_router.shape[1]
        tri = (lax.broadcasted_iota(jnp.int32, (tm, tm), 0) < lax.broadcasted_iota(jnp.int32, (tm, tm), 1))
        in_specs += [pl.BlockSpec((e, d), lambda i: (0, 0)),
                     pl.BlockSpec((e, d), lambda i: (0, 0)),
                     pl.BlockSpec((e, tm), lambda i: (0, 0)),
                     pl.BlockSpec((tm, tm), lambda i: (0, 0))]
        w_t = w_router.T.astype(F32)
        w_hi = w_t.astype(BF16)
        w_lo = (w_t - w_hi.astype(F32)).astype(BF16)
        args += [w_hi, w_lo, jnp.broadcast_to(b_router.astype(F32)[:, None], (e, tm)), tri.astype(BF16)]
        col = lambda rows, dt: (jax.ShapeDtypeStruct((rows, n), dt), pl.BlockSpec((rows, tm), lambda i: (0, i)))
        extra = [col(TOP_K, jnp.int32), col(TOP_K, F32), col(TOP_K, jnp.int32),
                 (jax.ShapeDtypeStruct((e, V7X_LANES), F32), pl.BlockSpec((e, V7X_LANES), lambda i: (0, 0))),
                 col(e, F32)]
        out_shape = (out_shape,) + tuple(s for s, _ in extra)
        out_specs = (out_specs,) + tuple(b for _, b in extra)
        scratch = [pltpu.VMEM((e, V7X_LANES), F32)]
    return pl.pallas_call(
        functools.partial(_adanorm_body, has_router=w_router is not None),
        out_shape=out_shape,
        grid=(n // tm,),
        in_specs=in_specs,
        out_specs=out_specs,
        scratch_shapes=scratch,
        compiler_params=_cparams(("arbitrary",)),
        name=name,
    )(*args)


def _attn_body(*refs, hb, groups, dq, dv, tq, tk, scale, t_new, t_ctx, window, has_sink,
               diff_post_scale, has_out_buf):
    it = iter(refs)
    q_ref, kn_ref, vn_ref = next(it), next(it), next(it)
    kc_ref = vc_ref = sink_ref = lam_ref = gsub_ref = None
    if t_ctx:
        kc_ref, vc_ref = next(it), next(it)
    if has_sink:
        sink_ref = next(it)
    diff = diff_post_scale is not None
    if diff:
        lam_ref, gsub_ref = next(it), next(it)
    if has_out_buf:
        next(it)
    o_ref = next(it)

    h = pl.program_id(1)
    i = pl.program_id(2)
    nstack = 2 * groups if diff else groups
    rows = nstack * tq

    def stack_rows(parts):
        return parts[0] if len(parts) == 1 else jnp.concatenate(parts, axis=0)

    if window is not None:
        wk = min(t_new, tq + 2 * window)
        if wk == t_new:
            wstart = 0
        else:
            wstart = pl.multiple_of(jnp.clip(i * tq - window, 0, t_new - wk), V7X_LANES)
        qi = i * tq + lax.broadcasted_iota(jnp.int32, (tq, wk), 0)
        kj = wstart + lax.broadcasted_iota(jnp.int32, (tq, wk), 1)
        wmask = stack_rows([jnp.abs(qi - kj) <= window] * nstack)

    for j in range(hb):
        qj = q_ref[:, j * groups * dq:(j + 1) * groups * dq]
        parts = [qj[:, g * dq:(g + 1) * dq] for g in range(groups)]
        if diff:
            lo = lax.broadcasted_iota(jnp.int32, (tq, dq), 1) < dq // 2
            zero = jnp.zeros((tq, dq), qj.dtype)
            parts = [jnp.where(lo, p, zero) for p in parts] + [jnp.where(lo, zero, p) for p in parts]
        qs = stack_rows(parts)

        if has_sink:
            sinks = []
            for g in range(groups):
                hh = (h * hb + j) * groups + g
                sinks.append(jnp.broadcast_to(sink_ref[pl.ds(hh, 1), :][:, :1], (tq, 1)))
            m = stack_rows(sinks) * LOG2E
            l = jnp.ones((rows, 1), F32)
        else:
            m = jnp.full((rows, 1), NEG_INF, F32)
            l = jnp.zeros((rows, 1), F32)
        acc = jnp.zeros((rows, dv), F32)

        def step(carry, kc, vc, mask):
            m, l, acc = carry
            s = lax.dot_general(qs, kc, (((1,), (1,)), ((), ())), preferred_element_type=F32) * (scale * LOG2E)
            if mask is not None:
                s = jnp.where(mask, s, NEG_INF)
            m_new = jnp.maximum(m, jnp.max(s, axis=-1, keepdims=True))
            alpha = jnp.exp2(m - m_new)
            p = jnp.exp2(s - m_new)
            l = alpha * l + jnp.sum(p, axis=-1, keepdims=True)
            acc = alpha * acc + jnp.dot(p.astype(BF16), vc, preferred_element_type=F32)
            return m_new, l, acc

        carry = (m, l, acc)
        kcols = slice(j * dq, (j + 1) * dq)
        vcols = slice(j * dv, (j + 1) * dv)
        if window is not None:
            carry = step(carry, kn_ref[pl.ds(wstart, wk), kcols], vn_ref[pl.ds(wstart, wk), vcols], wmask)
        else:
            for c in range(t_new // tk):
                carry = step(carry, kn_ref[c * tk:(c + 1) * tk, kcols], vn_ref[c * tk:(c + 1) * tk, vcols], None)
        if t_ctx:
            tkc = min(tk, t_ctx)
            for c in range(t_ctx // tkc):
                carry = step(carry, kc_ref[c * tkc:(c + 1) * tkc, kcols], vc_ref[c * tkc:(c + 1) * tkc, vcols], None)
        m, l, acc = carry
        o = acc / l
        if diff:
            half = groups * tq
            d = o[:half] - lam_ref[...] * o[half:]
            o = (d * lax.rsqrt(jnp.mean(d * d, axis=-1, keepdims=True) + EPS) * gsub_ref[...]) * diff_post_scale
        outs = [o[g * tq:(g + 1) * tq] for g in range(groups)]
        oj = outs[0] if groups == 1 else jnp.concatenate(outs, axis=1)
        o_ref[:, j * groups * dv:(j + 1) * groups * dv] = oj.astype(o_ref.dtype)


def _attention(q, q_off, k_new, v_new, kn_off, *, batch, seq, nkv, groups, dq, dv, scale, hb, tq,
               k_ctx=None, v_ctx=None, kc_off=0, t_ctx=0, sink=None, window=None,
               lam=None, g_sub=None, diff_post_scale=None, out=None, out_shape=None, out_off=(0, 0), name):
    t_new = seq
    tq = min(tq, seq)
    tk = min(ATTN_KEY_CHUNK, t_new)
    assert seq % tq == 0 and t_new % tk == 0 and nkv % hb == 0
    assert q_off % tq == 0 and kn_off % t_new == 0
    qb, nb = q_off // tq, kn_off // t_new
    spt = seq // tq
    in_specs = [pl.BlockSpec((tq, hb * groups * dq), lambda b, h, i: (qb + b * spt + i, h)),
                pl.BlockSpec((t_new, hb * dq), lambda b, h, i: (nb + b, h)),
                pl.BlockSpec((t_new, hb * dv), lambda b, h, i: (nb + b, h))]
    args = [q, k_new, v_new]
    if t_ctx:
        assert kc_off % t_ctx == 0
        cb = kc_off // t_ctx
        in_specs += [pl.BlockSpec((t_ctx, hb * dq), lambda b, h, i: (cb + b, h)),
                     pl.BlockSpec((t_ctx, hb * dv), lambda b, h, i: (cb + b, h))]
        args += [k_ctx, v_ctx]
    if sink is not None:
        in_specs.append(pl.BlockSpec(sink.shape, lambda b, h, i: (0, 0)))
        args.append(sink)
    if diff_post_scale is not None:
        in_specs += [pl.BlockSpec((1, dv), lambda b, h, i: (0, 0))] * 2
        args += [lam, g_sub]
    aliases = {}
    if out is not None:
        out_shape = out.shape
        in_specs.append(pl.BlockSpec(memory_space=pl.ANY))
        args.append(out)
        aliases = {len(args) - 1: 0}
    wblk = hb * groups * dv
    assert out_off[0] % tq == 0 and out_off[1] % wblk == 0
    ob, oc = out_off[0] // tq, out_off[1] // wblk
    body = functools.partial(
        _attn_body, hb=hb, groups=groups, dq=dq, dv=dv, tq=tq, tk=tk, scale=scale, t_new=t_new,
        t_ctx=t_ctx, window=window, has_sink=sink is not None, diff_post_scale=diff_post_scale,
        has_out_buf=out is not None)
    return pl.pallas_call(
        body,
        out_shape=jax.ShapeDtypeStruct(out_shape, BF16),
        grid=(batch, nkv // hb, spt),
        in_specs=in_specs,
        out_specs=pl.BlockSpec((tq, wblk), lambda b, h, i: (ob + b * spt + i, oc + h)),
        input_output_aliases=aliases,
        compiler_params=_cparams(("arbitrary", "arbitrary", "arbitrary")),
        name=name,
    )(*args)


def _experts_body(te_ref, nv_ref, x_ref, wg_ref, wu_ref, wd_ref, *rest, tile0):
    o_ref, wgb, wub, wdb = rest[-4:]
    t = pl.program_id(0)
    g = tile0 + t
    valid = g < nv_ref[0]
    prev = te_ref[jnp.maximum(g - 1, 0)]
    first = jnp.logical_or(t == 0, te_ref[g] != prev)

    @pl.when(jnp.logical_and(first, valid))
    def _():
        wgb[...] = wg_ref[...].astype(BF16)
        wub[...] = wu_ref[...].astype(BF16)
        wdb[...] = wd_ref[...].astype(BF16)

    @pl.when(valid)
    def _():
        x = x_ref[...]
        a = jnp.dot(x, wgb[...], preferred_element_type=F32)
        u = jnp.dot(x, wub[...], preferred_element_type=F32)
        hcur = a * jax.nn.sigmoid(a) * u
        o_ref[...] = jnp.dot(hcur.astype(BF16), wdb[...], preferred_element_type=F32).astype(o_ref.dtype)

    @pl.when(jnp.logical_not(valid))
    def _():
        o_ref[...] = jnp.zeros_like(o_ref)


def _routed_experts(xs, tile_expert, n_valid, w_gate, w_up, w_down, layer, tile0, n_tiles, rows_buf):
    r, d = xs.shape
    ff = w_gate.shape[-1]
    tm = MOE_ROW_TILE
    chunk_tiles = r // tm

    def x_blk(t, te, nv):
        last = jnp.clip(nv[0] - tile0 - 1, 0, chunk_tiles - 1)
        return (jnp.minimum(t, last), 0)

    def o_blk(t, te, nv):
        return (jnp.where(tile0 + t < nv[0], tile0 + t, n_tiles), 0)

    w_blk = lambda t, te, nv: (layer, te[tile0 + t], 0, 0)
    in_specs = [pl.BlockSpec((tm, d), x_blk),
                pl.BlockSpec((None, None, d, ff), w_blk),
                pl.BlockSpec((None, None, d, ff), w_blk),
                pl.BlockSpec((None, None, ff, d), w_blk)]
    args = [tile_expert, n_valid, xs, w_gate, w_up, w_down]
    aliases = {}
    if rows_buf is not None:
        in_specs.append(pl.BlockSpec(memory_space=pl.ANY))
        args.append(rows_buf)
        aliases = {len(args) - 1: 0}
    grid_spec = pltpu.PrefetchScalarGridSpec(
        num_scalar_prefetch=2,
        grid=(chunk_tiles,),
        in_specs=in_specs,
        out_specs=pl.BlockSpec((tm, d), o_blk),
        scratch_shapes=[pltpu.VMEM((d, ff), BF16), pltpu.VMEM((d, ff), BF16), pltpu.VMEM((ff, d), BF16)],
    )
    return pl.pallas_call(
        functools.partial(_experts_body, tile0=tile0),
        out_shape=jax.ShapeDtypeStruct(((n_tiles + 1) * tm, d), BF16),
        grid_spec=grid_spec,
        input_output_aliases=aliases,
        compiler_params=_cparams(("arbitrary",)),
        name="routed_experts",
    )(*args)


def _shared_body(t_ref, wg_ref, wu_ref, wd_ref, resid_ref, gate_ref, o_ref):
    x = t_ref[...]
    a = jnp.dot(x, wg_ref[...], preferred_element_type=F32)
    u = jnp.dot(x, wu_ref[...], preferred_element_type=F32)
    hcur = (a * jax.nn.sigmoid(a) * u).astype(BF16)
    shared = jnp.dot(hcur, wd_ref[...], preferred_element_type=F32)
    o_ref[...] = resid_ref[...] + gate_ref[...] * shared


def _shared_expert(t, ws_gate, ws_up, ws_down, resid, gates, group_fn):
    n, d = t.shape
    ff = ws_gate.shape[1]
    tm = ROW_TILE
    gfn = group_fn(tm)
    return pl.pallas_call(
        _shared_body,
        out_shape=jax.ShapeDtypeStruct((n, d), F32),
        grid=(n // tm,),
        in_specs=[pl.BlockSpec((tm, d), lambda i: (i, 0)),
                  pl.BlockSpec((d, ff), lambda i: (0, 0)),
                  pl.BlockSpec((d, ff), lambda i: (0, 0)),
                  pl.BlockSpec((ff, d), lambda i: (0, 0)),
                  pl.BlockSpec((tm, d), lambda i: (i, 0)),
                  pl.BlockSpec((None, 1, d), lambda i: (gfn(i), 0, 0))],
        out_specs=pl.BlockSpec((tm, d), lambda i: (i, 0)),
        compiler_params=_cparams(("arbitrary",)),
        name="shared_expert",
    )(t, ws_gate, ws_up, ws_down, resid, gates)


def _combine_body(rows_ref, w_ref, base_ref, gate_ref, *rest):
    o_ref = rest[-1]
    w = w_ref[...]
    acc = w[:, 0:1] * rows_ref[0].astype(F32)
    for c in range(1, rows_ref.shape[0]):
        acc = acc + w[:, c:c + 1] * rows_ref[c].astype(F32)
    o_ref[...] = base_ref[...] + gate_ref[...] * acc


def _combine(picked, weights, base, gates, group_fn, row_start, out, out_rows, out_off):
    d = base.shape[1]
    k, count, _ = picked.shape
    tm = COMBINE_ROW_TILE
    assert row_start % tm == 0 and count % tm == 0 and out_off % tm == 0
    r0, o0 = row_start // tm, out_off // tm
    gfn = group_fn(tm)
    in_specs = [pl.BlockSpec((k, tm, d), lambda i: (0, i, 0)),
                pl.BlockSpec((tm, k), lambda i: (r0 + i, 0)),
                pl.BlockSpec((tm, d), lambda i: (r0 + i, 0)),
                pl.BlockSpec((None, 1, d), lambda i: (gfn(r0 + i), 0, 0))]
    args = [picked, weights, base, gates]
    aliases = {}
    if out is not None:
        in_specs.append(pl.BlockSpec(memory_space=pl.ANY))
        args.append(out)
        aliases = {len(args) - 1: 0}
    return pl.pallas_call(
        _combine_body,
        out_shape=jax.ShapeDtypeStruct((out_rows, d), F32),
        grid=(count // tm,),
        in_specs=in_specs,
        out_specs=pl.BlockSpec((tm, d), lambda i: (o0 + i, 0)),
        input_output_aliases=aliases,
        compiler_params=_cparams(("arbitrary",)),
        name="moe_combine",
    )(*args)


def _plan_body(rs_ref, ids_ref, rank_ref, pos_ref):
    ids = ids_ref[...]
    base = jnp.zeros_like(ids)
    for e in range(N_EXPERTS):
        base = jnp.where(ids == e, rs_ref[e], base)
    pos_ref[...] = base + rank_ref[...]


def _row_token_body(te_ref, j0_ref, end_ref, hi_ref, lo_ref, o_ref, *, tm, n_tok, tile0):
    nb = end_ref.shape[-1]
    sub = lax.broadcasted_iota(jnp.int32, (tm, 1), 0)
    blk = lax.broadcasted_iota(jnp.int32, (tm, nb), 1).astype(F32)
    for s in range(o_ref.shape[0]):
        t = tile0 + pl.program_id(0) * o_ref.shape[0] + s
        e = te_ref[t]
        j = (j0_ref[t] + sub).astype(F32)
        full = jnp.sum(jnp.where(end_ref[e] <= j, 1.0, 0.0), axis=1, keepdims=True)
        pick = jnp.where(blk == full, 1.0, 0.0).astype(BF16)
        edge = (COUNT_SPLIT * jnp.dot(pick, hi_ref[e], preferred_element_type=F32)
                + jnp.dot(pick, lo_ref[e], preferred_element_type=F32))
        inside = jnp.sum(jnp.where(edge <= j, 1.0, 0.0), axis=1, keepdims=True)
        cnt = jnp.where(full >= nb, float(n_tok), full * V7X_LANES + inside)
        spare = lax.rem(t * tm, n_tok) + sub
        spare = jnp.where(spare >= n_tok, spare - n_tok, spare).astype(F32)
        tok = jnp.where(cnt >= n_tok, spare, cnt)
        o_ref[s] = jnp.broadcast_to(tok, (tm, V7X_LANES)).T[0:1, :].astype(jnp.int32)


def _dispatch_plan(ids, ranks, counts, cum, tm):
    k, n = ids.shape
    e = counts.shape[0]
    n_rows = n * k + e * tm
    n_tiles = n_rows // tm
    tiles_e = (counts + tm - 1) // tm
    tile_end = jnp.cumsum(tiles_e)
    row_start = ((tile_end - tiles_e) * tm).astype(jnp.int32)
    tile_ids = jnp.arange(n_tiles, dtype=jnp.int32)
    tile_expert = jnp.sum((tile_end[None, :] <= tile_ids[:, None]).astype(jnp.int32), axis=1)
    tile_expert = jnp.minimum(tile_expert, e - 1)

    assert n % V7X_LANES == 0 and tm <= n <= COUNT_SPLIT * 256
    nb = n // V7X_LANES
    cum_b = cum.reshape(e, nb, V7X_LANES)
    j0 = tile_ids * tm - row_start[tile_expert]
    block_end = cum_b[:, :, -1].reshape(e, 1, nb)
    cum_hi = jnp.floor(cum_b / COUNT_SPLIT)
    cum_lo = cum_b - COUNT_SPLIT * cum_hi
    chunk_tiles = n_tiles // MOE_CHUNKS
    per_step = max(c for c in range(1, MAX_ROW_TOKEN_TILES_PER_STEP + 1) if chunk_tiles % c == 0)

    def row_tokens(tile0, count):
        steps = count // per_step
        whole = lambda shape: pl.BlockSpec(shape, lambda t, *_: (0,) * len(shape))
        return pl.pallas_call(
            functools.partial(_row_token_body, tm=tm, n_tok=n, tile0=tile0),
            out_shape=jax.ShapeDtypeStruct((count, 1, tm), jnp.int32),
            grid_spec=pltpu.PrefetchScalarGridSpec(
                num_scalar_prefetch=2,
                grid=(steps,),
                in_specs=[whole((e, 1, nb)), whole((e, nb, V7X_LANES)), whole((e, nb, V7X_LANES))],
                out_specs=pl.BlockSpec((per_step, 1, tm), lambda t, *_: (t, 0, 0))),
            compiler_params=_cparams(("arbitrary",)),
            name="row_tokens",
        )(tile_expert, j0, block_end, cum_hi.astype(BF16), cum_lo.astype(BF16)).reshape(count * tm)

    first = row_tokens(0, chunk_tiles)
    rest = row_tokens(chunk_tiles, n_tiles - chunk_tiles)
    chunk_rows = chunk_tiles * tm
    row_token = [first] + [rest[c * chunk_rows:(c + 1) * chunk_rows] for c in range(MOE_CHUNKS - 1)]

    tn = _largest_tile(n, PLAN_COL_TILE, V7X_LANES)
    pos = pl.pallas_call(
        _plan_body,
        out_shape=jax.ShapeDtypeStruct((k, n), jnp.int32),
        grid_spec=pltpu.PrefetchScalarGridSpec(
            num_scalar_prefetch=1,
            grid=(n // tn,),
            in_specs=[pl.BlockSpec((k, tn), lambda i, rs: (0, i))] * 2,
            out_specs=pl.BlockSpec((k, tn), lambda i, rs: (0, i))),
        compiler_params=_cparams(("arbitrary",)),
        name="dispatch_rows",
    )(row_start, ids, ranks)
    return pos, row_token, tile_expert, tile_end[-1:].astype(jnp.int32)


def _moe(y, g_ffn, shift, scale, gates, group_fn, layer, w_router, b_router, w_gate, w_up, w_down,
         ws_gate, ws_up, ws_down, row_splits):
    n, d = y.shape
    t, ids, gate_w, ranks, counts, cum = _ada_norm(y, g_ffn, shift, scale, group_fn, w_router=w_router[layer],
                                                   b_router=b_router[layer], name="ada_norm_route")
    pos, row_token, tile_expert, n_valid = _dispatch_plan(
        ids, ranks, counts[:, 0].astype(jnp.int32), cum, MOE_ROW_TILE)
    base = _shared_expert(t, ws_gate[layer].astype(BF16), ws_up[layer].astype(BF16),
                          ws_down[layer].astype(BF16), y, gates, group_fn)
    n_tiles = tile_expert.shape[0]
    chunk_tiles = n_tiles // MOE_CHUNKS
    rows = None
    for c in range(MOE_CHUNKS):
        xs = t.at[row_token[c]].get(mode="promise_in_bounds")
        rows = _routed_experts(xs, tile_expert, n_valid, w_gate, w_up, w_down, layer,
                               c * chunk_tiles, n_tiles, rows)
    weights = gate_w.T
    outs = []
    for (r0, cnt) in row_splits:
        step = _largest_tile(cnt, COMBINE_CHUNK_ROWS, V7X_LANES)
        buf = None
        for a in range(0, cnt, step):
            picks = pos[:, r0 + a:r0 + a + step].reshape(-1)
            picked = rows.at[picks].get(mode="promise_in_bounds").reshape(TOP_K, step, d)
            buf = _combine(picked, weights, base, gates, group_fn, r0 + a, buf, cnt, a)
        outs.append(buf)
    return outs


def _rope_2d(n_tok, rot_dim):
    rows = n_tok // GRID_W
    row = jnp.broadcast_to(jnp.arange(rows, dtype=F32)[:, None], (rows, GRID_W)).reshape(-1)
    col = jnp.broadcast_to(jnp.arange(GRID_W, dtype=F32)[None, :], (rows, GRID_W)).reshape(-1)
    n_freq = rot_dim // 4
    inv = ROPE_THETA ** (-jnp.arange(n_freq, dtype=F32) / n_freq)
    ang = jnp.concatenate([row[:, None] * inv, col[:, None] * inv], axis=-1)
    return jnp.cos(ang), jnp.sin(ang)


def _rope_tables(rot_dim, n_ctx, dec_b, dec_s, n_tail):
    c, s = _rope_2d(dec_s, rot_dim)
    reps = V7X_LANES // rot_dim
    cos = jnp.concatenate([c, c] * reps, axis=1)
    sin = jnp.concatenate([-s, s] * reps, axis=1)
    if n_tail:
        fill = V7X_LANES - rot_dim
        cos = jnp.concatenate([c, c, jnp.ones((dec_s, fill), F32)], axis=1)
        sin = jnp.concatenate([-s, s, jnp.zeros((dec_s, fill), F32)], axis=1)
    ones = lambda r: jnp.ones((r, V7X_LANES), F32)
    zeros = lambda r: jnp.zeros((r, V7X_LANES), F32)
    cos = jnp.concatenate([ones(n_ctx)] + [cos] * dec_b + [ones(n_tail)], axis=0)
    sin = jnp.concatenate([zeros(n_ctx)] + [sin] * dec_b + [zeros(n_tail)], axis=0)
    return cos, sin


def _rotate_half(y, seg):
    half = seg // 2
    if seg == V7X_LANES:
        return pltpu.roll(y, half, axis=1)
    lane = lax.broadcasted_iota(jnp.int32, y.shape, 1)
    return jnp.where(lane % seg < half, pltpu.roll(y, V7X_LANES - half, axis=1), pltpu.roll(y, half, axis=1))


def _rotate_half_tail(t):
    half = MLA_ROPE // 2
    lane = lax.broadcasted_iota(jnp.int32, t.shape, 1)
    return jnp.where(lane < half, pltpu.roll(t, V7X_LANES - half, axis=1), pltpu.roll(t, half, axis=1))


def _segment_ones(seg):
    shift = seg.bit_length() - 1
    assert 1 << shift == seg
    r = lax.shift_right_logical(lax.broadcasted_iota(jnp.int32, (V7X_LANES, V7X_LANES), 0), shift)
    c = lax.shift_right_logical(lax.broadcasted_iota(jnp.int32, (V7X_LANES, V7X_LANES), 1), shift)
    return jnp.where(r == c, 1.0, 0.0).astype(BF16)


def _segment_sums(v, ones):
    hi = v.astype(BF16)
    lo = (v - hi.astype(F32)).astype(BF16)
    return jnp.dot(hi, ones, preferred_element_type=F32) + jnp.dot(lo, ones, preferred_element_type=F32)


def _seg_rms(x, seg, ones):
    return x * lax.rsqrt(_segment_sums(x * x, ones) / seg + EPS)


def _prep_body(proj_ref, *refs, plan):
    it = iter(refs)
    inputs = []
    for (_, _, kind, _) in plan:
        if kind in ("norm", "norm_f32"):
            inputs.append((next(it),))
        elif kind.startswith("heads"):
            inputs.append((next(it), next(it), next(it)))
        else:
            inputs.append(())
    outs = list(it)
    oi = 0
    for (col, width, kind, seg), ins in zip(plan, inputs):
        if kind in ("norm", "norm_f32"):
            (g_ref,) = ins
            x = proj_ref[:, col:col + width]
            y = x * lax.rsqrt(jnp.mean(x * x, axis=-1, keepdims=True) + EPS) * g_ref[...]
            outs[oi][...] = y.astype(outs[oi].dtype)
            oi += 1
        elif kind.startswith("heads"):
            g_ref, cos_ref, sin_ref = ins
            with_state = kind.endswith("+state")
            cos, sin, g = cos_ref[...], sin_ref[...], g_ref[...]
            ones = _segment_ones(seg)
            for c0 in range(0, width, V7X_LANES):
                y = _seg_rms(proj_ref[:, col + c0:col + c0 + V7X_LANES], seg, ones) * g
                if with_state:
                    outs[oi + 1][:, c0:c0 + V7X_LANES] = y
                y = y * cos + _rotate_half(y, seg) * sin
                outs[oi][:, c0:c0 + V7X_LANES] = y.astype(BF16)
            oi += 2 if with_state else 1
        else:
            outs[oi][...] = proj_ref[:, col:col + width].astype(BF16)
            oi += 1


def _prep(proj, plan, params, name):
    n, width_all = proj.shape
    tm = ROW_TILE
    in_specs = [pl.BlockSpec((tm, width_all), lambda i: (i, 0))]
    args = [proj]
    out_shape, out_specs = [], []
    for (col, width, kind, seg), ps in zip(plan, params):
        row_spec = pl.BlockSpec((tm, width), lambda i: (i, 0))
        if kind in ("norm", "norm_f32"):
            in_specs.append(pl.BlockSpec((1, width), lambda i: (0, 0)))
            args.append(ps[0].astype(F32).reshape(1, width))
            out_shape.append(jax.ShapeDtypeStruct((n, width), F32 if kind == "norm_f32" else BF16))
            out_specs.append(row_spec)
        elif kind.startswith("heads"):
            g, cos, sin = ps
            in_specs += [pl.BlockSpec((1, V7X_LANES), lambda i: (0, 0)),
                         pl.BlockSpec((tm, V7X_LANES), lambda i: (i, 0)),
                         pl.BlockSpec((tm, V7X_LANES), lambda i: (i, 0))]
            args += [jnp.tile(g.astype(F32), V7X_LANES // seg).reshape(1, V7X_LANES), cos, sin]
            out_shape.append(jax.ShapeDtypeStruct((n, width), BF16))
            out_specs.append(row_spec)
            if kind.endswith("+state"):
                out_shape.append(jax.ShapeDtypeStruct((n, width), F32))
                out_specs.append(row_spec)
        else:
            out_shape.append(jax.ShapeDtypeStruct((n, width), BF16))
            out_specs.append(row_spec)
    return pl.pallas_call(
        functools.partial(_prep_body, plan=plan),
        out_shape=tuple(out_shape),
        grid=(n // tm,),
        in_specs=in_specs,
        out_specs=tuple(out_specs),
        compiler_params=_cparams(("arbitrary",)),
        name=name,
    )(*args)


def _mla_q_body(x_ref, w_ref, g_ref, cos_ref, sin_ref, o_ref):
    acc = jnp.dot(x_ref[...], w_ref[...], preferred_element_type=F32)
    cos, sin = cos_ref[...], sin_ref[...]
    ones = _segment_ones(V7X_LANES)
    for h in range(MLA_HEADS):
        c0 = h * MLA_QK_PAD
        nope = acc[:, c0:c0 + MLA_NOPE]
        tail = acc[:, c0 + MLA_NOPE:c0 + MLA_QK_PAD]
        ss = _segment_sums(nope * nope, ones) + _segment_sums(tail * tail, ones)
        r = lax.rsqrt(ss / MLA_QK + EPS)
        o_ref[:, c0:c0 + MLA_NOPE] = (nope * r * g_ref[:, :MLA_NOPE]).astype(BF16)
        t = tail * r * g_ref[:, MLA_NOPE:]
        t = t * cos + _rotate_half_tail(t) * sin
        o_ref[:, c0 + MLA_NOPE:c0 + MLA_QK_PAD] = t.astype(BF16)


def _mla_q_up(q_lat, w_qb_p, g_qn_p, cos, sin):
    n, k = q_lat.shape
    width = w_qb_p.shape[1]
    tm = ROW_TILE
    return pl.pallas_call(
        _mla_q_body,
        out_shape=jax.ShapeDtypeStruct((n, width), BF16),
        grid=(n // tm,),
        in_specs=[pl.BlockSpec((tm, k), lambda i: (i, 0)),
                  pl.BlockSpec((k, width), lambda i: (0, 0)),
                  pl.BlockSpec((1, MLA_QK_PAD), lambda i: (0, 0)),
                  pl.BlockSpec((tm, V7X_LANES), lambda i: (i, 0)),
                  pl.BlockSpec((tm, V7X_LANES), lambda i: (i, 0))],
        out_specs=pl.BlockSpec((tm, width), lambda i: (i, 0)),
        compiler_params=_cparams(("arbitrary",)),
        name="mla_q_up",
    )(q_lat, w_qb_p, g_qn_p, cos, sin)


def _mla_kv_body(x_ref, w_ref, kr_ref, g_ref, cos_ref, sin_ref, k_ref, v_ref):
    acc = jnp.dot(x_ref[...].astype(BF16), w_ref[...], preferred_element_type=F32)
    cos, sin = cos_ref[...], sin_ref[...]
    kr = kr_ref[...]
    ones = _segment_ones(V7X_LANES)
    kr_ss = _segment_sums(kr * kr, ones)
    per = MLA_NOPE + MLA_V
    for h in range(MLA_HEADS):
        nope = acc[:, h * per:h * per + MLA_NOPE]
        r = lax.rsqrt((_segment_sums(nope * nope, ones) + kr_ss) / MLA_QK + EPS)
        c0 = h * MLA_QK_PAD
        k_ref[:, c0:c0 + MLA_NOPE] = (nope * r * g_ref[:, :MLA_NOPE]).astype(BF16)
        t = kr * r * g_ref[:, MLA_NOPE:]
        t = t * cos + _rotate_half_tail(t) * sin
        k_ref[:, c0 + MLA_NOPE:c0 + MLA_QK_PAD] = t.astype(BF16)
        v_ref[:, h * MLA_V:(h + 1) * MLA_V] = acc[:, h * per + MLA_NOPE:(h + 1) * per].astype(BF16)


def _mla_kv_up(c_kv, w_kvb, k_rope_p, g_kn_p, cos, sin):
    n, k = c_kv.shape
    tm = ROW_TILE
    return pl.pallas_call(
        _mla_kv_body,
        out_shape=(jax.ShapeDtypeStruct((n, MLA_HEADS * MLA_QK_PAD), BF16),
                   jax.ShapeDtypeStruct((n, MLA_HEADS * MLA_V), BF16)),
        grid=(n // tm,),
        in_specs=[pl.BlockSpec((tm, k), lambda i: (i, 0)),
                  pl.BlockSpec(w_kvb.shape, lambda i: (0, 0)),
                  pl.BlockSpec((tm, V7X_LANES), lambda i: (i, 0)),
                  pl.BlockSpec((1, MLA_QK_PAD), lambda i: (0, 0)),
                  pl.BlockSpec((tm, V7X_LANES), lambda i: (i, 0)),
                  pl.BlockSpec((tm, V7X_LANES), lambda i: (i, 0))],
        out_specs=(pl.BlockSpec((tm, MLA_HEADS * MLA_QK_PAD), lambda i: (i, 0)),
                   pl.BlockSpec((tm, MLA_HEADS * MLA_V), lambda i: (i, 0))),
        compiler_params=_cparams(("arbitrary",)),
        name="mla_kv_up",
    )(c_kv, w_kvb, k_rope_p, g_kn_p, cos, sin)


def _even_mixer(h, dims, cache, p, tables):
    (bp, sp, dec_b, dec_s, past) = dims
    n_ctx = bp * sp
    n = h.shape[0]
    (w_in, g_q, w_qb, g_kv, w_kvb, g_qn, g_kn, g_wq, g_wk, sink) = p
    ckv_c, krope_c, wk_c, wv_c = cache
    (cos_h, sin_h), (cos_m, sin_m) = tables["head"], tables["mla"]
    o1 = MLA_Q_RANK
    o2 = o1 + MLA_KV_RANK
    o3 = o2 + MLA_ROPE
    pad = (-(w_in.shape[1])) % V7X_LANES
    w_in_p = jnp.concatenate([w_in[:, :o2], w_in[:, o3:], w_in[:, o2:o3],
                              jnp.zeros((w_in.shape[0], pad), w_in.dtype)], axis=1).astype(BF16)
    proj = _matmul(h, w_in_p, out_dtype=F32, name="even_in_proj")
    c1 = o2 + WIN_HEADS * HEAD_DIM
    c2 = c1 + WIN_KV * HEAD_DIM
    c3 = c2 + WIN_KV * HEAD_DIM
    plan = ((0, o1, "norm", 0), (o1, MLA_KV_RANK, "norm_f32", 0),
            (o2, WIN_HEADS * HEAD_DIM, "heads", HEAD_DIM),
            (c1, WIN_KV * HEAD_DIM, "heads+state", HEAD_DIM),
            (c2, WIN_KV * HEAD_DIM, "cast", 0))
    q_lat, c_kv, wq_r, wk_r, wk, wv_b = _prep(
        proj, plan, ((g_q,), (g_kv,), (g_wq, cos_h, sin_h), (g_wk, cos_h, sin_h), ()), "even_prep")
    wv = proj[:n_ctx, c2:c3]
    k_rope_p = proj[:, c3:c3 + V7X_LANES]

    head_pad = ((0, 0), (0, 0), (0, MLA_QK_PAD - MLA_QK))
    w_qb_p = jnp.pad(w_qb.reshape(MLA_Q_RANK, MLA_HEADS, MLA_QK), head_pad)
    w_qb_p = w_qb_p.reshape(MLA_Q_RANK, MLA_HEADS * MLA_QK_PAD).astype(BF16)
    g_qn_p = jnp.pad(g_qn.astype(F32), (0, MLA_QK_PAD - MLA_QK)).reshape(1, MLA_QK_PAD)
    g_kn_p = jnp.pad(g_kn.astype(F32), (0, MLA_QK_PAD - MLA_QK)).reshape(1, MLA_QK_PAD)
    q_mla = _mla_q_up(q_lat, w_qb_p, g_qn_p, cos_m, sin_m)

    ckv_all = jnp.concatenate([c_kv, ckv_c.reshape(dec_b * past, MLA_KV_RANK)], axis=0)
    krope_cache = jnp.pad(krope_c.reshape(dec_b * past, MLA_ROPE), ((0, 0), (0, V7X_LANES - MLA_ROPE)))
    krope_all = jnp.concatenate([k_rope_p, krope_cache], axis=0)
    mk, mv = _mla_kv_up(ckv_all, w_kvb.astype(BF16), krope_all, g_kn_p, cos_m, sin_m)

    mla_scale = MLA_QK ** -0.5
    mla_cols = MLA_HEADS * MLA_V
    out = _attention(q_mla, 0, mk, mv, 0, batch=bp, seq=sp, nkv=MLA_HEADS, groups=1,
                     dq=MLA_QK_PAD, dv=MLA_V, scale=mla_scale, hb=MLA_HEADS, tq=Q_TILE_CTX,
                     out_shape=(n, mla_cols + WIN_HEADS * HEAD_DIM), name="mla_attn_ctx")
    out = _attention(q_mla, n_ctx, mk, mv, n_ctx, batch=dec_b, seq=dec_s, nkv=MLA_HEADS, groups=1,
                     dq=MLA_QK_PAD, dv=MLA_V, scale=mla_scale, hb=MLA_LAT_HEADS_PER_STEP, tq=Q_TILE_MLA_LAT,
                     k_ctx=mk, v_ctx=mv, kc_off=n, t_ctx=past, out=out, out_off=(n_ctx, 0),
                     name="mla_attn_lat")

    grp = WIN_HEADS // WIN_KV
    sink_b = jnp.broadcast_to(sink.astype(F32)[:, None], (WIN_HEADS, V7X_LANES))
    win_scale = HEAD_DIM ** -0.5
    out = _attention(wq_r, 0, wk_r, wv_b, 0, batch=bp, seq=sp, nkv=WIN_KV, groups=grp,
                     dq=HEAD_DIM, dv=HEAD_DIM, scale=win_scale, hb=WIN_KV, tq=Q_TILE_CTX, sink=sink_b,
                     out=out, out_off=(0, mla_cols), name="win_attn_ctx")
    out = _attention(wq_r, n_ctx, wk_r, wv_b, n_ctx, batch=dec_b, seq=dec_s, nkv=WIN_KV, groups=grp,
                     dq=HEAD_DIM, dv=HEAD_DIM, scale=win_scale, hb=WIN_KV, tq=Q_TILE_GQA_LAT,
                     k_ctx=wk_c.reshape(dec_b * past, WIN_KV * HEAD_DIM).astype(BF16),
                     v_ctx=wv_c.reshape(dec_b * past, WIN_KV * HEAD_DIM).astype(BF16),
                     kc_off=0, t_ctx=past, sink=sink_b, window=WINDOW,
                     out=out, out_off=(n_ctx, mla_cols), name="win_attn_lat")
    state = (c_kv[:n_ctx].reshape(bp, 1, sp, MLA_KV_RANK),
             k_rope_p[:n_ctx, :MLA_ROPE].reshape(bp, 1, sp, MLA_ROPE),
             wk[:n_ctx].reshape(bp, 1, sp, WIN_KV, HEAD_DIM),
             wv.reshape(bp, 1, sp, WIN_KV, HEAD_DIM))
    return out, state


def _odd_mixer(h, dims, cache, p, lam_init, layer, tables):
    (bp, sp, dec_b, dec_s, past) = dims
    n_ctx = bp * sp
    (w_in, g_dq, g_dk, lq1, lk1, lq2, lk2, g_sub, g_aq, g_ak) = p
    dk_c, dv_c, ak_c, av_c = cache
    (cos_h, sin_h), (cos_d, sin_d) = tables["head"], tables["diff"]
    o1 = DIFF_HEADS * HEAD_DIM
    o2 = o1 + DIFF_KV * HEAD_DIM
    o3 = o2 + DIFF_KV * HEAD_DIM
    o4 = o3 + AX_HEADS * HEAD_DIM
    o5 = o4 + AX_KV * HEAD_DIM
    proj = _matmul(h, w_in, layer=layer, out_dtype=F32, name="odd_in_proj")
    plan = ((0, o1, "heads", DIFF_D), (o1, o2 - o1, "heads+state", DIFF_D), (o2, o3 - o2, "cast", 0),
            (o3, o4 - o3, "heads", HEAD_DIM), (o4, o5 - o4, "heads+state", HEAD_DIM),
            (o5, AX_KV * HEAD_DIM, "cast", 0))
    dq_r, dk_r, dk, dv_b, aq_r, ak_r, ak, av_b = _prep(
        proj, plan, ((g_dq, cos_d, sin_d), (g_dk, cos_d, sin_d), (), (g_aq, cos_h, sin_h),
                     (g_ak, cos_h, sin_h), ()), "odd_prep")
    dv = proj[:n_ctx, o2:o3]
    av = proj[:n_ctx, o5:]

    lam = (jnp.exp(jnp.sum(lq1.astype(F32) * lk1.astype(F32)))
           - jnp.exp(jnp.sum(lq2.astype(F32) * lk2.astype(F32))) + lam_init)
    lam_b = jnp.broadcast_to(lam.astype(F32), (1, HEAD_DIM))
    g_sub_b = g_sub.astype(F32).reshape(1, HEAD_DIM)
    grp = DIFF_HEADS // DIFF_KV
    diff_kw = dict(nkv=DIFF_KV, groups=grp, dq=HEAD_DIM, dv=HEAD_DIM, scale=DIFF_D ** -0.5, hb=DIFF_KV,
                   lam=lam_b, g_sub=g_sub_b, diff_post_scale=1.0 - lam_init)
    n = n_ctx + dec_b * dec_s
    diff_cols = DIFF_HEADS * HEAD_DIM
    out = _attention(dq_r, 0, dk_r, dv_b, 0, batch=bp, seq=sp, tq=Q_TILE_CTX,
                     out_shape=(n, diff_cols + AX_HEADS * HEAD_DIM), name="diff_attn_ctx", **diff_kw)
    out = _attention(dq_r, n_ctx, dk_r, dv_b, n_ctx, batch=dec_b, seq=dec_s, tq=Q_TILE_DIFF_LAT,
                     k_ctx=dk_c.reshape(dec_b * past, DIFF_KV * HEAD_DIM).astype(BF16),
                     v_ctx=dv_c.reshape(dec_b * past, DIFF_KV * HEAD_DIM).astype(BF16),
                     kc_off=0, t_ctx=past, out=out, out_off=(n_ctx, 0), name="diff_attn_lat", **diff_kw)
    agrp = AX_HEADS // AX_KV
    ax_kw = dict(nkv=AX_KV, groups=agrp, dq=HEAD_DIM, dv=HEAD_DIM, scale=HEAD_DIM ** -0.5, hb=AX_KV)
    out = _attention(aq_r, 0, ak_r, av_b, 0, batch=bp, seq=sp, tq=Q_TILE_CTX,
                     out=out, out_off=(0, diff_cols), name="ax_attn_ctx", **ax_kw)
    out = _attention(aq_r, n_ctx, ak_r, av_b, n_ctx, batch=dec_b, seq=dec_s, tq=Q_TILE_GQA_LAT,
                     k_ctx=ak_c.reshape(dec_b * past, AX_KV * HEAD_DIM).astype(BF16),
                     v_ctx=av_c.reshape(dec_b * past, AX_KV * HEAD_DIM).astype(BF16),
                     kc_off=0, t_ctx=past, out=out, out_off=(n_ctx, diff_cols), name="ax_attn_lat", **ax_kw)
    state = (dk[:n_ctx].reshape(bp, 1, sp, DIFF_KV, 2, DIFF_D),
             dv.reshape(bp, 1, sp, DIFF_KV, HEAD_DIM),
             ak[:n_ctx].reshape(bp, 1, sp, AX_KV, HEAD_DIM),
             av.reshape(bp, 1, sp, AX_KV, HEAD_DIM))
    return out, state


def kernel(x_prompt, x_sample, cache_mla_ckv, cache_mla_krope, cache_win_k, cache_win_v, cache_diff_k, cache_diff_v, cache_ax_k, cache_ax_v, c, c_ctx, w_mod, b_mod, g_norm_mix, g_norm_ffn, w_in_even, g_mla_q, w_mla_qb, g_mla_kv, w_mla_kvb, g_mla_qn, g_mla_kn, g_win_qn, g_win_kn, win_sink, w_out_even, w_in_odd, g_diff_qn, g_diff_kn, diff_lq1, diff_lk1, diff_lq2, diff_lk2, g_diff_sub, g_ax_qn, g_ax_kn, w_out_odd, w_router, b_router, w_exp_gate, w_exp_up, w_exp_down, w_sh_gate, w_sh_up, w_sh_down):
    bp, sp, d = x_prompt.shape
    dec_b, dec_s, _ = x_sample.shape
    depth = w_mod.shape[0]
    n_ctx = bp * sp
    n = n_ctx + dec_b * dec_s
    dims = (bp, sp, dec_b, dec_s, cache_mla_ckv.shape[2])
    group_fn = lambda tm: _group_index_fn(tm, n_ctx, dec_s)
    n_groups = 1 + dec_b

    y = jnp.concatenate([x_prompt.reshape(n_ctx, d), x_sample.reshape(dec_b * dec_s, d)], axis=0)

    cond = jnp.concatenate([c_ctx[None], c, jnp.zeros((V7X_SUBLANES - n_groups % V7X_SUBLANES, d), F32)], axis=0)
    cond = jax.nn.silu(cond)

    past = cache_mla_ckv.shape[2]
    tables = {"head": _rope_tables(HEAD_DIM, n_ctx, dec_b, dec_s, 0),
              "diff": _rope_tables(DIFF_D, n_ctx, dec_b, dec_s, 0),
              "mla": _rope_tables(MLA_ROPE, n_ctx, dec_b, dec_s, dec_b * past)}

    states_even, states_odd = [], []
    for l in range(depth):
        i = l // 2
        mod = _matmul(cond, w_mod, layer=l, out_dtype=F32, bias=b_mod.reshape(depth, 1, 6 * d),
                      name="modulation")
        mod = mod[:n_groups].reshape(n_groups, 6, 1, d)
        sh1, sc1, g1, sh2, sc2, g2 = (mod[:, j] for j in range(6))
        h = _ada_norm(y, g_norm_mix[l][None], sh1, sc1, group_fn, name="ada_norm_mix")
        if l % 2 == 0:
            pe = (w_in_even[i], g_mla_q[i], w_mla_qb[i], g_mla_kv[i], w_mla_kvb[i], g_mla_qn[i],
                  g_mla_kn[i], g_win_qn[i], g_win_kn[i], win_sink[i])
            cache = (cache_mla_ckv[:, i], cache_mla_krope[:, i], cache_win_k[:, i], cache_win_v[:, i])
            out, state = _even_mixer(h, dims, cache, pe, tables)
            states_even.append(state)
            w_out = w_out_even
        else:
            po = (w_in_odd, g_diff_qn[i], g_diff_kn[i], diff_lq1[i], diff_lk1[i], diff_lq2[i],
                  diff_lk2[i], g_diff_sub[i], g_ax_qn[i], g_ax_kn[i])
            cache = (cache_diff_k[:, i], cache_diff_v[:, i], cache_ax_k[:, i], cache_ax_v[:, i])
            lam_init = 0.8 - 0.6 * math.exp(-0.3 * l)
            out, state = _odd_mixer(h, dims, cache, po, lam_init, i, tables)
            states_odd.append(state)
            w_out = w_out_odd
        y = _matmul(out, w_out[i].astype(BF16), out_dtype=F32, resid=y, gates=g1, group_fn=group_fn,
                    name="mixer_out_proj")
        splits = [(0, n)] if l + 1 < depth else [(0, n_ctx), (n_ctx, n - n_ctx)]
        outs = _moe(y, g_norm_ffn[l][None], sh2, sc2, g2, group_fn, l, w_router, b_router,
                    w_exp_gate, w_exp_up, w_exp_down, w_sh_gate, w_sh_up, w_sh_down, splits)
        y = outs[0]

    yp = outs[0].reshape(bp, sp, d)
    ys = outs[1].reshape(dec_b, dec_s, d)
    even = tuple(jnp.concatenate([s[j] for s in states_even], axis=1) for j in range(4))
    odd = tuple(jnp.concatenate([s[j] for s in states_odd], axis=1) for j in range(4))
    return (yp, ys) + even + odd
```
